```python
import math
import jax, jax.numpy as jnp
from jax import lax
import numpy as np

D_MODEL = 4096
BATCH = 4
SEQ = 2048
DEPTH = 2

CHUNK = 64
N_EVEN = (DEPTH + 1) // 2
N_ODD = DEPTH // 2
DEEPNORM_ALPHA = (2 * DEPTH) ** 0.25
DEEPNORM_BETA = (8 * DEPTH) ** -0.25
LN_EPS = 1e-5
RMS_EPS = 1e-6

MIX_WIDTH = D_MODEL
GM_WIDTH = MIX_WIDTH // 2
GM_GROUPS = 8
GM_GROUP_DIM = GM_WIDTH // GM_GROUPS
GM_BLOCK = 128
SSM_WIDTH = MIX_WIDTH - GM_WIDTH
SSM_GROUP_DIM = 16
SSM_GROUPS = SSM_WIDTH // SSM_GROUP_DIM
SSM_STATE = 64
HY_IN_WIDTH = 2 * GM_WIDTH + SSM_WIDTH
DT_MIN = 1e-3
DT_MAX = 1e-1

MLA_HEADS = 32
MLA_Q_RANK = 1024
MLA_KV_RANK = 512
MLA_NOPE = 128
MLA_ROPE = 64
MLA_V = 128
MLA_IN_WIDTH = MLA_Q_RANK + MLA_KV_RANK + MLA_ROPE
ROPE_THETA = 10000.0
Q_BLOCK = 128

N_EXPERTS = 32
TOP_K = 4
D_EXPERT = D_MODEL // 4
SWIGLU_LIMIT = 7.0
SWIGLU_ALPHA = 1.702
MOE_BLOCK = 128

kernel_name = "hybrid_gmlp_s5_mla_moe_deepnorm"


def layer_norm(x, g, b):
    xf = x.astype(jnp.float32)
    mu = jnp.mean(xf, -1, keepdims=True)
    var = jnp.mean(jnp.square(xf - mu), -1, keepdims=True)
    return ((xf - mu) * lax.rsqrt(var + LN_EPS) * g + b).astype(x.dtype)


def rms_norm(x, g):
    xf = x.astype(jnp.float32)
    return (xf * lax.rsqrt(jnp.mean(xf * xf, -1, keepdims=True) + RMS_EPS) * g).astype(x.dtype)


def chunk_causal_mask(q_idx, k_idx):
    return (k_idx[None, :] // CHUNK) <= (q_idx[:, None] // CHUNK)


def chunked_spatial_gating(z, ln_g, ln_b, w_s, b_s):
    bsz, seq, _ = z.shape
    u, v = jnp.split(z, 2, axis=-1)
    nblk = seq // GM_BLOCK
    v = v.reshape(bsz, nblk, GM_BLOCK, GM_GROUPS, GM_GROUP_DIM)
    v = layer_norm(v, ln_g, ln_b)
    idx = jnp.arange(GM_BLOCK)
    w = jnp.where(chunk_causal_mask(idx, idx)[None], w_s, 0.0)
    s = jnp.einsum('gij,bnjgc->bnigc', w, v) + b_s.T[:, :, None]
    return u * s.reshape(bsz, seq, GM_WIDTH)


def s5_mixer(u, lam_re, lam_im, log_dt, b_re, b_im, c_re, c_im, d_skip, w_glu, b_glu):
    f32 = jnp.float32
    bsz, seq, _ = u.shape
    uf = u.astype(f32).reshape(bsz, seq, SSM_GROUPS, SSM_GROUP_DIM)
    lam_re = lam_re.astype(f32)
    lam_im = lam_im.astype(f32)
    b_re = b_re.astype(f32)
    b_im = b_im.astype(f32)
    dt = jnp.exp(log_dt.astype(f32))[:, None]
    mag = jnp.exp(lam_re * dt)
    ab_re = mag * jnp.cos(lam_im * dt)
    ab_im = mag * jnp.sin(lam_im * dt)
    den = lam_re * lam_re + lam_im * lam_im
    num_re = ab_re - 1.0
    coef_re = (num_re * lam_re + ab_im * lam_im) / den
    coef_im = (ab_im * lam_re - num_re * lam_im) / den
    bb_re = coef_re[..., None] * b_re - coef_im[..., None] * b_im
    bb_im = coef_re[..., None] * b_im + coef_im[..., None] * b_re
    x_re = jnp.einsum('gnp,bsgp->bsgn', bb_re, uf)
    x_im = jnp.einsum('gnp,bsgp->bsgn', bb_im, uf)
    a_re = jnp.broadcast_to(ab_re[None, None], (1, seq, SSM_GROUPS, SSM_STATE))
    a_im = jnp.broadcast_to(ab_im[None, None], (1, seq, SSM_GROUPS, SSM_STATE))

    def combine(e1, e2):
        a1r, a1i, b1r, b1i = e1
        a2r, a2i, b2r, b2i = e2
        return (a2r * a1r - a2i * a1i,
                a2r * a1i + a2i * a1r,
                a2r * b1r - a2i * b1i + b2r,
                a2r * b1i + a2i * b1r + b2i)

    _, _, h_re, h_im = lax.associative_scan(combine, (a_re, a_im, x_re, x_im), axis=1)
    y = (jnp.einsum('gpn,bsgn->bsgp', c_re.astype(f32), h_re)
         - jnp.einsum('gpn,bsgn->bsgp', c_im.astype(f32), h_im)
         + d_skip.astype(f32) * uf)
    y = jax.nn.gelu(y.reshape(bsz, seq, SSM_WIDTH)).astype(u.dtype)
    return y * jax.nn.sigmoid(y @ w_glu + b_glu)


def apply_rope(x, cos, sin):
    x1, x2 = jnp.split(x, 2, axis=-1)
    return jnp.concatenate([x1 * cos - x2 * sin, x2 * cos + x1 * sin], axis=-1)


def mla_mixer(x, positions, w_in, q_norm_g, kv_norm_g, w_uq, w_ukv, w_o):
    bsz, seq, _ = x.shape
    c = x @ w_in
    cq, ckv, k_rope = jnp.split(c, [MLA_Q_RANK, MLA_Q_RANK + MLA_KV_RANK], axis=-1)
    q = (rms_norm(cq, q_norm_g) @ w_uq).reshape(bsz, seq, MLA_HEADS, MLA_NOPE + MLA_ROPE)
    kv = (rms_norm(ckv, kv_norm_g) @ w_ukv).reshape(bsz, seq, MLA_HEADS, MLA_NOPE + MLA_V)
    q_nope, q_rope = jnp.split(q, [MLA_NOPE], axis=-1)
    k_nope, v = jnp.split(kv, [MLA_NOPE], axis=-1)
    inv_freq = ROPE_THETA ** (-jnp.arange(0, MLA_ROPE, 2, dtype=jnp.float32) / MLA_ROPE)
    ang = positions.astype(jnp.float32)[..., None] * inv_freq
    cos, sin = jnp.cos(ang), jnp.sin(ang)
    q_rope = apply_rope(q_rope.astype(jnp.float32), cos[:, :, None], sin[:, :, None]).astype(x.dtype)
    k_rope = apply_rope(k_rope.astype(jnp.float32), cos, sin).astype(x.dtype)
    scale = (MLA_NOPE + MLA_ROPE) ** -0.5
    outs = []
    for blk in range(seq // Q_BLOCK):
        q0, q1 = blk * Q_BLOCK, (blk + 1) * Q_BLOCK
        s = (jnp.einsum('bqhd,bkhd->bhqk', q_nope[:, q0:q1], k_nope[:, :q1])
             + jnp.einsum('bqhr,bkr->bhqk', q_rope[:, q0:q1], k_rope[:, :q1]))
        s = s.astype(jnp.float32) * scale
        mask = chunk_causal_mask(jnp.arange(q0, q1), jnp.arange(q1))
        p = jax.nn.softmax(jnp.where(mask, s, -jnp.inf), axis=-1).astype(v.dtype)
        outs.append(jnp.einsum('bhqk,bkhd->bqhd', p, v[:, :q1]))
    o = jnp.concatenate(outs, axis=1).reshape(bsz, seq, MLA_HEADS * MLA_V)
    return o @ w_o


def moe_ffn(x, w_router, b_router, w_gu, b_gu, w_down, b_down):
    bsz, seq, d = x.shape
    n_tok = bsz * seq
    xt = x.reshape(n_tok, d)
    logits = (xt @ w_router).astype(jnp.float32) + b_router
    top_val, top_idx = lax.top_k(logits, TOP_K)
    gates = jax.nn.softmax(top_val, axis=-1)
    n_assign = n_tok * TOP_K
    flat_e = top_idx.reshape(-1)
    flat_tok = jnp.arange(n_assign, dtype=jnp.int32) // TOP_K
    flat_gate = gates.reshape(-1)
    counts = jnp.bincount(flat_e, length=N_EXPERTS)
    padded = (counts + MOE_BLOCK - 1) // MOE_BLOCK * MOE_BLOCK
    start = jnp.cumsum(counts) - counts
    pad_end = jnp.cumsum(padded)
    pad_start = pad_end - padded
    order = jnp.argsort(flat_e)
    sorted_e = flat_e[order]
    dest = pad_start[sorted_e] + jnp.arange(n_assign, dtype=jnp.int32) - start[sorted_e]
    n_rows = n_assign + N_EXPERTS * MOE_BLOCK
    n_blocks = n_rows // MOE_BLOCK
    row_tok = jnp.full((n_rows,), n_tok, jnp.int32).at[dest].set(flat_tok[order])
    row_gate = jnp.zeros((n_rows,), jnp.float32).at[dest].set(flat_gate[order])
    blk_exp = jnp.minimum(
        jnp.searchsorted(pad_end, jnp.arange(n_blocks) * MOE_BLOCK, side='right'), N_EXPERTS - 1)
    x_pad = jnp.concatenate([xt, jnp.zeros((1, d), xt.dtype)], axis=0)

    def expert_block(args):
        e, tok, g = args
        h = x_pad[tok] @ w_gu[e] + b_gu[e]
        h_glu, h_lin = jnp.split(h, 2, axis=-1)
        h_glu = jnp.minimum(h_glu, SWIGLU_LIMIT)
        h_lin = jnp.clip(h_lin, -SWIGLU_LIMIT, SWIGLU_LIMIT)
        act = h_glu * jax.nn.sigmoid(SWIGLU_ALPHA * h_glu) * (h_lin + 1.0)
        y = act @ w_down[e] + b_down[e]
        return y * g[:, None].astype(y.dtype)

    y_rows = lax.map(expert_block, (blk_exp, row_tok.reshape(n_blocks, MOE_BLOCK),
                                    row_gate.reshape(n_blocks, MOE_BLOCK)))
    y = jnp.zeros((n_tok + 1, d), x.dtype).at[row_tok].add(y_rows.reshape(n_rows, d).astype(x.dtype))
    return y[:n_tok].reshape(bsz, seq, d)


def setup_inputs(seed: int = 0) -> dict:
    key = jax.random.key(seed)
    k = jax.random.split(key, 32)
    f32 = jnp.float32

    def nrm(kk, shape, scale):
        return jax.random.normal(kk, shape, f32) * scale

    x = nrm(k[0], (BATCH, SEQ, D_MODEL), 1.0)
    offsets = jax.random.randint(k[1], (BATCH, 1), 0, 64) * CHUNK
    positions = (offsets + jnp.arange(SEQ)[None, :]).astype(jnp.int32)
    ln_g = 1.0 + nrm(k[2], (DEPTH, 2, D_MODEL), 0.02)
    ln_b = nrm(k[3], (DEPTH, 2, D_MODEL), 0.02)
    hy_w_in = nrm(k[4], (N_EVEN, D_MODEL, HY_IN_WIDTH), D_MODEL ** -0.5)
    hy_w_out = nrm(k[5], (N_EVEN, MIX_WIDTH, D_MODEL), MIX_WIDTH ** -0.5 * DEEPNORM_BETA)
    gm_ln_g = 1.0 + nrm(k[6], (N_EVEN, GM_GROUPS, GM_GROUP_DIM), 0.02)
    gm_ln_b = nrm(k[7], (N_EVEN, GM_GROUPS, GM_GROUP_DIM), 0.02)
    gm_w_s = nrm(k[8], (N_EVEN, GM_GROUPS, GM_BLOCK, GM_BLOCK), GM_BLOCK ** -0.5)
    gm_b_s = 1.0 + nrm(k[9], (N_EVEN, GM_GROUPS, GM_BLOCK), 0.02)
    ssm_lam_re = -0.5 + nrm(k[10], (N_EVEN, SSM_GROUPS, SSM_STATE), 0.01)
    ssm_lam_im = (jnp.pi * jnp.arange(SSM_STATE, dtype=f32))[None, None] + nrm(
        k[11], (N_EVEN, SSM_GROUPS, SSM_STATE), 0.01)
    ssm_log_dt = jax.random.uniform(k[12], (N_EVEN, SSM_GROUPS), f32,
                                    minval=math.log(DT_MIN), maxval=math.log(DT_MAX))
    ssm_b_re = nrm(k[13], (N_EVEN, SSM_GROUPS, SSM_STATE, SSM_GROUP_DIM), (2 * SSM_GROUP_DIM) ** -0.5)
    ssm_b_im = nrm(k[14], (N_EVEN, SSM_GROUPS, SSM_STATE, SSM_GROUP_DIM), (2 * SSM_GROUP_DIM) ** -0.5)
    ssm_c_re = nrm(k[15], (N_EVEN, SSM_GROUPS, SSM_GROUP_DIM, SSM_STATE), SSM_STATE ** -0.5)
    ssm_c_im = nrm(k[16], (N_EVEN, SSM_GROUPS, SSM_GROUP_DIM, SSM_STATE), SSM_STATE ** -0.5)
    ssm_d = nrm(k[17], (N_EVEN, SSM_GROUPS, SSM_GROUP_DIM), 1.0)
    ssm_w_glu = nrm(k[18], (N_EVEN, SSM_WIDTH, SSM_WIDTH), SSM_WIDTH ** -0.5)
    ssm_b_glu = nrm(k[19], (N_EVEN, SSM_WIDTH), 0.02)
    mla_w_in = nrm(k[20], (N_ODD, D_MODEL, MLA_IN_WIDTH), D_MODEL ** -0.5)
    mla_q_norm_g = 1.0 + nrm(k[21], (N_ODD, MLA_Q_RANK), 0.02)
    mla_kv_norm_g = 1.0 + nrm(k[22], (N_ODD, MLA_KV_RANK), 0.02)
    mla_w_uq = nrm(k[23], (N_ODD, MLA_Q_RANK, MLA_HEADS * (MLA_NOPE + MLA_ROPE)), MLA_Q_RANK ** -0.5)
    mla_w_ukv = nrm(k[24], (N_ODD, MLA_KV_RANK, MLA_HEADS * (MLA_NOPE + MLA_V)), MLA_KV_RANK ** -0.5)
    mla_w_o = nrm(k[25], (N_ODD, MLA_HEADS * MLA_V, D_MODEL), (MLA_HEADS * MLA_V) ** -0.5 * DEEPNORM_BETA)
    moe_w_router = nrm(k[26], (DEPTH, D_MODEL, N_EXPERTS), D_MODEL ** -0.5)
    moe_b_router = nrm(k[27], (DEPTH, N_EXPERTS), 0.01)
    moe_w_gu = nrm(k[28], (DEPTH, N_EXPERTS, D_MODEL, 2 * D_EXPERT), D_MODEL ** -0.5)
    moe_b_gu = nrm(k[29], (DEPTH, N_EXPERTS, 2 * D_EXPERT), 0.02)
    moe_w_down = nrm(k[30], (DEPTH, N_EXPERTS, D_EXPERT, D_MODEL), D_EXPERT ** -0.5 * DEEPNORM_BETA)
    moe_b_down = nrm(k[31], (DEPTH, N_EXPERTS, D_MODEL), 0.02 * DEEPNORM_BETA)
    return {
        "x": x, "positions": positions, "ln_g": ln_g, "ln_b": ln_b,
        "hy_w_in": hy_w_in, "hy_w_out": hy_w_out,
        "gm_ln_g": gm_ln_g, "gm_ln_b": gm_ln_b, "gm_w_s": gm_w_s, "gm_b_s": gm_b_s,
        "ssm_lam_re": ssm_lam_re, "ssm_lam_im": ssm_lam_im, "ssm_log_dt": ssm_log_dt,
        "ssm_b_re": ssm_b_re, "ssm_b_im": ssm_b_im, "ssm_c_re": ssm_c_re, "ssm_c_im": ssm_c_im,
        "ssm_d": ssm_d, "ssm_w_glu": ssm_w_glu, "ssm_b_glu": ssm_b_glu,
        "mla_w_in": mla_w_in, "mla_q_norm_g": mla_q_norm_g, "mla_kv_norm_g": mla_kv_norm_g,
        "mla_w_uq": mla_w_uq, "mla_w_ukv": mla_w_ukv, "mla_w_o": mla_w_o,
        "moe_w_router": moe_w_router, "moe_b_router": moe_b_router,
        "moe_w_gu": moe_w_gu, "moe_b_gu": moe_b_gu, "moe_w_down": moe_w_down, "moe_b_down": moe_b_down,
    }


def reference(x, positions, ln_g, ln_b, hy_w_in, hy_w_out, gm_ln_g, gm_ln_b, gm_w_s, gm_b_s,
              ssm_lam_re, ssm_lam_im, ssm_log_dt, ssm_b_re, ssm_b_im, ssm_c_re, ssm_c_im,
              ssm_d, ssm_w_glu, ssm_b_glu, mla_w_in, mla_q_norm_g, mla_kv_norm_g, mla_w_uq,
              mla_w_ukv, mla_w_o, moe_w_router, moe_b_router, moe_w_gu, moe_b_gu,
              moe_w_down, moe_b_down):
    h = x
    for layer in range(DEPTH):
        i = layer // 2
        if layer % 2 == 0:
            z = h @ hy_w_in[i]
            z_gm, z_ssm = jnp.split(z, [2 * GM_WIDTH], axis=-1)
            y_gm = chunked_spatial_gating(jax.nn.gelu(z_gm), gm_ln_g[i], gm_ln_b[i],
                                          gm_w_s[i], gm_b_s[i])
            y_ssm = s5_mixer(z_ssm, ssm_lam_re[i], ssm_lam_im[i], ssm_log_dt[i],
                             ssm_b_re[i], ssm_b_im[i], ssm_c_re[i], ssm_c_im[i],
                             ssm_d[i], ssm_w_glu[i], ssm_b_glu[i])
            mix = jnp.concatenate([y_gm, y_ssm], axis=-1) @ hy_w_out[i]
        else:
            mix = mla_mixer(h, positions, mla_w_in[i], mla_q_norm_g[i], mla_kv_norm_g[i],
                            mla_w_uq[i], mla_w_ukv[i], mla_w_o[i])
        h = layer_norm(DEEPNORM_ALPHA * h + mix, ln_g[layer, 0], ln_b[layer, 0])
        ffn = moe_ffn(h, moe_w_router[layer], moe_b_router[layer], moe_w_gu[layer],
                      moe_b_gu[layer], moe_w_down[layer], moe_b_down[layer])
        h = layer_norm(DEEPNORM_ALPHA * h + ffn, ln_g[layer, 1], ln_b[layer, 1])
    return h
```

```python
import functools
import math

import numpy as np
import jax
import jax.numpy as jnp
from jax import lax
from jax.experimental import pallas as pl
from jax.experimental.pallas import tpu as pltpu

F32 = jnp.float32
BF16 = jnp.bfloat16

D_MODEL = 4096
DEPTH = 2
CHUNK = 64
DEEPNORM_ALPHA = (2 * DEPTH) ** 0.25
LN_EPS = 1e-5
RMS_EPS = 1e-6

GM_WIDTH = 2048
GM_GROUPS = 8
GM_GROUP_DIM = 256
GM_BLOCK = 128
SSM_WIDTH = 2048
SSM_P = 16
SSM_GROUPS = 128
SSM_N = 64
SSM_T = 16
SSM_GT = 8

MLA_HEADS = 32
MLA_Q_RANK = 1024
MLA_KV_RANK = 512
MLA_NOPE = 128
MLA_ROPE = 64
MLA_V = 128
ROPE_THETA = 10000.0

N_EXPERTS = 32
TOP_K = 4
D_EXPERT = 1024
SWIGLU_LIMIT = 7.0
SWIGLU_ALPHA = 1.702

VMEM_LIMIT_BYTES = 56 * 1024 * 1024

MOE_TM = 1024
MOE_SB = 256
MOE_TH = 256
MOE_TN = 512
MOE_T1 = D_EXPERT // MOE_TH
MOE_T2 = D_MODEL // MOE_TN


def _params(*sem):
    return pltpu.CompilerParams(dimension_semantics=sem, vmem_limit_bytes=VMEM_LIMIT_BYTES)


def _dot(a, b):
    return jnp.dot(a, b, preferred_element_type=F32)


def _dot_nt(a, b):
    return lax.dot_general(a, b, (((1,), (1,)), ((), ())), preferred_element_type=F32)


def _mm_kernel(*refs, prologue, epilogue, n_x):
    it = iter(refs)
    x_refs = [next(it) for _ in range(n_x)]
    w_ref = next(it)
    g_ref = next(it) if prologue == "rms" else None
    b_ref = next(it) if epilogue == "glu" else None
    y_ref = next(it) if epilogue == "glu" else None
    o_ref = next(it)
    xs_ref = next(it) if prologue == "rms" else None

    if prologue == "rms":
        @pl.when(pl.program_id(1) == 0)
        def _():
            xf = x_refs[0][...].astype(F32)
            ms = jnp.mean(xf * xf, axis=-1, keepdims=True)
            xs_ref[...] = (xf * lax.rsqrt(ms + RMS_EPS) * g_ref[...]).astype(BF16)
        xs = [xs_ref[...]]
    else:
        xs = [r[...].astype(BF16) for r in x_refs]

    acc = None
    k0 = 0
    for x in xs:
        kk = x.shape[1]
        part = _dot(x, w_ref[k0:k0 + kk, :].astype(BF16))
        acc = part if acc is None else acc + part
        k0 += kk
    if epilogue == "gelu":
        acc = jax.nn.gelu(acc)
    elif epilogue == "glu":
        acc = y_ref[...].astype(F32) * jax.nn.sigmoid(acc + b_ref[...])
    o_ref[...] = acc.astype(o_ref.dtype)


def _mm(xs, w, *, col0=0, n_cols=None, tm, tn, out_dtype, prologue=None, gain=None,
        epilogue=None, bias=None, mul=None, name):
    xs = [(x, x.shape[1], 0) if not isinstance(x, tuple) else x for x in xs]
    m = xs[0][0].shape[0]
    k_total = w.shape[0]
    n_cols = w.shape[1] - col0 if n_cols is None else n_cols
    tn = min(tn, n_cols)
    assert m % tm == 0 and n_cols % tn == 0 and col0 % tn == 0
    cb0 = col0 // tn
    assert sum(kw for _, kw, _ in xs) == k_total
    in_specs = [pl.BlockSpec((tm, kw), functools.partial(lambda i, j, cb: (i, cb), cb=cb))
                for _, kw, cb in xs]
    in_specs.append(pl.BlockSpec((k_total, tn), lambda i, j: (0, cb0 + j)))
    args = [x for x, _, _ in xs] + [w]
    scratch = []
    if prologue == "rms":
        in_specs.append(pl.BlockSpec((1, k_total), lambda i, j: (0, 0)))
        args.append(gain.reshape(1, k_total))
        scratch.append(pltpu.VMEM((tm, k_total), BF16))
    if epilogue == "glu":
        in_specs.append(pl.BlockSpec((1, tn), lambda i, j: (0, j)))
        in_specs.append(pl.BlockSpec((tm, tn), lambda i, j: (i, j)))
        args += [bias.reshape(1, n_cols), mul]
    return pl.pallas_call(
        functools.partial(_mm_kernel, prologue=prologue, epilogue=epilogue, n_x=len(xs)),
        grid=(m // tm, n_cols // tn),
        in_specs=in_specs,
        out_specs=pl.BlockSpec((tm, tn), lambda i, j: (i, j)),
        out_shape=jax.ShapeDtypeStruct((m, n_cols), out_dtype),
        scratch_shapes=scratch,
        compiler_params=_params("arbitrary", "arbitrary"),
        name=name,
    )(*args)


def _gating_kernel(u_ref, v_ref, g_ref, b_ref, ws_ref, bst_ref, o_ref):
    row = lax.broadcasted_iota(jnp.int32, (GM_BLOCK, GM_BLOCK), 0)
    col = lax.broadcasted_iota(jnp.int32, (GM_BLOCK, GM_BLOCK), 1)
    visible = (col // CHUNK) <= (row // CHUNK)
    for g in range(GM_GROUPS):
        sl = slice(g * GM_GROUP_DIM, (g + 1) * GM_GROUP_DIM)
        v = v_ref[:, sl].astype(F32)
        mu = jnp.mean(v, axis=-1, keepdims=True)
        vc = v - mu
        var = jnp.mean(vc * vc, axis=-1, keepdims=True)
        vn = vc * lax.rsqrt(var + LN_EPS) * g_ref[:, sl] + b_ref[:, sl]
        w = jnp.where(visible, ws_ref[g], 0.0).astype(BF16)
        s = _dot(w, vn.astype(BF16)) + bst_ref[:, g:g + 1]
        o_ref[:, sl] = (u_ref[:, sl].astype(F32) * s).astype(o_ref.dtype)


def _spatial_gating(z_gm, ln_g, ln_b, w_s, b_s):
    n_tok = z_gm.shape[0]
    return pl.pallas_call(
        _gating_kernel,
        grid=(n_tok // GM_BLOCK,),
        in_specs=[
            pl.BlockSpec((GM_BLOCK, GM_WIDTH), lambda i: (i, 0)),
            pl.BlockSpec((GM_BLOCK, GM_WIDTH), lambda i: (i, 1)),
            pl.BlockSpec((1, GM_WIDTH), lambda i: (0, 0)),
            pl.BlockSpec((1, GM_WIDTH), lambda i: (0, 0)),
            pl.BlockSpec((GM_GROUPS, GM_BLOCK, GM_BLOCK), lambda i: (0, 0, 0)),
            pl.BlockSpec((GM_BLOCK, GM_GROUPS), lambda i: (0, 0)),
        ],
        out_specs=pl.BlockSpec((GM_BLOCK, GM_WIDTH), lambda i: (i, 0)),
        out_shape=jax.ShapeDtypeStruct((n_tok, GM_WIDTH), BF16),
        compiler_params=_params("arbitrary"),
        name="spatial_gating",
    )(z_gm, z_gm, ln_g.reshape(1, GM_WIDTH), ln_b.reshape(1, GM_WIDTH), w_s, b_s.T)


def _s5_tables(lam_re, lam_im, log_dt, b_re, b_im, c_re, c_im, d_skip, seq):
    hi = lax.Precision.HIGHEST
    g, n, p, t = SSM_GROUPS, SSM_N, SSM_P, SSM_T
    dt = jnp.exp(log_dt)[:, None]
    mag = jnp.exp(lam_re * dt)
    ab_re = mag * jnp.cos(lam_im * dt)
    ab_im = mag * jnp.sin(lam_im * dt)
    den = lam_re * lam_re + lam_im * lam_im
    num_re = ab_re - 1.0
    coef_re = (num_re * lam_re + ab_im * lam_im) / den
    coef_im = (ab_im * lam_re - num_re * lam_im) / den
    bb_re = coef_re[..., None] * b_re - coef_im[..., None] * b_im
    bb_im = coef_re[..., None] * b_im + coef_im[..., None] * b_re

    def power(k):
        k = jnp.asarray(k, F32)[..., None, None]
        mk = jnp.exp(k * (lam_re * dt))
        return mk * jnp.cos(k * (lam_im * dt)), mk * jnp.sin(k * (lam_im * dt))

    pw_re, pw_im = power(np.arange(t + 1))
    cp_re = c_re[None] * pw_re[:, :, None, :] - c_im[None] * pw_im[:, :, None, :]
    cp_im = c_re[None] * pw_im[:, :, None, :] + c_im[None] * pw_re[:, :, None, :]
    kern = (jnp.einsum('tgpn,gnq->gtpq', cp_re[:t], bb_re, precision=hi)
            - jnp.einsum('tgpn,gnq->gtpq', cp_im[:t], bb_im, precision=hi))
    ii = np.arange(t)[:, None]
    jj = np.arange(t)[None, :]
    blocks = jnp.where((ii >= jj)[None, :, :, None, None], kern[:, np.clip(ii - jj, 0, t - 1)], 0.0)
    m_t = blocks.transpose(0, 2, 4, 1, 3).reshape(g, t * p, t * p)
    rv_re, rv_im = pw_re[t - 1::-1][:t], pw_im[t - 1::-1][:t]
    st_re = rv_re[..., None] * bb_re[None] - rv_im[..., None] * bb_im[None]
    st_im = rv_re[..., None] * bb_im[None] + rv_im[..., None] * bb_re[None]
    w_st = jnp.concatenate([st_re, st_im], axis=2).transpose(1, 0, 3, 2).reshape(g, t * p, 2 * n)
    w_carry = jnp.concatenate([cp_re[1:], -cp_im[1:]], axis=3)
    w_carry = w_carry.transpose(1, 3, 0, 2).reshape(g, 2 * n, t * p)
    n_lvl = int(math.log2(seq // t))
    lv_re, lv_im = power(t * 2 ** np.arange(n_lvl))
    lvl = jnp.stack([jnp.concatenate([lv_re, lv_re], -1),
                     jnp.concatenate([-lv_im, lv_im], -1)], axis=2)
    lvl = lvl.transpose(1, 0, 2, 3).reshape(g, 2 * n_lvl, 2 * n)
    d_vec = jnp.tile(d_skip, (1, t))
    return m_t.astype(BF16), w_st.astype(BF16), w_carry.astype(BF16), lvl, d_vec


def _s5_kernel(u_ref, mt_ref, wst_ref, wc_ref, lvl_ref, d_ref, o_ref, *, n_chunks, n_lvl):
    rows = u_ref.shape[1]
    cidx = lax.broadcasted_iota(jnp.int32, (rows, 1), 0) % n_chunks
    for gi in range(SSM_GT):
        u = u_ref[gi]
        ub = u.astype(BF16)
        h = _dot(ub, wst_ref[gi])
        for lv in range(n_lvl):
            d = 1 << lv
            a_rr = lvl_ref[gi, 2 * lv:2 * lv + 1, :]
            a_is = lvl_ref[gi, 2 * lv + 1:2 * lv + 2, :]
            sh = jnp.where(cidx >= d, pltpu.roll(h, d, 0), 0.0)
            h = h + sh * a_rr + pltpu.roll(sh, SSM_N, 1) * a_is
        h_prev = jnp.where(cidx >= 1, pltpu.roll(h, 1, 0), 0.0)
        y = _dot(ub, mt_ref[gi]) + _dot(h_prev.astype(BF16), wc_ref[gi]) + d_ref[gi:gi + 1, :] * u
        o_ref[gi] = jax.nn.gelu(y).astype(o_ref.dtype)


def _s5_mixer(z_ssm, batch, seq, tables):
    m_t, w_st, w_carry, lvl, d_vec = tables
    g, p, t = SSM_GROUPS, SSM_P, SSM_T
    n_chunks = seq // t
    rows = batch * n_chunks
    u = z_ssm.reshape(batch, n_chunks, t, g, p).transpose(3, 0, 1, 2, 4).reshape(g, rows, t * p)
    n_lvl = lvl.shape[1] // 2
    y = pl.pallas_call(
        functools.partial(_s5_kernel, n_chunks=n_chunks, n_lvl=n_lvl),
        grid=(g // SSM_GT,),
        in_specs=[
            pl.BlockSpec((SSM_GT, rows, t * p), lambda i: (i, 0, 0)),
            pl.BlockSpec((SSM_GT, t * p, t * p), lambda i: (i, 0, 0)),
            pl.BlockSpec((SSM_GT, t * p, 2 * SSM_N), lambda i: (i, 0, 0)),
            pl.BlockSpec((SSM_GT, 2 * SSM_N, t * p), lambda i: (i, 0, 0)),
            pl.BlockSpec((SSM_GT, 2 * n_lvl, 2 * SSM_N), lambda i: (i, 0, 0)),
            pl.BlockSpec((SSM_GT, t * p), lambda i: (i, 0)),
        ],
        out_specs=pl.BlockSpec((SSM_GT, rows, t * p), lambda i: (i, 0, 0)),
        out_shape=jax.ShapeDtypeStruct((g, rows, t * p), BF16),
        compiler_params=_params("arbitrary"),
        name="s5_mixer",
    )(u, m_t, w_st, w_carry, lvl, d_vec)
    return y.reshape(g, batch, n_chunks, t, p).transpose(1, 2, 3, 0, 4).reshape(batch * seq, g * p)


def _rope_kernel(pos_ref, freq_ref, kr_ref, tab_ref, kro_ref):
    ang = pos_ref[...].astype(F32) * freq_ref[...]
    lane = lax.broadcasted_iota(jnp.int32, ang.shape, 1)
    sin = jnp.sin(ang)
    tab = jnp.where(lane < MLA_ROPE, jnp.cos(ang), jnp.where(lane < MLA_ROPE + MLA_ROPE // 2, -sin, sin))
    tab_ref[...] = tab
    kr = kr_ref[...]
    half = MLA_ROPE // 2
    sw = jnp.concatenate([kr[:, half:], kr[:, :half]], axis=1)
    kro_ref[...] = (kr * tab[:, :MLA_ROPE] + sw * tab[:, MLA_ROPE:]).astype(kro_ref.dtype)


def _rope_tables(positions, k_rope):
    n_tok = k_rope.shape[0]
    tm = 512
    inv_freq = ROPE_THETA ** (-np.arange(0, MLA_ROPE, 2, dtype=np.float64) / MLA_ROPE)
    freq = jnp.asarray(np.tile(inv_freq, 4)[None, :], F32)
    return pl.pallas_call(
        _rope_kernel,
        grid=(n_tok // tm,),
        in_specs=[
            pl.BlockSpec((tm, 1), lambda i: (i, 0)),
            pl.BlockSpec((1, 2 * MLA_ROPE), lambda i: (0, 0)),
            pl.BlockSpec((tm, MLA_ROPE), lambda i: (i, 0)),
        ],
        out_specs=[
            pl.BlockSpec((tm, 2 * MLA_ROPE), lambda i: (i, 0)),
            pl.BlockSpec((tm, MLA_ROPE), lambda i: (i, 0)),
        ],
        out_shape=[
            jax.ShapeDtypeStruct((n_tok, 2 * MLA_ROPE), F32),
            jax.ShapeDtypeStruct((n_tok, MLA_ROPE), BF16),
        ],
        compiler_params=_params("arbitrary"),
        name="rope_tables",
    )(positions.reshape(n_tok, 1), freq, k_rope)


ATT_TQ = 256
ATT_TK = 256


def _attn_kernel(q_ref, tab_ref, k0_ref, v0_ref, k1_ref, v1_ref, kr_ref, o_ref):
    qi = pl.program_id(2)
    scale = (MLA_NOPE + MLA_ROPE) ** -0.5
    half = MLA_ROPE // 2
    q = q_ref[...]
    cosf = tab_ref[:, :MLA_ROPE]
    sinf = tab_ref[:, MLA_ROPE:]
    row = lax.broadcasted_iota(jnp.int32, (ATT_TQ, ATT_TK), 0)
    col = lax.broadcasted_iota(jnp.int32, (ATT_TQ, ATT_TK), 1)
    visible = (col // CHUNK) <= (row // CHUNK)
    hd = MLA_NOPE + MLA_ROPE
    for hh, (k_ref, v_ref) in enumerate(((k0_ref, v0_ref), (k1_ref, v1_ref))):
        qn = (q[:, hh * hd:hh * hd + MLA_NOPE].astype(F32) * scale).astype(BF16)
        qr = q[:, hh * hd + MLA_NOPE:(hh + 1) * hd].astype(F32)
        sw = jnp.concatenate([qr[:, half:], qr[:, :half]], axis=1)
        qr = ((qr * cosf + sw * sinf) * scale).astype(BF16)

        def step(j, carry, masked):
            m, l, acc = carry
            ks = k_ref[pl.ds(pl.multiple_of(j * ATT_TK, ATT_TK), ATT_TK), :]
            krs = kr_ref[pl.ds(pl.multiple_of(j * ATT_TK, ATT_TK), ATT_TK), :]
            vs = v_ref[pl.ds(pl.multiple_of(j * ATT_TK, ATT_TK), ATT_TK), :]
            s = _dot_nt(qn, ks) + _dot_nt(qr, krs)
            if masked:
                s = jnp.where(visible, s, -1e30)
            m_new = jnp.maximum(m, jnp.max(s, axis=-1, keepdims=True))
            alpha = jnp.exp(m - m_new)
            p = jnp.exp(s - m_new)
            l = alpha * l + jnp.sum(p, axis=-1, keepdims=True)
            acc = alpha * acc + _dot(p.astype(BF16), vs)
            return m_new, l, acc

        init = (jnp.full((ATT_TQ, 1), -1e30, F32), jnp.zeros((ATT_TQ, 1), F32),
                jnp.zeros((ATT_TQ, MLA_V), F32))
        carry = lax.fori_loop(0, qi, functools.partial(step, masked=False), init)
        m, l, acc = step(qi, carry, True)
        o_ref[:, hh * MLA_V:(hh + 1) * MLA_V] = (acc / l).astype(o_ref.dtype)


def _attention(q, kv, k_rope, tab, batch, seq):
    n_tok = batch * seq
    nq = seq // ATT_TQ
    kvb = lambda off: pl.BlockSpec((seq, MLA_NOPE), lambda b, hp, qi: (b, 4 * hp + off))
    return pl.pallas_call(
        _attn_kernel,
        grid=(batch, MLA_HEADS // 2, nq),
        in_specs=[
            pl.BlockSpec((ATT_TQ, 2 * (MLA_NOPE + MLA_ROPE)), lambda b, hp, qi: (b * nq + qi, hp)),
            pl.BlockSpec((ATT_TQ, 2 * MLA_ROPE), lambda b, hp, qi: (b * nq + qi, 0)),
            kvb(0), kvb(1), kvb(2), kvb(3),
            pl.BlockSpec((seq, MLA_ROPE), lambda b, hp, qi: (b, 0)),
        ],
        out_specs=pl.BlockSpec((ATT_TQ, 2 * MLA_V), lambda b, hp, qi: (b * nq + qi, hp)),
        out_shape=jax.ShapeDtypeStruct((n_tok, MLA_HEADS * MLA_V), BF16),
        compiler_params=_params("arbitrary", "arbitrary", "arbitrary"),
        name="mla_attention",
    )(q, tab, kv, kv, kv, kv, k_rope)


LNR_TM = 256
SLOT_LANES = 128


def _ln_router_kernel(h_ref, mix_ref, g_ref, b_ref, wr_ref, br_ref,
                      ho_ref, hb_ref, idx_ref, gate_ref, rank_ref, cnt_ref, carry_ref):
    @pl.when(pl.program_id(0) == 0)
    def _():
        carry_ref[...] = jnp.zeros_like(carry_ref)

    x = DEEPNORM_ALPHA * h_ref[...] + mix_ref[...]
    mu = jnp.mean(x, axis=-1, keepdims=True)
    xc = x - mu
    var = jnp.mean(xc * xc, axis=-1, keepdims=True)
    hn = xc * lax.rsqrt(var + LN_EPS) * g_ref[...] + b_ref[...]
    ho_ref[...] = hn
    hb_ref[...] = hn.astype(BF16)

    logits = jnp.dot(hn, wr_ref[...], precision=lax.Precision.HIGHEST,
                     preferred_element_type=F32) + br_ref[...]
    tm = logits.shape[0]
    lane_e = lax.broadcasted_iota(jnp.int32, (tm, N_EXPERTS), 1)
    lane_s = lax.broadcasted_iota(jnp.int32, (tm, SLOT_LANES), 1)
    work = logits
    sel = jnp.zeros((tm, N_EXPERTS), F32)
    top_v, top_i, hot = [], [], []
    for _ in range(TOP_K):
        mx = jnp.max(work, axis=-1, keepdims=True)
        ix = jnp.min(jnp.where(work == mx, lane_e, N_EXPERTS), axis=-1, keepdims=True)
        oh = lane_e == ix
        work = jnp.where(oh, -jnp.inf, work)
        sel = sel + oh.astype(F32)
        top_v.append(mx)
        top_i.append(ix)
        hot.append(oh)
    ex = [jnp.exp(v - top_v[0]) for v in top_v]
    den = ex[0] + ex[1] + ex[2] + ex[3]

    r = lax.broadcasted_iota(jnp.int32, (tm, tm), 0)
    c = lax.broadcasted_iota(jnp.int32, (tm, tm), 1)
    strict = jnp.where(c < r, 1.0, 0.0).astype(BF16)
    prefix = _dot(strict, sel.astype(BF16)) + carry_ref[0:1, :]
    carry_ref[0:1, :] = carry_ref[0:1, :] + jnp.sum(sel, axis=0, keepdims=True)
    cnt_ref[...] = jnp.broadcast_to(carry_ref[0:1, :], cnt_ref.shape)

    idx_o = jnp.zeros((tm, SLOT_LANES), jnp.int32)
    gate_o = jnp.zeros((tm, SLOT_LANES), F32)
    rank_o = jnp.zeros((tm, SLOT_LANES), jnp.int32)
    for k in range(TOP_K):
        rk = jnp.sum(jnp.where(hot[k], prefix, 0.0), axis=-1, keepdims=True)
        idx_o = jnp.where(lane_s == k, top_i[k], idx_o)
        gate_o = jnp.where(lane_s == k, ex[k] / den, gate_o)
        rank_o = jnp.where(lane_s == k, rk.astype(jnp.int32), rank_o)
    idx_ref[...] = idx_o
    gate_ref[...] = gate_o
    rank_ref[...] = rank_o


def _ln_router(h, mix, ln_g, ln_b, w_router, b_router):
    n_tok, d = h.shape
    tm = LNR_TM
    row = lambda i: (i, 0)
    fixed = lambda i: (0, 0)
    return pl.pallas_call(
        _ln_router_kernel,
        grid=(n_tok // tm,),
        in_specs=[
            pl.BlockSpec((tm, d), row), pl.BlockSpec((tm, d), row),
            pl.BlockSpec((1, d), fixed), pl.BlockSpec((1, d), fixed),
            pl.BlockSpec((d, N_EXPERTS), fixed), pl.BlockSpec((1, N_EXPERTS), fixed),
        ],
        out_specs=[
            pl.BlockSpec((tm, d), row), pl.BlockSpec((tm, d), row),
            pl.BlockSpec((tm, SLOT_LANES), row), pl.BlockSpec((tm, SLOT_LANES), row),
            pl.BlockSpec((tm, SLOT_LANES), row), pl.BlockSpec((8, N_EXPERTS), fixed),
        ],
        out_shape=[
            jax.ShapeDtypeStruct((n_tok, d), F32), jax.ShapeDtypeStruct((n_tok, d), BF16),
            jax.ShapeDtypeStruct((n_tok, SLOT_LANES), jnp.int32),
            jax.ShapeDtypeStruct((n_tok, SLOT_LANES), F32),
            jax.ShapeDtypeStruct((n_tok, SLOT_LANES), jnp.int32),
            jax.ShapeDtypeStruct((8, N_EXPERTS), F32),
        ],
        scratch_shapes=[pltpu.VMEM((8, N_EXPERTS), F32)],
        compiler_params=_params("arbitrary"),
        name="ln_router",
    )(h, mix, ln_g.reshape(1, d), ln_b.reshape(1, d), w_router, b_router.reshape(1, N_EXPERTS))


def _moe_kernel(e_ref, valid_ref, src_ref, nv_ref, x_ref, wg_ref, wl_ref, bg_ref, bl_ref,
                wd_ref, bd_ref, o_ref, act_ref):
    i = pl.program_id(0)
    t = pl.program_id(1)
    n_valid = valid_ref[i]
    n_sub = MOE_TM // MOE_SB

    @pl.when(jnp.logical_and(t < MOE_T1, n_valid > 0))
    def _gate_up():
        wg = wg_ref[...].astype(BF16)
        wl = wl_ref[...].astype(BF16)
        for s in range(n_sub):
            @pl.when(s * MOE_SB < n_valid)
            def _():
                rows = slice(s * MOE_SB, (s + 1) * MOE_SB)
                xs = x_ref[rows, :]
                hg = jnp.minimum(_dot(xs, wg) + bg_ref[...], SWIGLU_LIMIT)
                hl = jnp.clip(_dot(xs, wl) + bl_ref[...], -SWIGLU_LIMIT, SWIGLU_LIMIT)
                a = hg * jax.nn.sigmoid(SWIGLU_ALPHA * hg) * (hl + 1.0)
                act_ref[t, rows, :] = a.astype(BF16)

    @pl.when(jnp.logical_and(t >= MOE_T1, i < nv_ref[0]))
    def _down():
        wd = wd_ref[...].astype(BF16)
        for s in range(n_sub):
            rows = slice(s * MOE_SB, (s + 1) * MOE_SB)

            @pl.when(s * MOE_SB < n_valid)
            def _():
                y = bd_ref[...] + _dot(act_ref[0, rows, :], wd[0:MOE_TH, :])
                for tt in range(1, MOE_T1):
                    y = y + _dot(act_ref[tt, rows, :], wd[tt * MOE_TH:(tt + 1) * MOE_TH, :])
                o_ref[rows, :] = y.astype(o_ref.dtype)

            @pl.when(s * MOE_SB >= n_valid)
            def _():
                o_ref[rows, :] = jnp.zeros((MOE_SB, MOE_TN), o_ref.dtype)


def _moe_experts(x_sorted, blk_exp, blk_valid, blk_src, n_valid_blocks, w_gu, b_gu, w_down, b_down):
    n_rows = x_sorted.shape[0]
    n_blocks = n_rows // MOE_TM
    t_last = MOE_T1 + MOE_T2 - 1

    def tt(i, t, nv):
        return jnp.where(i < nv[0], t, t_last)

    x_map = lambda i, t, e, v, s, nv: (s[i], 0)
    wg_map = lambda i, t, e, v, s, nv: (e[i], 0, jnp.minimum(tt(i, t, nv), MOE_T1 - 1))
    wl_map = lambda i, t, e, v, s, nv: (e[i], 0, MOE_T1 + jnp.minimum(tt(i, t, nv), MOE_T1 - 1))
    wd_map = lambda i, t, e, v, s, nv: (e[i], 0, jnp.maximum(tt(i, t, nv) - MOE_T1, 0))
    o_map = lambda i, t, e, v, s, nv: (s[i], jnp.maximum(tt(i, t, nv) - MOE_T1, 0))
    grid_spec = pltpu.PrefetchScalarGridSpec(
        num_scalar_prefetch=4,
        grid=(n_blocks, MOE_T1 + MOE_T2),
        in_specs=[
            pl.BlockSpec((MOE_TM, D_MODEL), x_map),
            pl.BlockSpec((None, D_MODEL, MOE_TH), wg_map),
            pl.BlockSpec((None, D_MODEL, MOE_TH), wl_map),
            pl.BlockSpec((None, 1, MOE_TH), wg_map),
            pl.BlockSpec((None, 1, MOE_TH), wl_map),
            pl.BlockSpec((None, D_EXPERT, MOE_TN), wd_map),
            pl.BlockSpec((None, 1, MOE_TN), wd_map),
        ],
        out_specs=pl.BlockSpec((MOE_TM, MOE_TN), o_map),
        scratch_shapes=[pltpu.VMEM((MOE_T1, MOE_TM, MOE_TH), BF16)],
    )
    return pl.pallas_call(
        _moe_kernel,
        grid_spec=grid_spec,
        out_shape=jax.ShapeDtypeStruct((n_rows, D_MODEL), BF16),
        compiler_params=_params("arbitrary", "arbitrary"),
        name="moe_experts",
    )(blk_exp, blk_valid, blk_src, n_valid_blocks, x_sorted, w_gu, w_gu,
      b_gu.reshape(N_EXPERTS, 1, 2 * D_EXPERT), b_gu.reshape(N_EXPERTS, 1, 2 * D_EXPERT),
      w_down, b_down.reshape(N_EXPERTS, 1, D_MODEL))


LNC_TM = 128


def _ln_combine_kernel(h_ref, y_ref, gate_ref, g_ref, b_ref, ho_ref, hb_ref):
    d = h_ref.shape[1]
    x = DEEPNORM_ALPHA * h_ref[...]
    for k in range(TOP_K):
        x = x + gate_ref[:, k:k + 1] * y_ref[:, k * d:(k + 1) * d].astype(F32)
    mu = jnp.mean(x, axis=-1, keepdims=True)
    xc = x - mu
    var = jnp.mean(xc * xc, axis=-1, keepdims=True)
    hn = xc * lax.rsqrt(var + LN_EPS) * g_ref[...] + b_ref[...]
    ho_ref[...] = hn
    hb_ref[...] = hn.astype(BF16)


def _ln_combine(h, y_sel, gates, ln_g, ln_b):
    n_tok, d = h.shape
    tm = LNC_TM
    row = lambda i: (i, 0)
    fixed = lambda i: (0, 0)
    return pl.pallas_call(
        _ln_combine_kernel,
        grid=(n_tok // tm,),
        in_specs=[
            pl.BlockSpec((tm, d), row), pl.BlockSpec((tm, TOP_K * d), row),
            pl.BlockSpec((tm, SLOT_LANES), row),
            pl.BlockSpec((1, d), fixed), pl.BlockSpec((1, d), fixed),
        ],
        out_specs=[pl.BlockSpec((tm, d), row), pl.BlockSpec((tm, d), row)],
        out_shape=[jax.ShapeDtypeStruct((n_tok, d), F32), jax.ShapeDtypeStruct((n_tok, d), BF16)],
        compiler_params=_params("arbitrary"),
        name="ln_combine",
    )(h, y_sel, gates, ln_g.reshape(1, d), ln_b.reshape(1, d))


def _moe_layer(h, hb, top_idx, gates, rank, counts, w_gu, b_gu, w_down, b_down, ln_g, ln_b):
    n_tok = h.shape[0]
    n_assign = n_tok * TOP_K
    n_blocks = n_assign // MOE_TM + N_EXPERTS
    n_rows = n_blocks * MOE_TM
    cnt = counts[0].astype(jnp.int32)
    padded = (cnt + MOE_TM - 1) // MOE_TM * MOE_TM
    pad_end = jnp.cumsum(padded)
    pad_start = pad_end - padded
    e_flat = top_idx[:, :TOP_K].reshape(-1)
    dest = pad_start[e_flat] + rank[:, :TOP_K].reshape(-1)
    tok = jnp.arange(n_assign, dtype=jnp.int32) // TOP_K
    row_tok = jnp.zeros((n_rows,), jnp.int32).at[dest].set(tok)
    n_valid_blocks = pad_end[-1] // MOE_TM
    blk = jnp.arange(n_blocks, dtype=jnp.int32)
    src = jnp.minimum(blk, n_valid_blocks - 1)
    blk_exp = jnp.minimum(jnp.searchsorted(pad_end, src * MOE_TM, side='right'), N_EXPERTS - 1)
    blk_exp = blk_exp.astype(jnp.int32)
    valid = jnp.clip(pad_start[blk_exp] + cnt[blk_exp] - src * MOE_TM, 0, MOE_TM)
    valid = jnp.where(blk < n_valid_blocks, valid, 0).astype(jnp.int32)
    x_sorted = jnp.take(hb, row_tok, axis=0)
    y_sorted = _moe_experts(x_sorted, blk_exp, valid, src.astype(jnp.int32),
                            n_valid_blocks.reshape(1).astype(jnp.int32), w_gu, b_gu, w_down, b_down)
    y_sel = jnp.take(y_sorted, dest, axis=0).reshape(n_tok, TOP_K * D_MODEL)
    return _ln_combine(h, y_sel, gates, ln_g, ln_b)


def kernel(x, positions, ln_g, ln_b, hy_w_in, hy_w_out, gm_ln_g, gm_ln_b, gm_w_s, gm_b_s, ssm_lam_re, ssm_lam_im, ssm_log_dt, ssm_b_re, ssm_b_im, ssm_c_re, ssm_c_im, ssm_d, ssm_w_glu, ssm_b_glu, mla_w_in, mla_q_norm_g, mla_kv_norm_g, mla_w_uq, mla_w_ukv, mla_w_o, moe_w_router, moe_b_router, moe_w_gu, moe_b_gu, moe_w_down, moe_b_down):
    batch, seq, d = x.shape
    n_tok = batch * seq
    h = x.reshape(n_tok, d)
    hb = h.astype(BF16)
    for layer in range(DEPTH):
        i = layer // 2
        if layer % 2 == 0:
            z_gm = _mm([hb], hy_w_in[i], col0=0, n_cols=2 * GM_WIDTH, tm=1024, tn=256,
                       out_dtype=BF16, epilogue="gelu", name="hy_in_gm")
            z_ssm = _mm([hb], hy_w_in[i], col0=2 * GM_WIDTH, n_cols=SSM_WIDTH, tm=1024, tn=256,
                        out_dtype=F32, name="hy_in_ssm")
            y_gm = _spatial_gating(z_gm, gm_ln_g[i], gm_ln_b[i], gm_w_s[i], gm_b_s[i])
            tables = _s5_tables(ssm_lam_re[i], ssm_lam_im[i], ssm_log_dt[i], ssm_b_re[i], ssm_b_im[i],
                                ssm_c_re[i], ssm_c_im[i], ssm_d[i], seq)
            y_act = _s5_mixer(z_ssm, batch, seq, tables)
            y_ssm = _mm([y_act], ssm_w_glu[i], tm=1024, tn=256, out_dtype=BF16, epilogue="glu",
                        bias=ssm_b_glu[i], mul=y_act, name="s5_glu")
            mix = _mm([y_gm, y_ssm], hy_w_out[i], tm=1024, tn=256, out_dtype=F32, name="hy_out")
        else:
            w_in = mla_w_in[i]
            n_main = MLA_Q_RANK + MLA_KV_RANK
            c_main = _mm([hb], w_in, col0=0, n_cols=n_main, tm=1024, tn=256, out_dtype=F32,
                         name="mla_in")
            k_rope = _mm([hb], w_in[:, n_main:], tm=1024, tn=MLA_ROPE, out_dtype=F32, name="mla_in_rope")
            tab, k_rope = _rope_tables(positions, k_rope)
            q = _mm([(c_main, MLA_Q_RANK, 0)], mla_w_uq[i], tm=1024, tn=384, out_dtype=BF16,
                    prologue="rms", gain=mla_q_norm_g[i], name="mla_uq")
            kv = _mm([(c_main, MLA_KV_RANK, MLA_Q_RANK // MLA_KV_RANK)], mla_w_ukv[i], tm=1024, tn=512, out_dtype=BF16,
                     prologue="rms", gain=mla_kv_norm_g[i], name="mla_ukv")
            o = _attention(q, kv, k_rope, tab, batch, seq)
            mix = _mm([o], mla_w_o[i], tm=1024, tn=256, out_dtype=F32, name="mla_out")
        h, hb, top_idx, gates, rank, counts = _ln_router(
            h, mix, ln_g[layer, 0], ln_b[layer, 0], moe_w_router[layer], moe_b_router[layer])
        h, hb = _moe_layer(h, hb, top_idx, gates, rank, counts, moe_w_gu[layer], moe_b_gu[layer],
                           moe_w_down[layer], moe_b_down[layer], ln_g[layer, 1], ln_b[layer, 1])
    return h.reshape(batch, seq, d)
```

```python
import functools
import math

import numpy as np
import jax
import jax.numpy as jnp
from jax import lax
from jax.experimental import pallas as pl
from jax.experimental.pallas import tpu as pltpu

F32 = jnp.float32
BF16 = jnp.bfloat16

D_MODEL = 4096
DEPTH = 2
CHUNK = 64
DEEPNORM_ALPHA = (2 * DEPTH) ** 0.25
LN_EPS = 1e-5
RMS_EPS = 1e-6
LANES = 128

GM_WIDTH = 2048
GM_GROUPS = 8
GM_GROUP_DIM = 256
GM_BLOCK = 128
SSM_WIDTH = 2048
SSM_P = 16
SSM_GROUPS = 128
SSM_N = 64
SSM_T = 16
SSM_OCT = LANES // SSM_P
SSM_NOCT = SSM_GROUPS // SSM_OCT
SSM_PAIRS = SSM_OCT // 2
SSM_SW = SSM_OCT * SSM_N

MLA_HEADS = 32
MLA_Q_RANK = 1024
MLA_KV_RANK = 512
MLA_NOPE = 128
MLA_ROPE = 64
MLA_V = 128
ROPE_THETA = 10000.0

N_EXPERTS = 32
TOP_K = 4
D_EXPERT = 1024
SWIGLU_LIMIT = 7.0
SWIGLU_ALPHA = 1.702

VMEM_LIMIT_BYTES = 56 * 1024 * 1024

MOE_TM = 1024
MOE_SB = 256
MOE_TH = 256
MOE_TN = 512
MOE_T1 = D_EXPERT // MOE_TH
MOE_T2 = D_MODEL // MOE_TN


def _params(*sem):
    return pltpu.CompilerParams(dimension_semantics=sem, vmem_limit_bytes=VMEM_LIMIT_BYTES)


def _dot(a, b):
    return jnp.dot(a, b, preferred_element_type=F32)


def _dot_nt(a, b):
    return lax.dot_general(a, b, (((1,), (1,)), ((), ())), preferred_element_type=F32)


def _mm_kernel(*refs, prologue, epilogue, n_x):
    it = iter(refs)
    x_refs = [next(it) for _ in range(n_x)]
    w_ref = next(it)
    g_ref = next(it) if prologue == "rms" else None
    b_ref = next(it) if epilogue == "glu" else None
    y_ref = next(it) if epilogue == "glu" else None
    o_ref = next(it)
    xs_ref = next(it) if prologue == "rms" else None

    if prologue == "rms":
        @pl.when(pl.program_id(1) == 0)
        def _():
            xf = x_refs[0][...].astype(F32)
            ms = jnp.mean(xf * xf, axis=-1, keepdims=True)
            xs_ref[...] = (xf * lax.rsqrt(ms + RMS_EPS) * g_ref[...]).astype(BF16)
        xs = [xs_ref[...]]
    else:
        xs = [r[...].astype(BF16) for r in x_refs]

    acc = None
    k0 = 0
    for x in xs:
        kk = x.shape[1]
        part = _dot(x, w_ref[k0:k0 + kk, :].astype(BF16))
        acc = part if acc is None else acc + part
        k0 += kk
    if epilogue == "gelu":
        acc = jax.nn.gelu(acc)
    elif epilogue == "glu":
        acc = y_ref[...].astype(F32) * jax.nn.sigmoid(acc + b_ref[...])
    o_ref[...] = acc.astype(o_ref.dtype)


def _mm(xs, w, *, col0=0, n_cols=None, tm, tn, out_dtype, prologue=None, gain=None,
        epilogue=None, bias=None, mul=None, name):
    xs = [(x, x.shape[1], 0) if not isinstance(x, tuple) else x for x in xs]
    m = xs[0][0].shape[0]
    k_total = w.shape[0]
    n_cols = w.shape[1] - col0 if n_cols is None else n_cols
    tn = min(tn, n_cols)
    assert m % tm == 0 and n_cols % tn == 0 and col0 % tn == 0
    cb0 = col0 // tn
    assert sum(kw for _, kw, _ in xs) == k_total
    in_specs = [pl.BlockSpec((tm, kw), functools.partial(lambda i, j, cb: (i, cb), cb=cb))
                for _, kw, cb in xs]
    in_specs.append(pl.BlockSpec((k_total, tn), lambda i, j: (0, cb0 + j)))
    args = [x for x, _, _ in xs] + [w]
    scratch = []
    if prologue == "rms":
        in_specs.append(pl.BlockSpec((1, k_total), lambda i, j: (0, 0)))
        args.append(gain.reshape(1, k_total))
        scratch.append(pltpu.VMEM((tm, k_total), BF16))
    if epilogue == "glu":
        in_specs.append(pl.BlockSpec((1, tn), lambda i, j: (0, j)))
        in_specs.append(pl.BlockSpec((tm, tn), lambda i, j: (i, j)))
        args += [bias.reshape(1, n_cols), mul]
    return pl.pallas_call(
        functools.partial(_mm_kernel, prologue=prologue, epilogue=epilogue, n_x=len(xs)),
        grid=(m // tm, n_cols // tn),
        in_specs=in_specs,
        out_specs=pl.BlockSpec((tm, tn), lambda i, j: (i, j)),
        out_shape=jax.ShapeDtypeStruct((m, n_cols), out_dtype),
        scratch_shapes=scratch,
        compiler_params=_params("arbitrary", "arbitrary"),
        name=name,
    )(*args)


def _gating_kernel(u_ref, v_ref, g_ref, b_ref, ws_ref, bst_ref, o_ref):
    row = lax.broadcasted_iota(jnp.int32, (GM_BLOCK, GM_BLOCK), 0)
    col = lax.broadcasted_iota(jnp.int32, (GM_BLOCK, GM_BLOCK), 1)
    visible = (col // CHUNK) <= (row // CHUNK)
    for g in range(GM_GROUPS):
        sl = slice(g * GM_GROUP_DIM, (g + 1) * GM_GROUP_DIM)
        v = v_ref[:, sl].astype(F32)
        mu = jnp.mean(v, axis=-1, keepdims=True)
        vc = v - mu
        var = jnp.mean(vc * vc, axis=-1, keepdims=True)
        vn = vc * lax.rsqrt(var + LN_EPS) * g_ref[:, sl] + b_ref[:, sl]
        w = jnp.where(visible, ws_ref[g], 0.0).astype(BF16)
        s = _dot(w, vn.astype(BF16)) + bst_ref[:, g:g + 1]
        o_ref[:, sl] = (u_ref[:, sl].astype(F32) * s).astype(o_ref.dtype)


def _spatial_gating(z_gm, ln_g, ln_b, w_s, b_s):
    n_tok = z_gm.shape[0]
    return pl.pallas_call(
        _gating_kernel,
        grid=(n_tok // GM_BLOCK,),
        in_specs=[
            pl.BlockSpec((GM_BLOCK, GM_WIDTH), lambda i: (i, 0)),
            pl.BlockSpec((GM_BLOCK, GM_WIDTH), lambda i: (i, 1)),
            pl.BlockSpec((1, GM_WIDTH), lambda i: (0, 0)),
            pl.BlockSpec((1, GM_WIDTH), lambda i: (0, 0)),
            pl.BlockSpec((GM_GROUPS, GM_BLOCK, GM_BLOCK), lambda i: (0, 0, 0)),
            pl.BlockSpec((GM_BLOCK, GM_GROUPS), lambda i: (0, 0)),
        ],
        out_specs=pl.BlockSpec((GM_BLOCK, GM_WIDTH), lambda i: (i, 0)),
        out_shape=jax.ShapeDtypeStruct((n_tok, GM_WIDTH), BF16),
        compiler_params=_params("arbitrary"),
        name="spatial_gating",
    )(z_gm, z_gm, ln_g.reshape(1, GM_WIDTH), ln_b.reshape(1, GM_WIDTH), w_s, b_s.T)


def _s5_tables(lam_re, lam_im, log_dt, b_re, b_im, c_re, c_im, d_skip, seq):
    hi = lax.Precision.HIGHEST
    n, p, t = SSM_N, SSM_P, SSM_T
    no, npr = SSM_NOCT, SSM_PAIRS
    dt = jnp.exp(log_dt)[:, None]
    mag = jnp.exp(lam_re * dt)
    ab_re = mag * jnp.cos(lam_im * dt)
    ab_im = mag * jnp.sin(lam_im * dt)
    den = lam_re * lam_re + lam_im * lam_im
    num_re = ab_re - 1.0
    coef_re = (num_re * lam_re + ab_im * lam_im) / den
    coef_im = (ab_im * lam_re - num_re * lam_im) / den
    bb_re = coef_re[..., None] * b_re - coef_im[..., None] * b_im
    bb_im = coef_re[..., None] * b_im + coef_im[..., None] * b_re

    def power(k):
        k = jnp.asarray(k, F32)[..., None, None]
        mk = jnp.exp(k * (lam_re * dt))
        return mk * jnp.cos(k * (lam_im * dt)), mk * jnp.sin(k * (lam_im * dt))

    pw_re, pw_im = power(np.arange(t + 1))
    cp_re = c_re[None] * pw_re[:, :, None, :] - c_im[None] * pw_im[:, :, None, :]
    cp_im = c_re[None] * pw_im[:, :, None, :] + c_im[None] * pw_re[:, :, None, :]
    kern = (jnp.einsum('tgpn,gnq->gtpq', cp_re[:t], bb_re, precision=hi)
            - jnp.einsum('tgpn,gnq->gtpq', cp_im[:t], bb_im, precision=hi))
    eye_o = jnp.eye(SSM_OCT, dtype=F32)
    eye_2 = jnp.eye(2, dtype=F32)
    kbd = jnp.einsum('oatpq,ab->otaqbp', kern.reshape(no, SSM_OCT, t, p, p), eye_o)
    kbd = kbd.reshape(no, t, LANES, LANES)
    rv_re, rv_im = pw_re[t - 1::-1], pw_im[t - 1::-1]
    st_re = rv_re[..., None] * bb_re[None] - rv_im[..., None] * bb_im[None]
    st_im = rv_re[..., None] * bb_im[None] + rv_im[..., None] * bb_re[None]
    st = jnp.stack([st_re, st_im], 0).reshape(2, t, no, npr, 2, n, p)
    w_st = jnp.einsum('rjoxanq,ab->ojxraqbn', st, eye_2).reshape(no, t, npr, 2, 2 * p, 2 * n)
    wc = jnp.stack([cp_re[1:], -cp_im[1:]], 0).reshape(2, t, no, npr, 2, p, n)
    w_ct = jnp.einsum('rioxapn,ab->oixrapbn', wc, eye_2).reshape(no, t, npr, 2, 2 * p, 2 * n)
    n_lvl = int(math.log2(seq // t))
    lv_re, lv_im = power(t * 2 ** np.arange(n_lvl))
    lr = lv_re.reshape(n_lvl, no, SSM_SW)
    li = lv_im.reshape(n_lvl, no, SSM_SW)
    lvl = jnp.stack([jnp.concatenate([lr, lr], -1), jnp.concatenate([-li, li], -1)], axis=1)
    lvl = lvl.transpose(2, 0, 1, 3).reshape(no, 2 * n_lvl, 2 * SSM_SW)
    d_vec = jnp.tile(d_skip.reshape(no, 1, LANES), (1, 1, t))
    return kbd.astype(BF16), w_st.astype(BF16), w_ct.astype(BF16), lvl, d_vec


def _s5_kernel(u_ref, kbd_ref, wst_ref, wct_ref, lvl_ref, d_ref, o_ref, toe_ref, st_ref, ct_ref,
               *, n_chunks, n_lvl):
    t = SSM_T

    @pl.when(pl.program_id(0) == 0)
    def _():
        toe_ref[...] = jnp.zeros_like(toe_ref)
        st_ref[...] = jnp.zeros_like(st_ref)
        ct_ref[...] = jnp.zeros_like(ct_ref)

    for i in range(t):
        for j in range(i + 1):
            toe_ref[j * LANES:(j + 1) * LANES, i * LANES:(i + 1) * LANES] = kbd_ref[i - j]
    for j in range(t):
        for x in range(SSM_PAIRS):
            for r in range(2):
                rows = slice(j * LANES + x * 2 * SSM_P, j * LANES + (x + 1) * 2 * SSM_P)
                lanes = slice(r * SSM_SW + x * LANES, r * SSM_SW + (x + 1) * LANES)
                st_ref[rows, lanes] = wst_ref[j, x, r]
                ct_ref[rows, lanes] = wct_ref[j, x, r]

    u = u_ref[...]
    ub = u.astype(BF16)
    h = _dot(ub, st_ref[...])
    cidx = lax.broadcasted_iota(jnp.int32, (u.shape[0], 1), 0) % n_chunks
    for lv in range(n_lvl):
        d = 1 << lv
        a_rr = lvl_ref[2 * lv:2 * lv + 1, :]
        a_is = lvl_ref[2 * lv + 1:2 * lv + 2, :]
        sh = jnp.where(cidx >= d, pltpu.roll(h, d, 0), 0.0)
        h = h + sh * a_rr + pltpu.roll(sh, SSM_SW, 1) * a_is
    h_prev = jnp.where(cidx >= 1, pltpu.roll(h, 1, 0), 0.0)
    y = _dot(ub, toe_ref[...]) + _dot_nt(h_prev.astype(BF16), ct_ref[...]) + d_ref[...] * u
    o_ref[...] = jax.nn.gelu(y).astype(o_ref.dtype)


def _s5_mixer(z_ssm, batch, seq, tables):
    kbd, w_st, w_ct, lvl, d_vec = tables
    t, no = SSM_T, SSM_NOCT
    n_chunks = seq // t
    rows = batch * n_chunks
    width = t * LANES
    n_lvl = lvl.shape[1] // 2
    u = z_ssm.reshape(rows, t, no, LANES).transpose(2, 0, 1, 3).reshape(no, rows, width)
    y = pl.pallas_call(
        functools.partial(_s5_kernel, n_chunks=n_chunks, n_lvl=n_lvl),
        grid=(no,),
        in_specs=[
            pl.BlockSpec((None, rows, width), lambda i: (i, 0, 0)),
            pl.BlockSpec((None, t, LANES, LANES), lambda i: (i, 0, 0, 0)),
            pl.BlockSpec((None, t, SSM_PAIRS, 2, 2 * SSM_P, LANES), lambda i: (i, 0, 0, 0, 0, 0)),
            pl.BlockSpec((None, t, SSM_PAIRS, 2, 2 * SSM_P, LANES), lambda i: (i, 0, 0, 0, 0, 0)),
            pl.BlockSpec((None, 2 * n_lvl, 2 * SSM_SW), lambda i: (i, 0, 0)),
            pl.BlockSpec((None, 1, width), lambda i: (i, 0, 0)),
        ],
        out_specs=pl.BlockSpec((None, rows, width), lambda i: (i, 0, 0)),
        out_shape=jax.ShapeDtypeStruct((no, rows, width), BF16),
        scratch_shapes=[
            pltpu.VMEM((width, width), BF16),
            pltpu.VMEM((width, 2 * SSM_SW), BF16),
            pltpu.VMEM((width, 2 * SSM_SW), BF16),
        ],
        compiler_params=_params("arbitrary"),
        name="s5_mixer",
    )(u, kbd, w_st, w_ct, lvl, d_vec)
    return y.reshape(no, rows, t, LANES).transpose(1, 2, 0, 3).reshape(batch * seq, SSM_WIDTH)


def _rope_kernel(pos_ref, freq_ref, kr_ref, tab_ref, kro_ref):
    ang = pos_ref[...].astype(F32) * freq_ref[...]
    lane = lax.broadcasted_iota(jnp.int32, ang.shape, 1)
    sin = jnp.sin(ang)
    tab = jnp.where(lane < MLA_ROPE, jnp.cos(ang), jnp.where(lane < MLA_ROPE + MLA_ROPE // 2, -sin, sin))
    tab_ref[...] = tab
    kr = kr_ref[...]
    half = MLA_ROPE // 2
    sw = jnp.concatenate([kr[:, half:], kr[:, :half]], axis=1)
    kro_ref[...] = (kr * tab[:, :MLA_ROPE] + sw * tab[:, MLA_ROPE:]).astype(kro_ref.dtype)


def _rope_tables(positions, k_rope):
    n_tok = k_rope.shape[0]
    tm = 512
    inv_freq = ROPE_THETA ** (-np.arange(0, MLA_ROPE, 2, dtype=np.float64) / MLA_ROPE)
    freq = jnp.asarray(np.tile(inv_freq, 4)[None, :], F32)
    return pl.pallas_call(
        _rope_kernel,
        grid=(n_tok // tm,),
        in_specs=[
            pl.BlockSpec((tm, 1), lambda i: (i, 0)),
            pl.BlockSpec((1, 2 * MLA_ROPE), lambda i: (0, 0)),
            pl.BlockSpec((tm, MLA_ROPE), lambda i: (i, 0)),
        ],
        out_specs=[
            pl.BlockSpec((tm, 2 * MLA_ROPE), lambda i: (i, 0)),
            pl.BlockSpec((tm, MLA_ROPE), lambda i: (i, 0)),
        ],
        out_shape=[
            jax.ShapeDtypeStruct((n_tok, 2 * MLA_ROPE), F32),
            jax.ShapeDtypeStruct((n_tok, MLA_ROPE), BF16),
        ],
        compiler_params=_params("arbitrary"),
        name="rope_tables",
    )(positions.reshape(n_tok, 1), freq, k_rope)


ATT_TK = 256
ATT_TQ = 2 * ATT_TK
ATT_DK = 2 * MLA_NOPE


def _attn_kernel(q_ref, tab_ref, k0_ref, v0_ref, k1_ref, v1_ref, kr_ref, o_ref, kc_ref):
    qi = pl.program_id(2)
    hd = MLA_NOPE + MLA_ROPE
    scale = hd ** -0.5
    half = MLA_ROPE // 2
    tk = ATT_TK

    @pl.when(qi == 0)
    def _():
        for hh, k_ref in enumerate((k0_ref, k1_ref)):
            kc_ref[hh, :, 0:MLA_NOPE] = k_ref[...]
            kc_ref[hh, :, MLA_NOPE:hd] = kr_ref[...]
            kc_ref[hh, :, hd:] = jnp.zeros((kr_ref.shape[0], ATT_DK - hd), BF16)

    q = q_ref[...]
    cosf = tab_ref[:, :MLA_ROPE]
    sinf = tab_ref[:, MLA_ROPE:]
    qc = []
    for hh in range(2):
        qn = q[:, hh * hd:hh * hd + MLA_NOPE].astype(F32) * scale
        qr = q[:, hh * hd + MLA_NOPE:(hh + 1) * hd].astype(F32)
        sw = jnp.concatenate([qr[:, half:], qr[:, :half]], axis=1)
        qr = (qr * cosf + sw * sinf) * scale
        pad = jnp.zeros((ATT_TQ, ATT_DK - hd), F32)
        qc.append(jnp.concatenate([qn, qr, pad], axis=1).astype(BF16))
    v_refs = (v0_ref, v1_ref)
    row = lax.broadcasted_iota(jnp.int32, (tk, tk), 0)
    col = lax.broadcasted_iota(jnp.int32, (tk, tk), 1)
    visible = (col // CHUNK) <= (row // CHUNK)

    def update(state, hh, sub, j, masked):
        m, l, acc = state
        start = pl.multiple_of(j * tk, tk)
        ks = kc_ref[hh, pl.ds(start, tk), :]
        vs = v_refs[hh][pl.ds(start, tk), :]
        s = _dot_nt(qc[hh][sub * tk:(sub + 1) * tk], ks)
        if masked:
            s = jnp.where(visible, s, -1e30)
        m_new = jnp.maximum(m, jnp.max(s, axis=-1, keepdims=True))
        alpha = jnp.exp(m - m_new)
        p = jnp.exp(s - m_new)
        l = alpha * l + jnp.sum(p, axis=-1, keepdims=True)
        acc = alpha * acc + _dot(p.astype(BF16), vs)
        return m_new, l, acc

    chains = [(hh, sub) for hh in range(2) for sub in range(2)]

    def body(j, states):
        return tuple(update(st, hh, sub, j, False) for st, (hh, sub) in zip(states, chains))

    init = tuple((jnp.full((tk, 1), -1e30, F32), jnp.zeros((tk, 1), F32), jnp.zeros((tk, MLA_V), F32))
                 for _ in chains)
    states = list(lax.fori_loop(0, 2 * qi, body, init))
    for c, (hh, sub) in enumerate(chains):
        st = update(states[c], hh, sub, 2 * qi, masked=(sub == 0))
        if sub == 1:
            st = update(st, hh, sub, 2 * qi + 1, masked=True)
        _, l, acc = st
        o_ref[sub * tk:(sub + 1) * tk, hh * MLA_V:(hh + 1) * MLA_V] = (acc / l).astype(o_ref.dtype)


def _attention(q, kv, k_rope, tab, batch, seq):
    n_tok = batch * seq
    nq = seq // ATT_TQ
    kvb = lambda off: pl.BlockSpec((seq, MLA_NOPE), lambda b, hp, qi: (b, 4 * hp + off))
    return pl.pallas_call(
        _attn_kernel,
        grid=(batch, MLA_HEADS // 2, nq),
        in_specs=[
            pl.BlockSpec((ATT_TQ, 2 * (MLA_NOPE + MLA_ROPE)), lambda b, hp, qi: (b * nq + qi, hp)),
            pl.BlockSpec((ATT_TQ, 2 * MLA_ROPE), lambda b, hp, qi: (b * nq + qi, 0)),
            kvb(0), kvb(1), kvb(2), kvb(3),
            pl.BlockSpec((seq, MLA_ROPE), lambda b, hp, qi: (b, 0)),
        ],
        out_specs=pl.BlockSpec((ATT_TQ, 2 * MLA_V), lambda b, hp, qi: (b * nq + qi, hp)),
        out_shape=jax.ShapeDtypeStruct((n_tok, MLA_HEADS * MLA_V), BF16),
        scratch_shapes=[pltpu.VMEM((2, seq, ATT_DK), BF16)],
        compiler_params=_params("arbitrary", "arbitrary", "arbitrary"),
        name="mla_attention",
    )(q, tab, kv, kv, kv, kv, k_rope)


LNR_TM = 256
SLOT_LANES = 128


def _ln_router_kernel(h_ref, mix_ref, g_ref, b_ref, wr_ref, br_ref,
                      ho_ref, hb_ref, idx_ref, gate_ref, rank_ref, cnt_ref, carry_ref):
    @pl.when(pl.program_id(0) == 0)
    def _():
        carry_ref[...] = jnp.zeros_like(carry_ref)

    x = DEEPNORM_ALPHA * h_ref[...] + mix_ref[...]
    mu = jnp.mean(x, axis=-1, keepdims=True)
    xc = x - mu
    var = jnp.mean(xc * xc, axis=-1, keepdims=True)
    hn = xc * lax.rsqrt(var + LN_EPS) * g_ref[...] + b_ref[...]
    ho_ref[...] = hn
    hb_ref[...] = hn.astype(BF16)

    logits = jnp.dot(hn, wr_ref[...], precision=lax.Precision.HIGHEST,
                     preferred_element_type=F32) + br_ref[...]
    tm = logits.shape[0]
    lane_e = lax.broadcasted_iota(jnp.int32, (tm, N_EXPERTS), 1)
    lane_s = lax.broadcasted_iota(jnp.int32, (tm, SLOT_LANES), 1)
    work = logits
    sel = jnp.zeros((tm, N_EXPERTS), F32)
    top_v, top_i, hot = [], [], []
    for _ in range(TOP_K):
        mx = jnp.max(work, axis=-1, keepdims=True)
        ix = jnp.min(jnp.where(work == mx, lane_e, N_EXPERTS), axis=-1, keepdims=True)
        oh = lane_e == ix
        work = jnp.where(oh, -jnp.inf, work)
        sel = sel + oh.astype(F32)
        top_v.append(mx)
        top_i.append(ix)
        hot.append(oh)
    ex = [jnp.exp(v - top_v[0]) for v in top_v]
    den = ex[0] + ex[1] + ex[2] + ex[3]

    r = lax.broadcasted_iota(jnp.int32, (tm, tm), 0)
    c = lax.broadcasted_iota(jnp.int32, (tm, tm), 1)
    strict = jnp.where(c < r, 1.0, 0.0).astype(BF16)
    prefix = _dot(strict, sel.astype(BF16)) + carry_ref[0:1, :]
    carry_ref[0:1, :] = carry_ref[0:1, :] + jnp.sum(sel, axis=0, keepdims=True)
    cnt_ref[...] = jnp.broadcast_to(carry_ref[0:1, :], cnt_ref.shape)

    idx_o = jnp.zeros((tm, SLOT_LANES), jnp.int32)
    gate_o = jnp.zeros((tm, SLOT_LANES), F32)
    rank_o = jnp.zeros((tm, SLOT_LANES), jnp.int32)
    for k in range(TOP_K):
        rk = jnp.sum(jnp.where(hot[k], prefix, 0.0), axis=-1, keepdims=True)
        idx_o = jnp.where(lane_s == k, top_i[k], idx_o)
        gate_o = jnp.where(lane_s == k, ex[k] / den, gate_o)
        rank_o = jnp.where(lane_s == k, rk.astype(jnp.int32), rank_o)
    idx_ref[...] = idx_o
    gate_ref[...] = gate_o
    rank_ref[...] = rank_o


def _ln_router(h, mix, ln_g, ln_b, w_router, b_router):
    n_tok, d = h.shape
    tm = LNR_TM
    row = lambda i: (i, 0)
    fixed = lambda i: (0, 0)
    return pl.pallas_call(
        _ln_router_kernel,
        grid=(n_tok // tm,),
        in_specs=[
            pl.BlockSpec((tm, d), row), pl.BlockSpec((tm, d), row),
            pl.BlockSpec((1, d), fixed), pl.BlockSpec((1, d), fixed),
            pl.BlockSpec((d, N_EXPERTS), fixed), pl.BlockSpec((1, N_EXPERTS), fixed),
        ],
        out_specs=[
            pl.BlockSpec((tm, d), row), pl.BlockSpec((tm, d), row),
            pl.BlockSpec((tm, SLOT_LANES), row), pl.BlockSpec((tm, SLOT_LANES), row),
            pl.BlockSpec((tm, SLOT_LANES), row), pl.BlockSpec((8, N_EXPERTS), fixed),
        ],
        out_shape=[
            jax.ShapeDtypeStruct((n_tok, d), F32), jax.ShapeDtypeStruct((n_tok, d), BF16),
            jax.ShapeDtypeStruct((n_tok, SLOT_LANES), jnp.int32),
            jax.ShapeDtypeStruct((n_tok, SLOT_LANES), F32),
            jax.ShapeDtypeStruct((n_tok, SLOT_LANES), jnp.int32),
            jax.ShapeDtypeStruct((8, N_EXPERTS), F32),
        ],
        scratch_shapes=[pltpu.VMEM((8, N_EXPERTS), F32)],
        compiler_params=_params("arbitrary"),
        name="ln_router",
    )(h, mix, ln_g.reshape(1, d), ln_b.reshape(1, d), w_router, b_router.reshape(1, N_EXPERTS))


def _moe_kernel(e_ref, valid_ref, src_ref, nv_ref, x_ref, wg_ref, wl_ref, bg_ref, bl_ref,
                wd_ref, bd_ref, o_ref, act_ref):
    i = pl.program_id(0)
    t = pl.program_id(1)
    n_valid = valid_ref[i]
    n_sub = MOE_TM // MOE_SB

    @pl.when(jnp.logical_and(t < MOE_T1, n_valid > 0))
    def _gate_up():
        wg = wg_ref[...].astype(BF16)
        wl = wl_ref[...].astype(BF16)
        for s in range(n_sub):
            @pl.when(s * MOE_SB < n_valid)
            def _():
                rows = slice(s * MOE_SB, (s + 1) * MOE_SB)
                xs = x_ref[rows, :]
                hg = jnp.minimum(_dot(xs, wg) + bg_ref[...], SWIGLU_LIMIT)
                hl = jnp.clip(_dot(xs, wl) + bl_ref[...], -SWIGLU_LIMIT, SWIGLU_LIMIT)
                a = hg * jax.nn.sigmoid(SWIGLU_ALPHA * hg) * (hl + 1.0)
                act_ref[t, rows, :] = a.astype(BF16)

    @pl.when(jnp.logical_and(t >= MOE_T1, i < nv_ref[0]))
    def _down():
        wd = wd_ref[...].astype(BF16)
        for s in range(n_sub):
            rows = slice(s * MOE_SB, (s + 1) * MOE_SB)

            @pl.when(s * MOE_SB < n_valid)
            def _():
                y = bd_ref[...] + _dot(act_ref[0, rows, :], wd[0:MOE_TH, :])
                for tt in range(1, MOE_T1):
                    y = y + _dot(act_ref[tt, rows, :], wd[tt * MOE_TH:(tt + 1) * MOE_TH, :])
                o_ref[rows, :] = y.astype(o_ref.dtype)

            @pl.when(s * MOE_SB >= n_valid)
            def _():
                o_ref[rows, :] = jnp.zeros((MOE_SB, MOE_TN), o_ref.dtype)


def _moe_experts(x_sorted, blk_exp, blk_valid, blk_src, n_valid_blocks, w_gu, b_gu, w_down, b_down):
    n_rows = x_sorted.shape[0]
    n_blocks = n_rows // MOE_TM
    t_last = MOE_T1 + MOE_T2 - 1

    def tt(i, t, nv):
        return jnp.where(i < nv[0], t, t_last)

    x_map = lambda i, t, e, v, s, nv: (s[i], 0)
    wg_map = lambda i, t, e, v, s, nv: (e[i], 0, jnp.minimum(tt(i, t, nv), MOE_T1 - 1))
    wl_map = lambda i, t, e, v, s, nv: (e[i], 0, MOE_T1 + jnp.minimum(tt(i, t, nv), MOE_T1 - 1))
    wd_map = lambda i, t, e, v, s, nv: (e[i], 0, jnp.maximum(tt(i, t, nv) - MOE_T1, 0))
    o_map = lambda i, t, e, v, s, nv: (s[i], jnp.maximum(tt(i, t, nv) - MOE_T1, 0))
    grid_spec = pltpu.PrefetchScalarGridSpec(
        num_scalar_prefetch=4,
        grid=(n_blocks, MOE_T1 + MOE_T2),
        in_specs=[
            pl.BlockSpec((MOE_TM, D_MODEL), x_map),
            pl.BlockSpec((None, D_MODEL, MOE_TH), wg_map),
            pl.BlockSpec((None, D_MODEL, MOE_TH), wl_map),
            pl.BlockSpec((None, 1, MOE_TH), wg_map),
            pl.BlockSpec((None, 1, MOE_TH), wl_map),
            pl.BlockSpec((None, D_EXPERT, MOE_TN), wd_map),
            pl.BlockSpec((None, 1, MOE_TN), wd_map),
        ],
        out_specs=pl.BlockSpec((MOE_TM, MOE_TN), o_map),
        scratch_shapes=[pltpu.VMEM((MOE_T1, MOE_TM, MOE_TH), BF16)],
    )
    return pl.pallas_call(
        _moe_kernel,
        grid_spec=grid_spec,
        out_shape=jax.ShapeDtypeStruct((n_rows, D_MODEL), BF16),
        compiler_params=_params("arbitrary", "arbitrary"),
        name="moe_experts",
    )(blk_exp, blk_valid, blk_src, n_valid_blocks, x_sorted, w_gu, w_gu,
      b_gu.reshape(N_EXPERTS, 1, 2 * D_EXPERT), b_gu.reshape(N_EXPERTS, 1, 2 * D_EXPERT),
      w_down, b_down.reshape(N_EXPERTS, 1, D_MODEL))


LNC_TM = 128


def _ln_combine_kernel(h_ref, y_ref, gate_ref, g_ref, b_ref, ho_ref, hb_ref):
    d = h_ref.shape[1]
    x = DEEPNORM_ALPHA * h_ref[...]
    for k in range(TOP_K):
        x = x + gate_ref[:, k:k + 1] * y_ref[:, k * d:(k + 1) * d].astype(F32)
    mu = jnp.mean(x, axis=-1, keepdims=True)
    xc = x - mu
    var = jnp.mean(xc * xc, axis=-1, keepdims=True)
    hn = xc * lax.rsqrt(var + LN_EPS) * g_ref[...] + b_ref[...]
    ho_ref[...] = hn
    hb_ref[...] = hn.astype(BF16)


def _ln_combine(h, y_sel, gates, ln_g, ln_b):
    n_tok, d = h.shape
    tm = LNC_TM
    row = lambda i: (i, 0)
    fixed = lambda i: (0, 0)
    return pl.pallas_call(
        _ln_combine_kernel,
        grid=(n_tok // tm,),
        in_specs=[
            pl.BlockSpec((tm, d), row), pl.BlockSpec((tm, TOP_K * d), row),
            pl.BlockSpec((tm, SLOT_LANES), row),
            pl.BlockSpec((1, d), fixed), pl.BlockSpec((1, d), fixed),
        ],
        out_specs=[pl.BlockSpec((tm, d), row), pl.BlockSpec((tm, d), row)],
        out_shape=[jax.ShapeDtypeStruct((n_tok, d), F32), jax.ShapeDtypeStruct((n_tok, d), BF16)],
        compiler_params=_params("arbitrary"),
        name="ln_combine",
    )(h, y_sel, gates, ln_g.reshape(1, d), ln_b.reshape(1, d))


def _moe_layer(h, hb, top_idx, gates, rank, counts, w_gu, b_gu, w_down, b_down, ln_g, ln_b):
    n_tok = h.shape[0]
    n_assign = n_tok * TOP_K
    n_blocks = n_assign // MOE_TM + N_EXPERTS
    n_rows = n_blocks * MOE_TM
    cnt = counts[0].astype(jnp.int32)
    padded = (cnt + MOE_TM - 1) // MOE_TM * MOE_TM
    pad_end = jnp.cumsum(padded)
    pad_start = pad_end - padded
    e_flat = top_idx[:, :TOP_K].reshape(-1)
    dest = pad_start[e_flat] + rank[:, :TOP_K].reshape(-1)
    tok = jnp.arange(n_assign, dtype=jnp.int32) // TOP_K
    row_tok = jnp.zeros((n_rows,), jnp.int32).at[dest].set(tok)
    n_valid_blocks = pad_end[-1] // MOE_TM
    blk = jnp.arange(n_blocks, dtype=jnp.int32)
    src = jnp.minimum(blk, n_valid_blocks - 1)
    blk_exp = jnp.sum((pad_end[None, :] <= (src * MOE_TM)[:, None]).astype(jnp.int32), axis=1)
    blk_exp = jnp.minimum(blk_exp, N_EXPERTS - 1)
    valid = jnp.clip(pad_start[blk_exp] + cnt[blk_exp] - src * MOE_TM, 0, MOE_TM)
    valid = jnp.where(blk < n_valid_blocks, valid, 0).astype(jnp.int32)
    x_sorted = jnp.take(hb, row_tok, axis=0)
    y_sorted = _moe_experts(x_sorted, blk_exp, valid, src.astype(jnp.int32),
                            n_valid_blocks.reshape(1).astype(jnp.int32), w_gu, b_gu, w_down, b_down)
    y_sel = jnp.take(y_sorted, dest, axis=0).reshape(n_tok, TOP_K * D_MODEL)
    return _ln_combine(h, y_sel, gates, ln_g, ln_b)


def kernel(x, positions, ln_g, ln_b, hy_w_in, hy_w_out, gm_ln_g, gm_ln_b, gm_w_s, gm_b_s, ssm_lam_re, ssm_lam_im, ssm_log_dt, ssm_b_re, ssm_b_im, ssm_c_re, ssm_c_im, ssm_d, ssm_w_glu, ssm_b_glu, mla_w_in, mla_q_norm_g, mla_kv_norm_g, mla_w_uq, mla_w_ukv, mla_w_o, moe_w_router, moe_b_router, moe_w_gu, moe_b_gu, moe_w_down, moe_b_down):
    batch, seq, d = x.shape
    n_tok = batch * seq
    h = x.reshape(n_tok, d)
    hb = h.astype(BF16)
    for layer in range(DEPTH):
        i = layer // 2
        if layer % 2 == 0:
            z_gm = _mm([hb], hy_w_in[i], col0=0, n_cols=2 * GM_WIDTH, tm=1024, tn=256,
                       out_dtype=BF16, epilogue="gelu", name="hy_in_gm")
            z_ssm = _mm([hb], hy_w_in[i], col0=2 * GM_WIDTH, n_cols=SSM_WIDTH, tm=1024, tn=256,
                        out_dtype=F32, name="hy_in_ssm")
            y_gm = _spatial_gating(z_gm, gm_ln_g[i], gm_ln_b[i], gm_w_s[i], gm_b_s[i])
            tables = _s5_tables(ssm_lam_re[i], ssm_lam_im[i], ssm_log_dt[i], ssm_b_re[i], ssm_b_im[i],
                                ssm_c_re[i], ssm_c_im[i], ssm_d[i], seq)
            y_act = _s5_mixer(z_ssm, batch, seq, tables)
            y_ssm = _mm([y_act], ssm_w_glu[i], tm=1024, tn=256, out_dtype=BF16, epilogue="glu",
                        bias=ssm_b_glu[i], mul=y_act, name="s5_glu")
            mix = _mm([y_gm, y_ssm], hy_w_out[i], tm=1024, tn=256, out_dtype=F32, name="hy_out")
        else:
            w_in = mla_w_in[i]
            n_main = MLA_Q_RANK + MLA_KV_RANK
            c_main = _mm([hb], w_in, col0=0, n_cols=n_main, tm=1024, tn=256, out_dtype=F32,
                         name="mla_in")
            k_rope = _mm([hb], w_in[:, n_main:], tm=1024, tn=MLA_ROPE, out_dtype=F32, name="mla_in_rope")
            tab, k_rope = _rope_tables(positions, k_rope)
            q = _mm([(c_main, MLA_Q_RANK, 0)], mla_w_uq[i], tm=1024, tn=384, out_dtype=BF16,
                    prologue="rms", gain=mla_q_norm_g[i], name="mla_uq")
            kv = _mm([(c_main, MLA_KV_RANK, MLA_Q_RANK // MLA_KV_RANK)], mla_w_ukv[i], tm=1024, tn=512,
                     out_dtype=BF16, prologue="rms", gain=mla_kv_norm_g[i], name="mla_ukv")
            o = _attention(q, kv, k_rope, tab, batch, seq)
            mix = _mm([o], mla_w_o[i], tm=1024, tn=256, out_dtype=F32, name="mla_out")
        h, hb, top_idx, gates, rank, counts = _ln_router(
            h, mix, ln_g[layer, 0], ln_b[layer, 0], moe_w_router[layer], moe_b_router[layer])
        h, hb = _moe_layer(h, hb, top_idx, gates, rank, counts, moe_w_gu[layer], moe_b_gu[layer],
                           moe_w_down[layer], moe_b_down[layer], ln_g[layer, 1], ln_b[layer, 1])
    return h.reshape(batch, seq, d)
```

```python
import functools
import math

import numpy as np
import jax
import jax.numpy as jnp
from jax import lax
from jax.experimental import pallas as pl
from jax.experimental.pallas import tpu as pltpu

F32 = jnp.float32
BF16 = jnp.bfloat16

D_MODEL = 4096
DEPTH = 2
CHUNK = 64
DEEPNORM_ALPHA = (2 * DEPTH) ** 0.25
LN_EPS = 1e-5
RMS_EPS = 1e-6
LANES = 128

GM_WIDTH = 2048
GM_GROUPS = 8
GM_GROUP_DIM = 256
GM_BLOCK = 128
SSM_WIDTH = 2048
SSM_P = 16
SSM_GROUPS = 128
SSM_N = 64
SSM_T = 16
SSM_OCT = LANES // SSM_P
SSM_NOCT = SSM_GROUPS // SSM_OCT
SSM_PAIRS = SSM_OCT // 2
SSM_SW = SSM_OCT * SSM_N

MLA_HEADS = 32
MLA_Q_RANK = 1024
MLA_KV_RANK = 512
MLA_NOPE = 128
MLA_ROPE = 64
MLA_V = 128
ROPE_THETA = 10000.0

N_EXPERTS = 32
TOP_K = 4
D_EXPERT = 1024
SWIGLU_LIMIT = 7.0
SWIGLU_ALPHA = 1.702

VMEM_LIMIT_BYTES = 56 * 1024 * 1024

MOE_TM = 1024
MOE_SB = 256
MOE_TH = 256
MOE_TN = 512
MOE_T1 = D_EXPERT // MOE_TH
MOE_T2 = D_MODEL // MOE_TN


def _params(*sem):
    return pltpu.CompilerParams(dimension_semantics=sem, vmem_limit_bytes=VMEM_LIMIT_BYTES)


def _dot(a, b):
    return jnp.dot(a, b, preferred_element_type=F32)


def _dot_nt(a, b):
    return lax.dot_general(a, b, (((1,), (1,)), ((), ())), preferred_element_type=F32)


def _mm_kernel(*refs, prologue, epilogue, n_x):
    it = iter(refs)
    x_refs = [next(it) for _ in range(n_x)]
    w_ref = next(it)
    g_ref = next(it) if prologue == "rms" else None
    b_ref = next(it) if epilogue == "glu" else None
    y_ref = next(it) if epilogue == "glu" else None
    o_ref = next(it)
    xs_ref = next(it) if prologue == "rms" else None

    if prologue == "rms":
        @pl.when(pl.program_id(1) == 0)
        def _():
            xf = x_refs[0][...].astype(F32)
            ms = jnp.mean(xf * xf, axis=-1, keepdims=True)
            xs_ref[...] = (xf * lax.rsqrt(ms + RMS_EPS) * g_ref[...]).astype(BF16)
        xs = [xs_ref[...]]
    else:
        xs = [r[...].astype(BF16) for r in x_refs]

    acc = None
    k0 = 0
    for x in xs:
        kk = x.shape[1]
        part = _dot(x, w_ref[k0:k0 + kk, :].astype(BF16))
        acc = part if acc is None else acc + part
        k0 += kk
    if epilogue == "gelu":
        acc = jax.nn.gelu(acc)
    elif epilogue == "glu":
        acc = y_ref[...].astype(F32) * jax.nn.sigmoid(acc + b_ref[...])
    o_ref[...] = acc.astype(o_ref.dtype)


def _mm(xs, w, *, col0=0, n_cols=None, tm, tn, out_dtype, prologue=None, gain=None,
        epilogue=None, bias=None, mul=None, name):
    xs = [(x, x.shape[1], 0) if not isinstance(x, tuple) else x for x in xs]
    m = xs[0][0].shape[0]
    k_total = w.shape[0]
    n_cols = w.shape[1] - col0 if n_cols is None else n_cols
    tn = min(tn, n_cols)
    assert m % tm == 0 and n_cols % tn == 0 and col0 % tn == 0
    cb0 = col0 // tn
    assert sum(kw for _, kw, _ in xs) == k_total
    in_specs = [pl.BlockSpec((tm, kw), functools.partial(lambda i, j, cb: (i, cb), cb=cb))
                for _, kw, cb in xs]
    in_specs.append(pl.BlockSpec((k_total, tn), lambda i, j: (0, cb0 + j)))
    args = [x for x, _, _ in xs] + [w]
    scratch = []
    if prologue == "rms":
        in_specs.append(pl.BlockSpec((1, k_total), lambda i, j: (0, 0)))
        args.append(gain.reshape(1, k_total))
        scratch.append(pltpu.VMEM((tm, k_total), BF16))
    if epilogue == "glu":
        in_specs.append(pl.BlockSpec((1, tn), lambda i, j: (0, j)))
        in_specs.append(pl.BlockSpec((tm, tn), lambda i, j: (i, j)))
        args += [bias.reshape(1, n_cols), mul]
    return pl.pallas_call(
        functools.partial(_mm_kernel, prologue=prologue, epilogue=epilogue, n_x=len(xs)),
        grid=(m // tm, n_cols // tn),
        in_specs=in_specs,
        out_specs=pl.BlockSpec((tm, tn), lambda i, j: (i, j)),
        out_shape=jax.ShapeDtypeStruct((m, n_cols), out_dtype),
        scratch_shapes=scratch,
        compiler_params=_params("arbitrary", "arbitrary"),
        name=name,
    )(*args)


def _gating_kernel(u_ref, v_ref, g_ref, b_ref, ws_ref, bst_ref, o_ref):
    row = lax.broadcasted_iota(jnp.int32, (GM_BLOCK, GM_BLOCK), 0)
    col = lax.broadcasted_iota(jnp.int32, (GM_BLOCK, GM_BLOCK), 1)
    visible = (col // CHUNK) <= (row // CHUNK)
    for g in range(GM_GROUPS):
        sl = slice(g * GM_GROUP_DIM, (g + 1) * GM_GROUP_DIM)
        v = v_ref[:, sl].astype(F32)
        mu = jnp.mean(v, axis=-1, keepdims=True)
        vc = v - mu
        var = jnp.mean(vc * vc, axis=-1, keepdims=True)
        vn = vc * lax.rsqrt(var + LN_EPS) * g_ref[:, sl] + b_ref[:, sl]
        w = jnp.where(visible, ws_ref[g], 0.0).astype(BF16)
        s = _dot(w, vn.astype(BF16)) + bst_ref[:, g:g + 1]
        o_ref[:, sl] = (u_ref[:, sl].astype(F32) * s).astype(o_ref.dtype)


def _spatial_gating(z_gm, ln_g, ln_b, w_s, b_s):
    n_tok = z_gm.shape[0]
    return pl.pallas_call(
        _gating_kernel,
        grid=(n_tok // GM_BLOCK,),
        in_specs=[
            pl.BlockSpec((GM_BLOCK, GM_WIDTH), lambda i: (i, 0)),
            pl.BlockSpec((GM_BLOCK, GM_WIDTH), lambda i: (i, 1)),
            pl.BlockSpec((1, GM_WIDTH), lambda i: (0, 0)),
            pl.BlockSpec((1, GM_WIDTH), lambda i: (0, 0)),
            pl.BlockSpec((GM_GROUPS, GM_BLOCK, GM_BLOCK), lambda i: (0, 0, 0)),
            pl.BlockSpec((GM_BLOCK, GM_GROUPS), lambda i: (0, 0)),
        ],
        out_specs=pl.BlockSpec((GM_BLOCK, GM_WIDTH), lambda i: (i, 0)),
        out_shape=jax.ShapeDtypeStruct((n_tok, GM_WIDTH), BF16),
        compiler_params=_params("arbitrary"),
        name="spatial_gating",
    )(z_gm, z_gm, ln_g.reshape(1, GM_WIDTH), ln_b.reshape(1, GM_WIDTH), w_s, b_s.T)


def _s5_tables(lam_re, lam_im, log_dt, b_re, b_im, c_re, c_im, d_skip, seq):
    hi = lax.Precision.HIGHEST
    n, p, t = SSM_N, SSM_P, SSM_T
    no, npr = SSM_NOCT, SSM_PAIRS
    dt = jnp.exp(log_dt)[:, None]
    mag = jnp.exp(lam_re * dt)
    ab_re = mag * jnp.cos(lam_im * dt)
    ab_im = mag * jnp.sin(lam_im * dt)
    den = lam_re * lam_re + lam_im * lam_im
    num_re = ab_re - 1.0
    coef_re = (num_re * lam_re + ab_im * lam_im) / den
    coef_im = (ab_im * lam_re - num_re * lam_im) / den
    bb_re = coef_re[..., None] * b_re - coef_im[..., None] * b_im
    bb_im = coef_re[..., None] * b_im + coef_im[..., None] * b_re

    def power(k):
        k = jnp.asarray(k, F32)[..., None, None]
        mk = jnp.exp(k * (lam_re * dt))
        return mk * jnp.cos(k * (lam_im * dt)), mk * jnp.sin(k * (lam_im * dt))

    pw_re, pw_im = power(np.arange(t + 1))
    cp_re = c_re[None] * pw_re[:, :, None, :] - c_im[None] * pw_im[:, :, None, :]
    cp_im = c_re[None] * pw_im[:, :, None, :] + c_im[None] * pw_re[:, :, None, :]
    kern = (jnp.einsum('tgpn,gnq->gtpq', cp_re[:t], bb_re, precision=hi)
            - jnp.einsum('tgpn,gnq->gtpq', cp_im[:t], bb_im, precision=hi))
    eye_o = jnp.eye(SSM_OCT, dtype=F32)
    eye_2 = jnp.eye(2, dtype=F32)
    kbd = jnp.einsum('oatpq,ab->otaqbp', kern.reshape(no, SSM_OCT, t, p, p), eye_o)
    kbd = kbd.reshape(no, t, LANES, LANES)
    rv_re, rv_im = pw_re[t - 1::-1], pw_im[t - 1::-1]
    st_re = rv_re[..., None] * bb_re[None] - rv_im[..., None] * bb_im[None]
    st_im = rv_re[..., None] * bb_im[None] + rv_im[..., None] * bb_re[None]
    st = jnp.stack([st_re, st_im], 0).reshape(2, t, no, npr, 2, n, p)
    w_st = jnp.einsum('rjoxanq,ab->ojxraqbn', st, eye_2).reshape(no, t, npr, 2, 2 * p, 2 * n)
    wc = jnp.stack([cp_re[1:], -cp_im[1:]], 0).reshape(2, t, no, npr, 2, p, n)
    w_ct = jnp.einsum('rioxapn,ab->oixrapbn', wc, eye_2).reshape(no, t, npr, 2, 2 * p, 2 * n)
    n_lvl = int(math.log2(seq // t))
    lv_re, lv_im = power(t * 2 ** np.arange(n_lvl))
    lr = lv_re.reshape(n_lvl, no, SSM_SW)
    li = lv_im.reshape(n_lvl, no, SSM_SW)
    lvl = jnp.stack([jnp.concatenate([lr, lr], -1), jnp.concatenate([-li, li], -1)], axis=1)
    lvl = lvl.transpose(2, 0, 1, 3).reshape(no, 2 * n_lvl, 2 * SSM_SW)
    d_vec = jnp.tile(d_skip.reshape(no, 1, LANES), (1, 1, t))
    return kbd.astype(BF16), w_st.astype(BF16), w_ct.astype(BF16), lvl, d_vec


def _s5_kernel(u_ref, kbd_ref, wst_ref, wct_ref, lvl_ref, d_ref, o_ref, toe_ref, st_ref, ct_ref,
               *, n_chunks, n_lvl):
    t = SSM_T

    @pl.when(pl.program_id(0) == 0)
    def _():
        toe_ref[...] = jnp.zeros_like(toe_ref)
        st_ref[...] = jnp.zeros_like(st_ref)
        ct_ref[...] = jnp.zeros_like(ct_ref)

    for i in range(t):
        for j in range(i + 1):
            toe_ref[j * LANES:(j + 1) * LANES, i * LANES:(i + 1) * LANES] = kbd_ref[i - j]
    for j in range(t):
        for x in range(SSM_PAIRS):
            for r in range(2):
                rows = slice(j * LANES + x * 2 * SSM_P, j * LANES + (x + 1) * 2 * SSM_P)
                lanes = slice(r * SSM_SW + x * LANES, r * SSM_SW + (x + 1) * LANES)
                st_ref[rows, lanes] = wst_ref[j, x, r]
                ct_ref[rows, lanes] = wct_ref[j, x, r]

    u = u_ref[...]
    ub = u.astype(BF16)
    h = _dot(ub, st_ref[...])
    cidx = lax.broadcasted_iota(jnp.int32, (u.shape[0], 1), 0) % n_chunks
    for lv in range(n_lvl):
        d = 1 << lv
        a_rr = lvl_ref[2 * lv:2 * lv + 1, :]
        a_is = lvl_ref[2 * lv + 1:2 * lv + 2, :]
        sh = jnp.where(cidx >= d, pltpu.roll(h, d, 0), 0.0)
        h = h + sh * a_rr + pltpu.roll(sh, SSM_SW, 1) * a_is
    h_prev = jnp.where(cidx >= 1, pltpu.roll(h, 1, 0), 0.0)
    y = _dot(ub, toe_ref[...]) + _dot_nt(h_prev.astype(BF16), ct_ref[...]) + d_ref[...] * u
    o_ref[...] = jax.nn.gelu(y).astype(o_ref.dtype)


def _s5_mixer(z_ssm, batch, seq, tables):
    kbd, w_st, w_ct, lvl, d_vec = tables
    t, no = SSM_T, SSM_NOCT
    n_chunks = seq // t
    rows = batch * n_chunks
    width = t * LANES
    n_lvl = lvl.shape[1] // 2
    u = z_ssm.reshape(rows, t, no, LANES).transpose(2, 0, 1, 3).reshape(no, rows, width)
    y = pl.pallas_call(
        functools.partial(_s5_kernel, n_chunks=n_chunks, n_lvl=n_lvl),
        grid=(no,),
        in_specs=[
            pl.BlockSpec((None, rows, width), lambda i: (i, 0, 0)),
            pl.BlockSpec((None, t, LANES, LANES), lambda i: (i, 0, 0, 0)),
            pl.BlockSpec((None, t, SSM_PAIRS, 2, 2 * SSM_P, LANES), lambda i: (i, 0, 0, 0, 0, 0)),
            pl.BlockSpec((None, t, SSM_PAIRS, 2, 2 * SSM_P, LANES), lambda i: (i, 0, 0, 0, 0, 0)),
            pl.BlockSpec((None, 2 * n_lvl, 2 * SSM_SW), lambda i: (i, 0, 0)),
            pl.BlockSpec((None, 1, width), lambda i: (i, 0, 0)),
        ],
        out_specs=pl.BlockSpec((None, rows, width), lambda i: (i, 0, 0)),
        out_shape=jax.ShapeDtypeStruct((no, rows, width), BF16),
        scratch_shapes=[
            pltpu.VMEM((width, width), BF16),
            pltpu.VMEM((width, 2 * SSM_SW), BF16),
            pltpu.VMEM((width, 2 * SSM_SW), BF16),
        ],
        compiler_params=_params("arbitrary"),
        name="s5_mixer",
    )(u, kbd, w_st, w_ct, lvl, d_vec)
    return y.reshape(no, rows, t, LANES).transpose(1, 2, 0, 3).reshape(batch * seq, SSM_WIDTH)


def _rope_kernel(pos_ref, freq_ref, kr_ref, tab_ref, kro_ref):
    ang = pos_ref[...].astype(F32) * freq_ref[...]
    lane = lax.broadcasted_iota(jnp.int32, ang.shape, 1)
    sin = jnp.sin(ang)
    tab = jnp.where(lane < MLA_ROPE, jnp.cos(ang), jnp.where(lane < MLA_ROPE + MLA_ROPE // 2, -sin, sin))
    tab_ref[...] = tab
    kr = kr_ref[...]
    half = MLA_ROPE // 2
    sw = jnp.concatenate([kr[:, half:], kr[:, :half]], axis=1)
    kro_ref[...] = (kr * tab[:, :MLA_ROPE] + sw * tab[:, MLA_ROPE:]).astype(kro_ref.dtype)


def _rope_tables(positions, k_rope):
    n_tok = k_rope.shape[0]
    tm = 512
    inv_freq = ROPE_THETA ** (-np.arange(0, MLA_ROPE, 2, dtype=np.float64) / MLA_ROPE)
    freq = jnp.asarray(np.tile(inv_freq, 4)[None, :], F32)
    return pl.pallas_call(
        _rope_kernel,
        grid=(n_tok // tm,),
        in_specs=[
            pl.BlockSpec((tm, 1), lambda i: (i, 0)),
            pl.BlockSpec((1, 2 * MLA_ROPE), lambda i: (0, 0)),
            pl.BlockSpec((tm, MLA_ROPE), lambda i: (i, 0)),
        ],
        out_specs=[
            pl.BlockSpec((tm, 2 * MLA_ROPE), lambda i: (i, 0)),
            pl.BlockSpec((tm, MLA_ROPE), lambda i: (i, 0)),
        ],
        out_shape=[
            jax.ShapeDtypeStruct((n_tok, 2 * MLA_ROPE), F32),
            jax.ShapeDtypeStruct((n_tok, MLA_ROPE), BF16),
        ],
        compiler_params=_params("arbitrary"),
        name="rope_tables",
    )(positions.reshape(n_tok, 1), freq, k_rope)


ATT_TK = 256
ATT_TQ = 2 * ATT_TK
ATT_DK = 2 * MLA_NOPE


def _attn_kernel(q_ref, tab_ref, k0_ref, v0_ref, k1_ref, v1_ref, kr_ref, o_ref, kc_ref):
    qi = pl.program_id(2)
    hd = MLA_NOPE + MLA_ROPE
    scale = hd ** -0.5
    half = MLA_ROPE // 2
    tk = ATT_TK

    @pl.when(qi == 0)
    def _():
        for hh, k_ref in enumerate((k0_ref, k1_ref)):
            kc_ref[hh, :, 0:MLA_NOPE] = k_ref[...]
            kc_ref[hh, :, MLA_NOPE:hd] = kr_ref[...]
            kc_ref[hh, :, hd:] = jnp.zeros((kr_ref.shape[0], ATT_DK - hd), BF16)

    q = q_ref[...]
    cosf = tab_ref[:, :MLA_ROPE]
    sinf = tab_ref[:, MLA_ROPE:]
    qc = []
    for hh in range(2):
        qn = q[:, hh * hd:hh * hd + MLA_NOPE].astype(F32) * scale
        qr = q[:, hh * hd + MLA_NOPE:(hh + 1) * hd].astype(F32)
        sw = jnp.concatenate([qr[:, half:], qr[:, :half]], axis=1)
        qr = (qr * cosf + sw * sinf) * scale
        pad = jnp.zeros((ATT_TQ, ATT_DK - hd), F32)
        qc.append(jnp.concatenate([qn, qr, pad], axis=1).astype(BF16))
    v_refs = (v0_ref, v1_ref)
    row = lax.broadcasted_iota(jnp.int32, (tk, tk), 0)
    col = lax.broadcasted_iota(jnp.int32, (tk, tk), 1)
    visible = (col // CHUNK) <= (row // CHUNK)

    def update(state, hh, sub, j, masked):
        m, l, acc = state
        start = pl.multiple_of(j * tk, tk)
        ks = kc_ref[hh, pl.ds(start, tk), :]
        vs = v_refs[hh][pl.ds(start, tk), :]
        s = _dot_nt(qc[hh][sub * tk:(sub + 1) * tk], ks)
        if masked:
            s = jnp.where(visible, s, -1e30)
        m_new = jnp.maximum(m, jnp.max(s, axis=-1, keepdims=True))
        alpha = jnp.exp(m - m_new)
        p = jnp.exp(s - m_new)
        l = alpha * l + jnp.sum(p, axis=-1, keepdims=True)
        acc = alpha * acc + _dot(p.astype(BF16), vs)
        return m_new, l, acc

    chains = [(hh, sub) for hh in range(2) for sub in range(2)]

    def body(j, states):
        return tuple(update(st, hh, sub, j, False) for st, (hh, sub) in zip(states, chains))

    init = tuple((jnp.full((tk, 1), -1e30, F32), jnp.zeros((tk, 1), F32), jnp.zeros((tk, MLA_V), F32))
                 for _ in chains)
    states = list(lax.fori_loop(0, 2 * qi, body, init))
    for c, (hh, sub) in enumerate(chains):
        st = update(states[c], hh, sub, 2 * qi, masked=(sub == 0))
        if sub == 1:
            st = update(st, hh, sub, 2 * qi + 1, masked=True)
        _, l, acc = st
        o_ref[sub * tk:(sub + 1) * tk, hh * MLA_V:(hh + 1) * MLA_V] = (acc / l).astype(o_ref.dtype)


def _attention(q, kv, k_rope, tab, batch, seq):
    n_tok = batch * seq
    nq = seq // ATT_TQ
    kvb = lambda off: pl.BlockSpec((seq, MLA_NOPE), lambda b, hp, qi: (b, 4 * hp + off))
    return pl.pallas_call(
        _attn_kernel,
        grid=(batch, MLA_HEADS // 2, nq),
        in_specs=[
            pl.BlockSpec((ATT_TQ, 2 * (MLA_NOPE + MLA_ROPE)), lambda b, hp, qi: (b * nq + qi, hp)),
            pl.BlockSpec((ATT_TQ, 2 * MLA_ROPE), lambda b, hp, qi: (b * nq + qi, 0)),
            kvb(0), kvb(1), kvb(2), kvb(3),
            pl.BlockSpec((seq, MLA_ROPE), lambda b, hp, qi: (b, 0)),
        ],
        out_specs=pl.BlockSpec((ATT_TQ, 2 * MLA_V), lambda b, hp, qi: (b * nq + qi, hp)),
        out_shape=jax.ShapeDtypeStruct((n_tok, MLA_HEADS * MLA_V), BF16),
        scratch_shapes=[pltpu.VMEM((2, seq, ATT_DK), BF16)],
        compiler_params=_params("arbitrary", "arbitrary", "arbitrary"),
        name="mla_attention",
    )(q, tab, kv, kv, kv, kv, k_rope)


LNR_TM = 256
SLOT_LANES = 128


def _ln_router_kernel(h_ref, mix_ref, g_ref, b_ref, wr_ref, br_ref,
                      ho_ref, hb_ref, idx_ref, gate_ref, rank_ref, cnt_ref, carry_ref):
    @pl.when(pl.program_id(0) == 0)
    def _():
        carry_ref[...] = jnp.zeros_like(carry_ref)

    x = DEEPNORM_ALPHA * h_ref[...] + mix_ref[...]
    mu = jnp.mean(x, axis=-1, keepdims=True)
    xc = x - mu
    var = jnp.mean(xc * xc, axis=-1, keepdims=True)
    hn = xc * lax.rsqrt(var + LN_EPS) * g_ref[...] + b_ref[...]
    ho_ref[...] = hn
    hb_ref[...] = hn.astype(BF16)

    logits = jnp.dot(hn, wr_ref[...], precision=lax.Precision.HIGHEST,
                     preferred_element_type=F32) + br_ref[...]
    tm = logits.shape[0]
    lane_e = lax.broadcasted_iota(jnp.int32, (tm, N_EXPERTS), 1)
    lane_s = lax.broadcasted_iota(jnp.int32, (tm, SLOT_LANES), 1)
    work = logits
    sel = jnp.zeros((tm, N_EXPERTS), F32)
    top_v, top_i, hot = [], [], []
    for _ in range(TOP_K):
        mx = jnp.max(work, axis=-1, keepdims=True)
        ix = jnp.min(jnp.where(work == mx, lane_e, N_EXPERTS), axis=-1, keepdims=True)
        oh = lane_e == ix
        work = jnp.where(oh, -jnp.inf, work)
        sel = sel + oh.astype(F32)
        top_v.append(mx)
        top_i.append(ix)
        hot.append(oh)
    ex = [jnp.exp(v - top_v[0]) for v in top_v]
    den = ex[0] + ex[1] + ex[2] + ex[3]

    r = lax.broadcasted_iota(jnp.int32, (tm, tm), 0)
    c = lax.broadcasted_iota(jnp.int32, (tm, tm), 1)
    strict = jnp.where(c < r, 1.0, 0.0).astype(BF16)
    prefix = _dot(strict, sel.astype(BF16)) + carry_ref[0:1, :]
    carry_ref[0:1, :] = carry_ref[0:1, :] + jnp.sum(sel, axis=0, keepdims=True)
    cnt_ref[...] = jnp.broadcast_to(carry_ref[0:1, :], cnt_ref.shape)

    idx_o = jnp.zeros((tm, SLOT_LANES), jnp.int32)
    gate_o = jnp.zeros((tm, SLOT_LANES), F32)
    rank_o = jnp.zeros((tm, SLOT_LANES), jnp.int32)
    for k in range(TOP_K):
        rk = jnp.sum(jnp.where(hot[k], prefix, 0.0), axis=-1, keepdims=True)
        idx_o = jnp.where(lane_s == k, top_i[k], idx_o)
        gate_o = jnp.where(lane_s == k, ex[k] / den, gate_o)
        rank_o = jnp.where(lane_s == k, rk.astype(jnp.int32), rank_o)
    idx_ref[...] = idx_o
    gate_ref[...] = gate_o
    rank_ref[...] = rank_o


def _ln_router(h, mix, ln_g, ln_b, w_router, b_router):
    n_tok, d = h.shape
    tm = LNR_TM
    row = lambda i: (i, 0)
    fixed = lambda i: (0, 0)
    return pl.pallas_call(
        _ln_router_kernel,
        grid=(n_tok // tm,),
        in_specs=[
            pl.BlockSpec((tm, d), row), pl.BlockSpec((tm, d), row),
            pl.BlockSpec((1, d), fixed), pl.BlockSpec((1, d), fixed),
            pl.BlockSpec((d, N_EXPERTS), fixed), pl.BlockSpec((1, N_EXPERTS), fixed),
        ],
        out_specs=[
            pl.BlockSpec((tm, d), row), pl.BlockSpec((tm, d), row),
            pl.BlockSpec((tm, SLOT_LANES), row), pl.BlockSpec((tm, SLOT_LANES), row),
            pl.BlockSpec((tm, SLOT_LANES), row), pl.BlockSpec((8, N_EXPERTS), fixed),
        ],
        out_shape=[
            jax.ShapeDtypeStruct((n_tok, d), F32), jax.ShapeDtypeStruct((n_tok, d), BF16),
            jax.ShapeDtypeStruct((n_tok, SLOT_LANES), jnp.int32),
            jax.ShapeDtypeStruct((n_tok, SLOT_LANES), F32),
            jax.ShapeDtypeStruct((n_tok, SLOT_LANES), jnp.int32),
            jax.ShapeDtypeStruct((8, N_EXPERTS), F32),
        ],
        scratch_shapes=[pltpu.VMEM((8, N_EXPERTS), F32)],
        compiler_params=_params("arbitrary"),
        name="ln_router",
    )(h, mix, ln_g.reshape(1, d), ln_b.reshape(1, d), w_router, b_router.reshape(1, N_EXPERTS))


def _moe_kernel(e_ref, valid_ref, src_ref, nv_ref, x_ref, wg_ref, wl_ref, bg_ref, bl_ref,
                wd_ref, bd_ref, o_ref, act_ref):
    i = pl.program_id(0)
    t = pl.program_id(1)
    n_valid = valid_ref[i]
    n_sub = MOE_TM // MOE_SB

    @pl.when(jnp.logical_and(t < MOE_T1, n_valid > 0))
    def _gate_up():
        wg = wg_ref[...].astype(BF16)
        wl = wl_ref[...].astype(BF16)
        for s in range(n_sub):
            @pl.when(s * MOE_SB < n_valid)
            def _():
                rows = slice(s * MOE_SB, (s + 1) * MOE_SB)
                xs = x_ref[rows, :]
                hg = jnp.minimum(_dot(xs, wg) + bg_ref[...], SWIGLU_LIMIT)
                hl = jnp.clip(_dot(xs, wl) + bl_ref[...], -SWIGLU_LIMIT, SWIGLU_LIMIT)
                a = hg * jax.nn.sigmoid(SWIGLU_ALPHA * hg) * (hl + 1.0)
                act_ref[t, rows, :] = a.astype(BF16)

    @pl.when(jnp.logical_and(t >= MOE_T1, i < nv_ref[0]))
    def _down():
        wd = wd_ref[...].astype(BF16)
        for s in range(n_sub):
            rows = slice(s * MOE_SB, (s + 1) * MOE_SB)

            @pl.when(s * MOE_SB < n_valid)
            def _():
                y = bd_ref[...] + _dot(act_ref[0, rows, :], wd[0:MOE_TH, :])
                for tt in range(1, MOE_T1):
                    y = y + _dot(act_ref[tt, rows, :], wd[tt * MOE_TH:(tt + 1) * MOE_TH, :])
                o_ref[rows, :] = y.astype(o_ref.dtype)

            @pl.when(s * MOE_SB >= n_valid)
            def _():
                o_ref[rows, :] = jnp.zeros((MOE_SB, MOE_TN), o_ref.dtype)


def _moe_experts(x_sorted, blk_exp, blk_valid, blk_src, n_valid_blocks, w_gu, b_gu, w_down, b_down,
                 layer):
    n_rows = x_sorted.shape[0]
    n_blocks = n_rows // MOE_TM
    t_last = MOE_T1 + MOE_T2 - 1

    def tt(i, t, nv):
        return jnp.where(i < nv[0], t, t_last)

    x_map = lambda i, t, e, v, s, nv: (s[i], 0)
    wg_map = lambda i, t, e, v, s, nv: (layer, e[i], 0, jnp.minimum(tt(i, t, nv), MOE_T1 - 1))
    wl_map = lambda i, t, e, v, s, nv: (layer, e[i], 0, MOE_T1 + jnp.minimum(tt(i, t, nv), MOE_T1 - 1))
    wd_map = lambda i, t, e, v, s, nv: (layer, e[i], 0, jnp.maximum(tt(i, t, nv) - MOE_T1, 0))
    o_map = lambda i, t, e, v, s, nv: (s[i], jnp.maximum(tt(i, t, nv) - MOE_T1, 0))
    grid_spec = pltpu.PrefetchScalarGridSpec(
        num_scalar_prefetch=4,
        grid=(n_blocks, MOE_T1 + MOE_T2),
        in_specs=[
            pl.BlockSpec((MOE_TM, D_MODEL), x_map),
            pl.BlockSpec((None, None, D_MODEL, MOE_TH), wg_map),
            pl.BlockSpec((None, None, D_MODEL, MOE_TH), wl_map),
            pl.BlockSpec((None, None, 1, MOE_TH), wg_map),
            pl.BlockSpec((None, None, 1, MOE_TH), wl_map),
            pl.BlockSpec((None, None, D_EXPERT, MOE_TN), wd_map),
            pl.BlockSpec((None, None, 1, MOE_TN), wd_map),
        ],
        out_specs=pl.BlockSpec((MOE_TM, MOE_TN), o_map),
        scratch_shapes=[pltpu.VMEM((MOE_T1, MOE_TM, MOE_TH), BF16)],
    )
    return pl.pallas_call(
        _moe_kernel,
        grid_spec=grid_spec,
        out_shape=jax.ShapeDtypeStruct((n_rows, D_MODEL), BF16),
        compiler_params=_params("arbitrary", "arbitrary"),
        name="moe_experts",
    )(blk_exp, blk_valid, blk_src, n_valid_blocks, x_sorted, w_gu, w_gu,
      b_gu.reshape(DEPTH, N_EXPERTS, 1, 2 * D_EXPERT), b_gu.reshape(DEPTH, N_EXPERTS, 1, 2 * D_EXPERT),
      w_down, b_down.reshape(DEPTH, N_EXPERTS, 1, D_MODEL))


LNC_TM = 128


def _ln_combine_kernel(h_ref, y0_ref, y1_ref, y2_ref, y3_ref, gate_ref, g_ref, b_ref, ho_ref, hb_ref):
    x = DEEPNORM_ALPHA * h_ref[...]
    for k, y_ref in enumerate((y0_ref, y1_ref, y2_ref, y3_ref)):
        x = x + gate_ref[:, k:k + 1] * y_ref[...].astype(F32)
    mu = jnp.mean(x, axis=-1, keepdims=True)
    xc = x - mu
    var = jnp.mean(xc * xc, axis=-1, keepdims=True)
    hn = xc * lax.rsqrt(var + LN_EPS) * g_ref[...] + b_ref[...]
    ho_ref[...] = hn
    hb_ref[...] = hn.astype(BF16)


def _ln_combine(h, y_sel, gates, ln_g, ln_b):
    n_tok, d = h.shape
    tm = LNC_TM
    nblk = n_tok // tm
    row = lambda i: (i, 0)
    fixed = lambda i: (0, 0)
    slot = lambda k: pl.BlockSpec((tm, d), lambda i: (k * nblk + i, 0))
    return pl.pallas_call(
        _ln_combine_kernel,
        grid=(nblk,),
        in_specs=[
            pl.BlockSpec((tm, d), row), slot(0), slot(1), slot(2), slot(3),
            pl.BlockSpec((tm, SLOT_LANES), row),
            pl.BlockSpec((1, d), fixed), pl.BlockSpec((1, d), fixed),
        ],
        out_specs=[pl.BlockSpec((tm, d), row), pl.BlockSpec((tm, d), row)],
        out_shape=[jax.ShapeDtypeStruct((n_tok, d), F32), jax.ShapeDtypeStruct((n_tok, d), BF16)],
        compiler_params=_params("arbitrary"),
        name="ln_combine",
    )(h, y_sel, y_sel, y_sel, y_sel, gates, ln_g.reshape(1, d), ln_b.reshape(1, d))


def _moe_layer(h, hb, top_idx, gates, rank, counts, w_gu, b_gu, w_down, b_down, layer, ln_g, ln_b):
    n_tok = h.shape[0]
    n_assign = n_tok * TOP_K
    n_blocks = n_assign // MOE_TM + N_EXPERTS
    n_rows = n_blocks * MOE_TM
    cnt = counts[0].astype(jnp.int32)
    padded = (cnt + MOE_TM - 1) // MOE_TM * MOE_TM
    pad_end = jnp.cumsum(padded)
    pad_start = pad_end - padded
    e_flat = top_idx[:, :TOP_K].reshape(-1)
    dest = pad_start[e_flat] + rank[:, :TOP_K].reshape(-1)
    tok = jnp.arange(n_assign, dtype=jnp.int32) // TOP_K
    row_tok = jnp.zeros((n_rows,), jnp.int32).at[dest].set(tok)
    n_valid_blocks = pad_end[-1] // MOE_TM
    blk = jnp.arange(n_blocks, dtype=jnp.int32)
    src = jnp.minimum(blk, n_valid_blocks - 1)
    blk_exp = jnp.sum((pad_end[None, :] <= (src * MOE_TM)[:, None]).astype(jnp.int32), axis=1)
    blk_exp = jnp.minimum(blk_exp, N_EXPERTS - 1)
    valid = jnp.clip(pad_start[blk_exp] + cnt[blk_exp] - src * MOE_TM, 0, MOE_TM)
    valid = jnp.where(blk < n_valid_blocks, valid, 0).astype(jnp.int32)
    x_sorted = jnp.take(hb, row_tok, axis=0, mode="clip")
    y_sorted = _moe_experts(x_sorted, blk_exp, valid, src.astype(jnp.int32),
                            n_valid_blocks.reshape(1).astype(jnp.int32), w_gu, b_gu, w_down, b_down,
                            layer)
    dest_by_slot = dest.reshape(n_tok, TOP_K).T.reshape(-1)
    y_sel = jnp.take(y_sorted, dest_by_slot, axis=0, mode="clip")
    return _ln_combine(h, y_sel, gates, ln_g, ln_b)


def kernel(x, positions, ln_g, ln_b, hy_w_in, hy_w_out, gm_ln_g, gm_ln_b, gm_w_s, gm_b_s, ssm_lam_re, ssm_lam_im, ssm_log_dt, ssm_b_re, ssm_b_im, ssm_c_re, ssm_c_im, ssm_d, ssm_w_glu, ssm_b_glu, mla_w_in, mla_q_norm_g, mla_kv_norm_g, mla_w_uq, mla_w_ukv, mla_w_o, moe_w_router, moe_b_router, moe_w_gu, moe_b_gu, moe_w_down, moe_b_down):
    batch, seq, d = x.shape
    n_tok = batch * seq
    h = x.reshape(n_tok, d)
    hb = h.astype(BF16)
    for layer in range(DEPTH):
        i = layer // 2
        if layer % 2 == 0:
            z_gm = _mm([hb], hy_w_in[i], col0=0, n_cols=2 * GM_WIDTH, tm=1024, tn=256,
                       out_dtype=BF16, epilogue="gelu", name="hy_in_gm")
            z_ssm = _mm([hb], hy_w_in[i], col0=2 * GM_WIDTH, n_cols=SSM_WIDTH, tm=1024, tn=256,
                        out_dtype=F32, name="hy_in_ssm")
            y_gm = _spatial_gating(z_gm, gm_ln_g[i], gm_ln_b[i], gm_w_s[i], gm_b_s[i])
            tables = _s5_tables(ssm_lam_re[i], ssm_lam_im[i], ssm_log_dt[i], ssm_b_re[i], ssm_b_im[i],
                                ssm_c_re[i], ssm_c_im[i], ssm_d[i], seq)
            y_act = _s5_mixer(z_ssm, batch, seq, tables)
            y_ssm = _mm([y_act], ssm_w_glu[i], tm=1024, tn=256, out_dtype=BF16, epilogue="glu",
                        bias=ssm_b_glu[i], mul=y_act, name="s5_glu")
            mix = _mm([y_gm, y_ssm], hy_w_out[i], tm=1024, tn=256, out_dtype=F32, name="hy_out")
        else:
            w_in = mla_w_in[i]
            n_main = MLA_Q_RANK + MLA_KV_RANK
            c_main = _mm([hb], w_in, col0=0, n_cols=n_main, tm=1024, tn=256, out_dtype=F32,
                         name="mla_in")
            k_rope = _mm([hb], w_in[:, n_main:], tm=1024, tn=MLA_ROPE, out_dtype=F32, name="mla_in_rope")
            tab, k_rope = _rope_tables(positions, k_rope)
            q = _mm([(c_main, MLA_Q_RANK, 0)], mla_w_uq[i], tm=1024, tn=384, out_dtype=BF16,
                    prologue="rms", gain=mla_q_norm_g[i], name="mla_uq")
            kv = _mm([(c_main, MLA_KV_RANK, MLA_Q_RANK // MLA_KV_RANK)], mla_w_ukv[i], tm=1024, tn=512,
                     out_dtype=BF16, prologue="rms", gain=mla_kv_norm_g[i], name="mla_ukv")
            o = _attention(q, kv, k_rope, tab, batch, seq)
            mix = _mm([o], mla_w_o[i], tm=1024, tn=256, out_dtype=F32, name="mla_out")
        h, hb, top_idx, gates, rank, counts = _ln_router(
            h, mix, ln_g[layer, 0], ln_b[layer, 0], moe_w_router[layer], moe_b_router[layer])
        h, hb = _moe_layer(h, hb, top_idx, gates, rank, counts, moe_w_gu, moe_b_gu,
                           moe_w_down, moe_b_down, layer, ln_g[layer, 1], ln_b[layer, 1])
    return h.reshape(batch, seq, d)
```

```python
import functools
import math

import numpy as np
import jax
import jax.numpy as jnp
from jax import lax
from jax.experimental import pallas as pl
from jax.experimental.pallas import tpu as pltpu

F32 = jnp.float32
BF16 = jnp.bfloat16
U32 = jnp.uint32
HI_MASK = np.uint32(0xFFFF0000)

D_MODEL = 4096
DEPTH = 2
CHUNK = 64
DEEPNORM_ALPHA = (2 * DEPTH) ** 0.25
LN_EPS = 1e-5
RMS_EPS = 1e-6
LANES = 128

GM_WIDTH = 2048
GM_GROUPS = 8
GM_GROUP_DIM = 256
GM_BLOCK = 128
SSM_WIDTH = 2048
SSM_P = 16
SSM_GROUPS = 128
SSM_N = 64
SSM_T = 16
SSM_OCT = LANES // SSM_P
SSM_NOCT = SSM_GROUPS // SSM_OCT
SSM_PAIRS = SSM_OCT // 2
SSM_SW = SSM_OCT * SSM_N

MLA_HEADS = 32
MLA_Q_RANK = 1024
MLA_KV_RANK = 512
MLA_NOPE = 128
MLA_ROPE = 64
MLA_V = 128
ROPE_THETA = 10000.0

N_EXPERTS = 32
TOP_K = 4
D_EXPERT = 1024
SWIGLU_LIMIT = 7.0
SWIGLU_ALPHA = 1.702

VMEM_LIMIT_BYTES = 56 * 1024 * 1024


def _params(*sem):
    return pltpu.CompilerParams(dimension_semantics=sem, vmem_limit_bytes=VMEM_LIMIT_BYTES)


def _dot(a, b):
    return jnp.dot(a, b, preferred_element_type=F32)


def _dot_nt(a, b):
    return lax.dot_general(a, b, (((1,), (1,)), ((), ())), preferred_element_type=F32)


def _mm_kernel(*refs, prologue, epilogue, n_x):
    it = iter(refs)
    x_refs = [next(it) for _ in range(n_x)]
    w_ref = next(it)
    g_ref = next(it) if prologue == "rms" else None
    b_ref = next(it) if epilogue == "glu" else None
    y_ref = next(it) if epilogue == "glu" else None
    o_ref = next(it)
    xs_ref = next(it) if prologue == "rms" else None

    if prologue == "rms":
        @pl.when(pl.program_id(1) == 0)
        def _():
            xf = x_refs[0][...].astype(F32)
            ms = jnp.mean(xf * xf, axis=-1, keepdims=True)
            xs_ref[...] = (xf * lax.rsqrt(ms + RMS_EPS) * g_ref[...]).astype(BF16)
        xs = [xs_ref[...]]
    else:
        xs = [r[...].astype(BF16) for r in x_refs]

    acc = None
    k0 = 0
    for x in xs:
        kk = x.shape[1]
        part = _dot(x, w_ref[k0:k0 + kk, :].astype(BF16))
        acc = part if acc is None else acc + part
        k0 += kk
    if epilogue == "gelu":
        acc = jax.nn.gelu(acc)
    elif epilogue == "glu":
        acc = y_ref[...].astype(F32) * jax.nn.sigmoid(acc + b_ref[...])
    o_ref[...] = acc.astype(o_ref.dtype)


def _mm(xs, w, *, col0=0, n_cols=None, tm, tn, out_dtype, prologue=None, gain=None,
        epilogue=None, bias=None, mul=None, name):
    xs = [(x, x.shape[1], 0) if not isinstance(x, tuple) else x for x in xs]
    m = xs[0][0].shape[0]
    k_total = w.shape[0]
    n_cols = w.shape[1] - col0 if n_cols is None else n_cols
    tn = min(tn, n_cols)
    assert m % tm == 0 and n_cols % tn == 0 and col0 % tn == 0
    cb0 = col0 // tn
    assert sum(kw for _, kw, _ in xs) == k_total
    in_specs = [pl.BlockSpec((tm, kw), functools.partial(lambda i, j, cb: (i, cb), cb=cb))
                for _, kw, cb in xs]
    in_specs.append(pl.BlockSpec((k_total, tn), lambda i, j: (0, cb0 + j)))
    args = [x for x, _, _ in xs] + [w]
    scratch = []
    if prologue == "rms":
        in_specs.append(pl.BlockSpec((1, k_total), lambda i, j: (0, 0)))
        args.append(gain.reshape(1, k_total))
        scratch.append(pltpu.VMEM((tm, k_total), BF16))
    if epilogue == "glu":
        in_specs.append(pl.BlockSpec((1, tn), lambda i, j: (0, j)))
        in_specs.append(pl.BlockSpec((tm, tn), lambda i, j: (i, j)))
        args += [bias.reshape(1, n_cols), mul]
    return pl.pallas_call(
        functools.partial(_mm_kernel, prologue=prologue, epilogue=epilogue, n_x=len(xs)),
        grid=(m // tm, n_cols // tn),
        in_specs=in_specs,
        out_specs=pl.BlockSpec((tm, tn), lambda i, j: (i, j)),
        out_shape=jax.ShapeDtypeStruct((m, n_cols), out_dtype),
        scratch_shapes=scratch,
        compiler_params=_params("arbitrary", "arbitrary"),
        name=name,
    )(*args)


def _gating_kernel(u_ref, v_ref, g_ref, b_ref, ws_ref, bst_ref, o_ref):
    row = lax.broadcasted_iota(jnp.int32, (GM_BLOCK, GM_BLOCK), 0)
    col = lax.broadcasted_iota(jnp.int32, (GM_BLOCK, GM_BLOCK), 1)
    visible = (col // CHUNK) <= (row // CHUNK)
    for g in range(GM_GROUPS):
        sl = slice(g * GM_GROUP_DIM, (g + 1) * GM_GROUP_DIM)
        v = v_ref[:, sl].astype(F32)
        mu = jnp.mean(v, axis=-1, keepdims=True)
        vc = v - mu
        var = jnp.mean(vc * vc, axis=-1, keepdims=True)
        vn = vc * lax.rsqrt(var + LN_EPS) * g_ref[:, sl] + b_ref[:, sl]
        w = jnp.where(visible, ws_ref[g], 0.0).astype(BF16)
        s = _dot(w, vn.astype(BF16)) + bst_ref[:, g:g + 1]
        o_ref[:, sl] = (u_ref[:, sl].astype(F32) * s).astype(o_ref.dtype)


def _spatial_gating(z_gm, ln_g, ln_b, w_s, b_s):
    n_tok = z_gm.shape[0]
    return pl.pallas_call(
        _gating_kernel,
        grid=(n_tok // GM_BLOCK,),
        in_specs=[
            pl.BlockSpec((GM_BLOCK, GM_WIDTH), lambda i: (i, 0)),
            pl.BlockSpec((GM_BLOCK, GM_WIDTH), lambda i: (i, 1)),
            pl.BlockSpec((1, GM_WIDTH), lambda i: (0, 0)),
            pl.BlockSpec((1, GM_WIDTH), lambda i: (0, 0)),
            pl.BlockSpec((GM_GROUPS, GM_BLOCK, GM_BLOCK), lambda i: (0, 0, 0)),
            pl.BlockSpec((GM_BLOCK, GM_GROUPS), lambda i: (0, 0)),
        ],
        out_specs=pl.BlockSpec((GM_BLOCK, GM_WIDTH), lambda i: (i, 0)),
        out_shape=jax.ShapeDtypeStruct((n_tok, GM_WIDTH), BF16),
        compiler_params=_params("arbitrary"),
        name="spatial_gating",
    )(z_gm, z_gm, ln_g.reshape(1, GM_WIDTH), ln_b.reshape(1, GM_WIDTH), w_s, b_s.T)


def _s5_tables(lam_re, lam_im, log_dt, b_re, b_im, c_re, c_im, d_skip, seq):
    hi = lax.Precision.HIGHEST
    n, p, t = SSM_N, SSM_P, SSM_T
    no, npr = SSM_NOCT, SSM_PAIRS
    dt = jnp.exp(log_dt)[:, None]
    mag = jnp.exp(lam_re * dt)
    ab_re = mag * jnp.cos(lam_im * dt)
    ab_im = mag * jnp.sin(lam_im * dt)
    den = lam_re * lam_re + lam_im * lam_im
    num_re = ab_re - 1.0
    coef_re = (num_re * lam_re + ab_im * lam_im) / den
    coef_im = (ab_im * lam_re - num_re * lam_im) / den
    bb_re = coef_re[..., None] * b_re - coef_im[..., None] * b_im
    bb_im = coef_re[..., None] * b_im + coef_im[..., None] * b_re

    def power(k):
        k = jnp.asarray(k, F32)[..., None, None]
        mk = jnp.exp(k * (lam_re * dt))
        return mk * jnp.cos(k * (lam_im * dt)), mk * jnp.sin(k * (lam_im * dt))

    pw_re, pw_im = power(np.arange(t + 1))
    cp_re = c_re[None] * pw_re[:, :, None, :] - c_im[None] * pw_im[:, :, None, :]
    cp_im = c_re[None] * pw_im[:, :, None, :] + c_im[None] * pw_re[:, :, None, :]
    kern = (jnp.einsum('tgpn,gnq->gtpq', cp_re[:t], bb_re, precision=hi)
            - jnp.einsum('tgpn,gnq->gtpq', cp_im[:t], bb_im, precision=hi))
    eye_o = jnp.eye(SSM_OCT, dtype=F32)
    eye_2 = jnp.eye(2, dtype=F32)
    kbd = jnp.einsum('oatpq,ab->otaqbp', kern.reshape(no, SSM_OCT, t, p, p), eye_o)
    kbd = kbd.reshape(no, t, LANES, LANES)
    rv_re, rv_im = pw_re[t - 1::-1], pw_im[t - 1::-1]
    st_re = rv_re[..., None] * bb_re[None] - rv_im[..., None] * bb_im[None]
    st_im = rv_re[..., None] * bb_im[None] + rv_im[..., None] * bb_re[None]
    st = jnp.stack([st_re, st_im], 0).reshape(2, t, no, npr, 2, n, p)
    w_st = jnp.einsum('rjoxanq,ab->ojxraqbn', st, eye_2).reshape(no, t, npr, 2, 2 * p, 2 * n)
    wc = jnp.stack([cp_re[1:], -cp_im[1:]], 0).reshape(2, t, no, npr, 2, p, n)
    w_ct = jnp.einsum('rioxapn,ab->oixrapbn', wc, eye_2).reshape(no, t, npr, 2, 2 * p, 2 * n)
    n_lvl = int(math.log2(seq // t))
    lv_re, lv_im = power(t * 2 ** np.arange(n_lvl))
    lr = lv_re.reshape(n_lvl, no, SSM_SW)
    li = lv_im.reshape(n_lvl, no, SSM_SW)
    lvl = jnp.stack([jnp.concatenate([lr, lr], -1), jnp.concatenate([-li, li], -1)], axis=1)
    lvl = lvl.transpose(2, 0, 1, 3).reshape(no, 2 * n_lvl, 2 * SSM_SW)
    d_vec = jnp.tile(d_skip.reshape(no, 1, LANES), (1, 1, t))
    return kbd.astype(BF16), w_st.astype(BF16), w_ct.astype(BF16), lvl, d_vec


def _s5_kernel(u_ref, kbd_ref, wst_ref, wct_ref, lvl_ref, d_ref, o_ref, toe_ref, st_ref, ct_ref,
               *, n_chunks, n_lvl):
    t = SSM_T

    @pl.when(pl.program_id(0) == 0)
    def _():
        toe_ref[...] = jnp.zeros_like(toe_ref)
        st_ref[...] = jnp.zeros_like(st_ref)
        ct_ref[...] = jnp.zeros_like(ct_ref)

    for i in range(t):
        for j in range(i + 1):
            toe_ref[j * LANES:(j + 1) * LANES, i * LANES:(i + 1) * LANES] = kbd_ref[i - j]
    for j in range(t):
        for x in range(SSM_PAIRS):
            for r in range(2):
                rows = slice(j * LANES + x * 2 * SSM_P, j * LANES + (x + 1) * 2 * SSM_P)
                lanes = slice(r * SSM_SW + x * LANES, r * SSM_SW + (x + 1) * LANES)
                st_ref[rows, lanes] = wst_ref[j, x, r]
                ct_ref[rows, lanes] = wct_ref[j, x, r]

    u = u_ref[...]
    ub = u.astype(BF16)
    h = _dot(ub, st_ref[...])
    cidx = lax.broadcasted_iota(jnp.int32, (u.shape[0], 1), 0) % n_chunks
    for lv in range(n_lvl):
        d = 1 << lv
        a_rr = lvl_ref[2 * lv:2 * lv + 1, :]
        a_is = lvl_ref[2 * lv + 1:2 * lv + 2, :]
        sh = jnp.where(cidx >= d, pltpu.roll(h, d, 0), 0.0)
        h = h + sh * a_rr + pltpu.roll(sh, SSM_SW, 1) * a_is
    h_prev = jnp.where(cidx >= 1, pltpu.roll(h, 1, 0), 0.0)
    y = _dot(ub, toe_ref[...]) + _dot_nt(h_prev.astype(BF16), ct_ref[...]) + d_ref[...] * u
    o_ref[...] = jax.nn.gelu(y).astype(o_ref.dtype)


def _s5_mixer(z_ssm, batch, seq, tables):
    kbd, w_st, w_ct, lvl, d_vec = tables
    t, no = SSM_T, SSM_NOCT
    n_chunks = seq // t
    rows = batch * n_chunks
    width = t * LANES
    n_lvl = lvl.shape[1] // 2
    u = z_ssm.reshape(rows, t, no, LANES).transpose(2, 0, 1, 3).reshape(no, rows, width)
    y = pl.pallas_call(
        functools.partial(_s5_kernel, n_chunks=n_chunks, n_lvl=n_lvl),
        grid=(no,),
        in_specs=[
            pl.BlockSpec((None, rows, width), lambda i: (i, 0, 0)),
            pl.BlockSpec((None, t, LANES, LANES), lambda i: (i, 0, 0, 0)),
            pl.BlockSpec((None, t, SSM_PAIRS, 2, 2 * SSM_P, LANES), lambda i: (i, 0, 0, 0, 0, 0)),
            pl.BlockSpec((None, t, SSM_PAIRS, 2, 2 * SSM_P, LANES), lambda i: (i, 0, 0, 0, 0, 0)),
            pl.BlockSpec((None, 2 * n_lvl, 2 * SSM_SW), lambda i: (i, 0, 0)),
            pl.BlockSpec((None, 1, width), lambda i: (i, 0, 0)),
        ],
        out_specs=pl.BlockSpec((None, rows, width), lambda i: (i, 0, 0)),
        out_shape=jax.ShapeDtypeStruct((no, rows, width), BF16),
        scratch_shapes=[
            pltpu.VMEM((width, width), BF16),
            pltpu.VMEM((width, 2 * SSM_SW), BF16),
            pltpu.VMEM((width, 2 * SSM_SW), BF16),
        ],
        compiler_params=_params("arbitrary"),
        name="s5_mixer",
    )(u, kbd, w_st, w_ct, lvl, d_vec)
    return y.reshape(no, rows, t, LANES).transpose(1, 2, 0, 3).reshape(batch * seq, SSM_WIDTH)


def _rope_kernel(pos_ref, freq_ref, kr_ref, tab_ref, kro_ref):
    ang = pos_ref[...].astype(F32) * freq_ref[...]
    lane = lax.broadcasted_iota(jnp.int32, ang.shape, 1)
    sin = jnp.sin(ang)
    tab = jnp.where(lane < MLA_ROPE, jnp.cos(ang), jnp.where(lane < MLA_ROPE + MLA_ROPE // 2, -sin, sin))
    tab_ref[...] = tab
    kr = kr_ref[...]
    half = MLA_ROPE // 2
    sw = jnp.concatenate([kr[:, half:], kr[:, :half]], axis=1)
    kro_ref[...] = (kr * tab[:, :MLA_ROPE] + sw * tab[:, MLA_ROPE:]).astype(kro_ref.dtype)


def _rope_tables(positions, k_rope):
    n_tok = k_rope.shape[0]
    tm = 512
    inv_freq = ROPE_THETA ** (-np.arange(0, MLA_ROPE, 2, dtype=np.float64) / MLA_ROPE)
    freq = jnp.asarray(np.tile(inv_freq, 4)[None, :], F32)
    return pl.pallas_call(
        _rope_kernel,
        grid=(n_tok // tm,),
        in_specs=[
            pl.BlockSpec((tm, 1), lambda i: (i, 0)),
            pl.BlockSpec((1, 2 * MLA_ROPE), lambda i: (0, 0)),
            pl.BlockSpec((tm, MLA_ROPE), lambda i: (i, 0)),
        ],
        out_specs=[
            pl.BlockSpec((tm, 2 * MLA_ROPE), lambda i: (i, 0)),
            pl.BlockSpec((tm, MLA_ROPE), lambda i: (i, 0)),
        ],
        out_shape=[
            jax.ShapeDtypeStruct((n_tok, 2 * MLA_ROPE), F32),
            jax.ShapeDtypeStruct((n_tok, MLA_ROPE), BF16),
        ],
        compiler_params=_params("arbitrary"),
        name="rope_tables",
    )(positions.reshape(n_tok, 1), freq, k_rope)


ATT_TK = 256
ATT_TQ = 2 * ATT_TK
ATT_DK = 2 * MLA_NOPE


def _attn_kernel(q_ref, tab_ref, k0_ref, v0_ref, k1_ref, v1_ref, kr_ref, o_ref, kc_ref):
    qi = pl.program_id(2)
    hd = MLA_NOPE + MLA_ROPE
    scale = hd ** -0.5
    half = MLA_ROPE // 2
    tk = ATT_TK

    @pl.when(qi == 0)
    def _():
        for hh, k_ref in enumerate((k0_ref, k1_ref)):
            kc_ref[hh, :, 0:MLA_NOPE] = k_ref[...]
            kc_ref[hh, :, MLA_NOPE:hd] = kr_ref[...]
            kc_ref[hh, :, hd:] = jnp.zeros((kr_ref.shape[0], ATT_DK - hd), BF16)

    q = q_ref[...]
    cosf = tab_ref[:, :MLA_ROPE]
    sinf = tab_ref[:, MLA_ROPE:]
    qc = []
    for hh in range(2):
        qn = q[:, hh * hd:hh * hd + MLA_NOPE].astype(F32) * scale
        qr = q[:, hh * hd + MLA_NOPE:(hh + 1) * hd].astype(F32)
        sw = jnp.concatenate([qr[:, half:], qr[:, :half]], axis=1)
        qr = (qr * cosf + sw * sinf) * scale
        pad = jnp.zeros((ATT_TQ, ATT_DK - hd), F32)
        qc.append(jnp.concatenate([qn, qr, pad], axis=1).astype(BF16))
    v_refs = (v0_ref, v1_ref)
    row = lax.broadcasted_iota(jnp.int32, (tk, tk), 0)
    col = lax.broadcasted_iota(jnp.int32, (tk, tk), 1)
    visible = (col // CHUNK) <= (row // CHUNK)

    def update(state, hh, sub, j, masked):
        m, l, acc = state
        start = pl.multiple_of(j * tk, tk)
        ks = kc_ref[hh, pl.ds(start, tk), :]
        vs = v_refs[hh][pl.ds(start, tk), :]
        s = _dot_nt(qc[hh][sub * tk:(sub + 1) * tk], ks)
        if masked:
            s = jnp.where(visible, s, -1e30)
        m_new = jnp.maximum(m, jnp.max(s, axis=-1, keepdims=True))
        alpha = jnp.exp(m - m_new)
        p = jnp.exp(s - m_new)
        l = alpha * l + jnp.sum(p, axis=-1, keepdims=True)
        acc = alpha * acc + _dot(p.astype(BF16), vs)
        return m_new, l, acc

    chains = [(hh, sub) for hh in range(2) for sub in range(2)]

    def body(j, states):
        return tuple(update(st, hh, sub, j, False) for st, (hh, sub) in zip(states, chains))

    init = tuple((jnp.full((tk, 1), -1e30, F32), jnp.zeros((tk, 1), F32), jnp.zeros((tk, MLA_V), F32))
                 for _ in chains)
    states = list(lax.fori_loop(0, 2 * qi, body, init))
    for c, (hh, sub) in enumerate(chains):
        st = update(states[c], hh, sub, 2 * qi, masked=(sub == 0))
        if sub == 1:
            st = update(st, hh, sub, 2 * qi + 1, masked=True)
        _, l, acc = st
        o_ref[sub * tk:(sub + 1) * tk, hh * MLA_V:(hh + 1) * MLA_V] = (acc / l).astype(o_ref.dtype)


def _attention(q, kv, k_rope, tab, batch, seq):
    n_tok = batch * seq
    nq = seq // ATT_TQ
    kvb = lambda off: pl.BlockSpec((seq, MLA_NOPE), lambda b, hp, qi: (b, 4 * hp + off))
    return pl.pallas_call(
        _attn_kernel,
        grid=(batch, MLA_HEADS // 2, nq),
        in_specs=[
            pl.BlockSpec((ATT_TQ, 2 * (MLA_NOPE + MLA_ROPE)), lambda b, hp, qi: (b * nq + qi, hp)),
            pl.BlockSpec((ATT_TQ, 2 * MLA_ROPE), lambda b, hp, qi: (b * nq + qi, 0)),
            kvb(0), kvb(1), kvb(2), kvb(3),
            pl.BlockSpec((seq, MLA_ROPE), lambda b, hp, qi: (b, 0)),
        ],
        out_specs=pl.BlockSpec((ATT_TQ, 2 * MLA_V), lambda b, hp, qi: (b * nq + qi, hp)),
        out_shape=jax.ShapeDtypeStruct((n_tok, MLA_HEADS * MLA_V), BF16),
        scratch_shapes=[pltpu.VMEM((2, seq, ATT_DK), BF16)],
        compiler_params=_params("arbitrary", "arbitrary", "arbitrary"),
        name="mla_attention",
    )(q, tab, kv, kv, kv, kv, k_rope)


LNR_TM = 256
SLOT_LANES = 128


def _pack_halves(x):
    w = x.shape[1] // 2
    bits = lax.bitcast_convert_type(x.astype(BF16).astype(F32), U32)
    return (bits[:, :w] >> 16) | (bits[:, w:] & HI_MASK)


def _unpack_halves(p):
    return (lax.bitcast_convert_type(p << 16, F32), lax.bitcast_convert_type(p & HI_MASK, F32))


def _ln_router_kernel(h_ref, mix_ref, g_ref, b_ref, wr_ref, br_ref,
                      ho_ref, hq_ref, idx_ref, gate_ref, rank_ref, cnt_ref, carry_ref):
    @pl.when(pl.program_id(0) == 0)
    def _():
        carry_ref[...] = jnp.zeros_like(carry_ref)

    x = DEEPNORM_ALPHA * h_ref[...] + mix_ref[...]
    mu = jnp.mean(x, axis=-1, keepdims=True)
    xc = x - mu
    var = jnp.mean(xc * xc, axis=-1, keepdims=True)
    hn = xc * lax.rsqrt(var + LN_EPS) * g_ref[...] + b_ref[...]
    ho_ref[...] = hn
    hq_ref[...] = _pack_halves(hn)

    logits =jnp.dot(hn, wr_ref[...], precision=lax.Precision.HIGHEST,
                     preferred_element_type=F32) + br_ref[...]
    tm = logits.shape[0]
    lane_e = lax.broadcasted_iota(jnp.int32, (tm, N_EXPERTS), 1)
    lane_s = lax.broadcasted_iota(jnp.int32, (tm, SLOT_LANES), 1)
    work = logits
    sel = jnp.zeros((tm, N_EXPERTS), F32)
    top_v, top_i, hot = [], [], []
    for _ in range(TOP_K):
        mx = jnp.max(work, axis=-1, keepdims=True)
        ix = jnp.min(jnp.where(work == mx, lane_e, N_EXPERTS), axis=-1, keepdims=True)
        oh = lane_e == ix
        work = jnp.where(oh, -jnp.inf, work)
        sel = sel + oh.astype(F32)
        top_v.append(mx)
        top_i.append(ix)
        hot.append(oh)
    ex = [jnp.exp(v - top_v[0]) for v in top_v]
    den = ex[0] + ex[1] + ex[2] + ex[3]

    r = lax.broadcasted_iota(jnp.int32, (tm, tm), 0)
    c = lax.broadcasted_iota(jnp.int32, (tm, tm), 1)
    strict = jnp.where(c < r, 1.0, 0.0).astype(BF16)
    prefix = _dot(strict, sel.astype(BF16)) + carry_ref[0:1, :]
    carry_ref[0:1, :] = carry_ref[0:1, :] + jnp.sum(sel, axis=0, keepdims=True)
    cnt_ref[...] = jnp.broadcast_to(carry_ref[0:1, :], cnt_ref.shape)

    idx_o = jnp.zeros((tm, SLOT_LANES), jnp.int32)
    gate_o = jnp.zeros((tm, SLOT_LANES), F32)
    rank_o = jnp.zeros((tm, SLOT_LANES), jnp.int32)
    for k in range(TOP_K):
        rk = jnp.sum(jnp.where(hot[k], prefix, 0.0), axis=-1, keepdims=True)
        idx_o = jnp.where(lane_s == k, top_i[k], idx_o)
        gate_o = jnp.where(lane_s == k, ex[k] / den, gate_o)
        rank_o = jnp.where(lane_s == k, rk.astype(jnp.int32), rank_o)
    idx_ref[...] = idx_o
    gate_ref[...] = gate_o
    rank_ref[...] = rank_o


def _ln_router(h, mix, ln_g, ln_b, w_router, b_router):
    n_tok, d = h.shape
    tm = LNR_TM
    row = lambda i: (i, 0)
    fixed = lambda i: (0, 0)
    return pl.pallas_call(
        _ln_router_kernel,
        grid=(n_tok // tm,),
        in_specs=[
            pl.BlockSpec((tm, d), row), pl.BlockSpec((tm, d), row),
            pl.BlockSpec((1, d), fixed), pl.BlockSpec((1, d), fixed),
            pl.BlockSpec((d, N_EXPERTS), fixed), pl.BlockSpec((1, N_EXPERTS), fixed),
        ],
        out_specs=[
            pl.BlockSpec((tm, d), row), pl.BlockSpec((tm, d // 2), row),
            pl.BlockSpec((tm, SLOT_LANES), row), pl.BlockSpec((tm, SLOT_LANES), row),
            pl.BlockSpec((tm, SLOT_LANES), row), pl.BlockSpec((8, N_EXPERTS), fixed),
        ],
        out_shape=[
            jax.ShapeDtypeStruct((n_tok, d), F32), jax.ShapeDtypeStruct((n_tok, d // 2), U32),
            jax.ShapeDtypeStruct((n_tok, SLOT_LANES), jnp.int32),
            jax.ShapeDtypeStruct((n_tok, SLOT_LANES), F32),
            jax.ShapeDtypeStruct((n_tok, SLOT_LANES), jnp.int32),
            jax.ShapeDtypeStruct((8, N_EXPERTS), F32),
        ],
        scratch_shapes=[pltpu.VMEM((8, N_EXPERTS), F32)],
        compiler_params=_params("arbitrary"),
        name="ln_router",
    )(h, mix, ln_g.reshape(1, d), ln_b.reshape(1, d), w_router, b_router.reshape(1, N_EXPERTS))


MOE_TM = 1280
MOE_SB = 256
MOE_NSUB = MOE_TM // MOE_SB
MOE_TH = 128
MOE_TN = 512
MOE_T1 = D_EXPERT // MOE_TH
MOE_T2 = D_MODEL // 2 // MOE_TN
MOE_UNIT = 32
MOE_UPS = -(-(MOE_TM // MOE_UNIT) // (MOE_T1 + MOE_T2))
MOE_KD = 2 * MOE_TH


def _moe_kernel(e_ref, n_ref, prow_ref, nv_ref, ids_cur, ids_nxt, xq_hbm, wg_ref, wl_ref, bg_ref, bl_ref,
                wdl_ref, wdh_ref, bdl_ref, bdh_ref, y_hbm, xbuf, act_ref, wcat_ref, obuf, xsem, osem):
    i = pl.program_id(0)
    t = pl.program_id(1)
    nb = pl.num_programs(0)
    n_i = n_ref[i]
    nsub_i = (n_i + MOE_SB - 1) // MOE_SB
    slot = i % 2
    half = D_MODEL // 2

    def issue_unit(ids_ref, slot_, u):
        base = u * MOE_UNIT
        for r in range(MOE_UNIT):
            pltpu.make_async_copy(xq_hbm.at[pl.ds(ids_ref[0, base + r], 1)],
                                  xbuf.at[slot_, pl.ds(base + r, 1)], xsem.at[slot_]).start()

    def rows_arrived(slot_, s):
        return pltpu.make_async_copy(xq_hbm.at[pl.ds(0, MOE_SB)],
                                     xbuf.at[slot_, pl.ds(s * MOE_SB, MOE_SB)], xsem.at[slot_])

    @pl.when(jnp.logical_and(i == 0, t == 0))
    def _first_block_rows():
        def body(u, c):
            @pl.when(u * MOE_UNIT < nsub_i * MOE_SB)
            def _():
                issue_unit(ids_cur, 0, u)
            return c
        lax.fori_loop(0, MOE_TM // MOE_UNIT, body, 0)

    @pl.when(jnp.logical_and(t == 0, n_i > 0))
    def _wait_rows():
        for s in range(MOE_NSUB):
            @pl.when(s < nsub_i)
            def _():
                rows_arrived(slot, s).wait()

    nxt = jnp.minimum(i + 1, nb - 1)
    rows_next = jnp.where(i + 1 < nb, (n_ref[nxt] + MOE_SB - 1) // MOE_SB * MOE_SB, 0)
    for uu in range(MOE_UPS):
        u = t * MOE_UPS + uu

        @pl.when(u * MOE_UNIT < rows_next)
        def _():
            issue_unit(ids_nxt, 1 - slot, u)

    @pl.when(jnp.logical_and(t < MOE_T1, n_i > 0))
    def _gate_up():
        wcat_ref[:, 0:MOE_TH] = wg_ref[...].astype(BF16)
        wcat_ref[:, MOE_TH:] = wl_ref[...].astype(BF16)
        for s in range(MOE_NSUB):
            @pl.when(s < nsub_i)
            def _():
                rows = slice(s * MOE_SB, (s + 1) * MOE_SB)
                x_lo, x_hi = _unpack_halves(xbuf[slot, rows, :])
                hcat = (_dot(x_lo.astype(BF16), wcat_ref[0:half, :])
                        + _dot(x_hi.astype(BF16), wcat_ref[half:, :]))
                hg = jnp.minimum(hcat[:, :MOE_TH] + bg_ref[...], SWIGLU_LIMIT)
                hl = jnp.clip(hcat[:, MOE_TH:] + bl_ref[...], -SWIGLU_LIMIT, SWIGLU_LIMIT)
                a = (hg * jax.nn.sigmoid(SWIGLU_ALPHA * hg) * (hl + 1.0)).astype(BF16)

                @pl.when(t % 2 == 0)
                def _():
                    act_ref[t // 2, rows, 0:MOE_TH] = a

                @pl.when(t % 2 == 1)
                def _():
                    act_ref[t // 2, rows, MOE_TH:] = a

    @pl.when(jnp.logical_and(t >= MOE_T1, n_i > 0))
    def _down():
        t2 = t - MOE_T1
        oslot = t2 % 2
        wdl = wdl_ref[...].astype(BF16)
        wdh = wdh_ref[...].astype(BF16)
        col0 = pl.multiple_of(t2 * MOE_TN, MOE_TN)

        def result_copy(slot_, s):
            return pltpu.make_async_copy(
                obuf.at[slot_, pl.ds(s * MOE_SB, MOE_SB)],
                y_hbm.at[pl.ds(pl.multiple_of(prow_ref[i] + s * MOE_SB, MOE_SB), MOE_SB),
                         pl.ds(col0, MOE_TN)], osem.at[slot_])

        for s in range(MOE_NSUB):
            @pl.when(s < nsub_i)
            def _():
                rows = slice(s * MOE_SB, (s + 1) * MOE_SB)
                y_lo = bdl_ref[...]
                y_hi = bdh_ref[...]
                for k in range(D_EXPERT // MOE_KD):
                    a = act_ref[k, rows, :]
                    y_lo = y_lo + _dot(a, wdl[k * MOE_KD:(k + 1) * MOE_KD, :])
                    y_hi = y_hi + _dot(a, wdh[k * MOE_KD:(k + 1) * MOE_KD, :])
                obuf[oslot, rows, :] = _pack_halves(jnp.concatenate([y_lo, y_hi], axis=1))
                result_copy(oslot, s).start()

        for s in range(MOE_NSUB):
            @pl.when(jnp.logical_and(s < nsub_i, t2 >= 1))
            def _():
                result_copy(1 - oslot, s).wait()

            @pl.when(jnp.logical_and(s < nsub_i, t2 == MOE_T2 - 1))
            def _():
                result_copy(oslot, s).wait()

    @pl.when(jnp.logical_and(i == nv_ref[0] - 1, t == MOE_T1 + MOE_T2 - 1))
    def _zero_tail():
        zslot = 1 - slot
        xbuf[zslot, 0:MOE_SB, :] = jnp.zeros((MOE_SB, half), U32)

        def tail_copy(j):
            return pltpu.make_async_copy(
                xbuf.at[zslot, pl.ds(0, MOE_SB)],
                y_hbm.at[pl.ds(pl.multiple_of(j * MOE_SB, MOE_SB), MOE_SB)], xsem.at[zslot])

        def start(j, c):
            tail_copy(j).start()
            return c

        def wait(j, c):
            tail_copy(j).wait()
            return c

        first = nv_ref[1] // MOE_SB
        total = y_hbm.shape[0] // MOE_SB
        lax.fori_loop(first, total, start, 0)
        lax.fori_loop(first, total, wait, 0)


def _moe_experts(xq, blk_exp, blk_n, blk_prow, n_valid_blocks, ids, w_gu, b_gu, w_down, b_down, layer,
                 n_rows_out):
    nb = ids.shape[0]
    t_last = MOE_T1 + MOE_T2 - 1
    half = D_MODEL // 2
    hi_blk = half // MOE_TN

    def src(i, nv):
        return jnp.maximum(jnp.minimum(i, nv[0] - 1), 0)

    def tt(i, t, nv):
        return jnp.where(i < nv[0], t, t_last)

    def t1(i, t, nv):
        return jnp.minimum(tt(i, t, nv), MOE_T1 - 1)

    def t2(i, t, nv):
        return jnp.maximum(tt(i, t, nv) - MOE_T1, 0)

    ids_cur_map = lambda i, t, e, n, p, nv: (src(i, nv), 0, 0)
    ids_nxt_map = lambda i, t, e, n, p, nv: (jnp.minimum(src(i, nv) + 1, nb - 1), 0, 0)
    wg_map = lambda i, t, e, n, p, nv: (layer, e[i], 0, t1(i, t, nv))
    wl_map = lambda i, t, e, n, p, nv: (layer, e[i], 0, MOE_T1 + t1(i, t, nv))
    wdl_map = lambda i, t, e, n, p, nv: (layer, e[i], 0, t2(i, t, nv))
    wdh_map = lambda i, t, e, n, p, nv: (layer, e[i], 0, hi_blk + t2(i, t, nv))
    grid_spec = pltpu.PrefetchScalarGridSpec(
        num_scalar_prefetch=4,
        grid=(nb, MOE_T1 + MOE_T2),
        in_specs=[
            pl.BlockSpec((None, 1, MOE_TM), ids_cur_map, memory_space=pltpu.SMEM),
            pl.BlockSpec((None, 1, MOE_TM), ids_nxt_map, memory_space=pltpu.SMEM),
            pl.BlockSpec(memory_space=pl.ANY),
            pl.BlockSpec((None, None, D_MODEL, MOE_TH), wg_map),
            pl.BlockSpec((None, None, D_MODEL, MOE_TH), wl_map),
            pl.BlockSpec((None, None, 1, MOE_TH), wg_map),
            pl.BlockSpec((None, None, 1, MOE_TH), wl_map),
            pl.BlockSpec((None, None, D_EXPERT, MOE_TN), wdl_map),
            pl.BlockSpec((None, None, D_EXPERT, MOE_TN), wdh_map),
            pl.BlockSpec((None, None, 1, MOE_TN), wdl_map),
            pl.BlockSpec((None, None, 1, MOE_TN), wdh_map),
        ],
        out_specs=pl.BlockSpec(memory_space=pl.ANY),
        scratch_shapes=[
            pltpu.VMEM((2, MOE_TM, half), U32),
            pltpu.VMEM((D_EXPERT // MOE_KD, MOE_TM, MOE_KD), BF16),
            pltpu.VMEM((D_MODEL, 2 * MOE_TH), BF16),
            pltpu.VMEM((2, MOE_TM, MOE_TN), U32),
            pltpu.SemaphoreType.DMA((2,)),
            pltpu.SemaphoreType.DMA((2,)),
        ],
    )
    bgu = b_gu.reshape(DEPTH, N_EXPERTS, 1, 2 * D_EXPERT)
    ids3 = ids.reshape(nb, 1, MOE_TM)
    bdn = b_down.reshape(DEPTH, N_EXPERTS, 1, D_MODEL)
    return pl.pallas_call(
        _moe_kernel,
        grid_spec=grid_spec,
        out_shape=jax.ShapeDtypeStruct((n_rows_out, half), U32),
        compiler_params=_params("arbitrary", "arbitrary"),
        name="moe_experts",
    )(blk_exp, blk_n, blk_prow, n_valid_blocks, ids3, ids3, xq, w_gu, w_gu, bgu, bgu,
      w_down, w_down, bdn, bdn)


LNC_TM = 128


def _ln_combine_kernel(ids_cur, ids_nxt, h_ref, gate_ref, g_ref, b_ref, y_hbm, ho_ref, hb_ref, ybuf, sem):
    i = pl.program_id(0)
    tm = h_ref.shape[0]
    w = h_ref.shape[1] // 2
    rows = TOP_K * tm
    slot = i % 2

    def issue(ids_ref, slot_):
        def body(u, c):
            for r in range(MOE_UNIT):
                idx = u * MOE_UNIT + r
                pltpu.make_async_copy(y_hbm.at[pl.ds(ids_ref[0, idx], 1)],
                                      ybuf.at[slot_, pl.ds(idx, 1)], sem.at[slot_]).start()
            return c
        lax.fori_loop(0, rows // MOE_UNIT, body, 0)

    @pl.when(i == 0)
    def _():
        issue(ids_cur, 0)

    @pl.when(i + 1 < pl.num_programs(0))
    def _():
        issue(ids_nxt, 1 - slot)

    pltpu.make_async_copy(y_hbm.at[pl.ds(0, rows)], ybuf.at[slot], sem.at[slot]).wait()

    x_lo = DEEPNORM_ALPHA * h_ref[:, :w]
    x_hi = DEEPNORM_ALPHA * h_ref[:, w:]
    for k in range(TOP_K):
        lo, hi = _unpack_halves(ybuf[slot, k * tm:(k + 1) * tm, :])
        gk = gate_ref[:, k:k + 1]
        x_lo = x_lo + gk * lo
        x_hi = x_hi + gk * hi
    x = jnp.concatenate([x_lo, x_hi], axis=1)
    mu = jnp.mean(x, axis=-1, keepdims=True)
    xc = x - mu
    var = jnp.mean(xc * xc, axis=-1, keepdims=True)
    hn = xc * lax.rsqrt(var + LN_EPS) * g_ref[...] + b_ref[...]
    ho_ref[...] = hn
    hb_ref[...] = hn.astype(BF16)


def _ln_combine(h, yq, dest_tiles, gates, ln_g, ln_b):
    n_tok, d = h.shape
    tm = LNC_TM
    nblk = n_tok // tm
    row = lambda i: (i, 0)
    fixed = lambda i: (0, 0)
    dest3 = dest_tiles.reshape(nblk, 1, TOP_K * tm)
    return pl.pallas_call(
        _ln_combine_kernel,
        grid=(nblk,),
        in_specs=[
            pl.BlockSpec((None, 1, TOP_K * tm), lambda i: (i, 0, 0), memory_space=pltpu.SMEM),
            pl.BlockSpec((None, 1, TOP_K * tm), lambda i: (jnp.minimum(i + 1, nblk - 1), 0, 0),
                         memory_space=pltpu.SMEM),
            pl.BlockSpec((tm, d), row),
            pl.BlockSpec((tm, SLOT_LANES), row),
            pl.BlockSpec((1, d), fixed), pl.BlockSpec((1, d), fixed),
            pl.BlockSpec(memory_space=pl.ANY),
        ],
        out_specs=[pl.BlockSpec((tm, d), row), pl.BlockSpec((tm, d), row)],
        out_shape=[jax.ShapeDtypeStruct((n_tok, d), F32), jax.ShapeDtypeStruct((n_tok, d), BF16)],
        scratch_shapes=[pltpu.VMEM((2, TOP_K * tm, d // 2), U32), pltpu.SemaphoreType.DMA((2,))],
        compiler_params=_params("arbitrary"),
        name="ln_combine",
    )(dest3, dest3, h, gates, ln_g.reshape(1, d), ln_b.reshape(1, d), yq)


def _moe_layer(h, hq, top_idx, gates, rank, counts, w_gu, b_gu, w_down, b_down, layer, ln_g, ln_b):
    n_tok = h.shape[0]
    n_assign = n_tok * TOP_K
    nb = N_EXPERTS + n_assign // MOE_TM
    n_rows_out = n_assign + N_EXPERTS * MOE_SB
    i32 = jnp.int32
    cnt = counts[0].astype(i32)
    start = jnp.cumsum(cnt) - cnt
    pcnt = (cnt + MOE_SB - 1) // MOE_SB * MOE_SB
    pstart = jnp.cumsum(pcnt) - pcnt
    nblk = (cnt + MOE_TM - 1) // MOE_TM
    bend = jnp.cumsum(nblk)
    bfirst = bend - nblk
    n_valid = bend[-1]
    e_flat = top_idx[:, :TOP_K].reshape(-1)
    rank_flat = rank[:, :TOP_K].reshape(-1)
    tok = jnp.arange(n_assign, dtype=i32) // TOP_K
    row_tok = jnp.zeros((n_assign + MOE_TM,), i32).at[start[e_flat] + rank_flat].set(tok)
    dest = pstart[e_flat] + rank_flat
    blk = jnp.arange(nb, dtype=i32)
    src = jnp.maximum(jnp.minimum(blk, n_valid - 1), 0)
    be = jnp.minimum(jnp.sum((bend[None, :] <= src[:, None]).astype(i32), axis=1), N_EXPERTS - 1)
    b_in = src - bfirst[be]
    blk_n = jnp.where(blk < n_valid, jnp.clip(cnt[be] - b_in * MOE_TM, 0, MOE_TM), 0).astype(i32)
    blk_prow = (pstart[be] + b_in * MOE_TM).astype(i32)
    row0 = start[be] + b_in * MOE_TM
    ids = jnp.take(row_tok, row0[:, None] + jnp.arange(MOE_TM, dtype=i32)[None, :], mode="clip")
    used_rows = pstart[-1] + pcnt[-1]
    yq = _moe_experts(hq, be.astype(i32), blk_n, blk_prow, jnp.stack([n_valid, used_rows]).astype(i32), ids,
                      w_gu, b_gu, w_down, b_down, layer, n_rows_out)
    tm = LNC_TM
    dest_tiles = dest.reshape(n_tok // tm, tm, TOP_K).transpose(0, 2, 1).reshape(n_tok // tm, TOP_K * tm)
    return _ln_combine(h, yq, dest_tiles, gates, ln_g, ln_b)


def kernel(x, positions, ln_g, ln_b, hy_w_in, hy_w_out, gm_ln_g, gm_ln_b, gm_w_s, gm_b_s, ssm_lam_re, ssm_lam_im, ssm_log_dt, ssm_b_re, ssm_b_im, ssm_c_re, ssm_c_im, ssm_d, ssm_w_glu, ssm_b_glu, mla_w_in, mla_q_norm_g, mla_kv_norm_g, mla_w_uq, mla_w_ukv, mla_w_o, moe_w_router, moe_b_router, moe_w_gu, moe_b_gu, moe_w_down, moe_b_down):
    batch, seq, d = x.shape
    n_tok = batch * seq
    h = x.reshape(n_tok, d)
    hb = h.astype(BF16)
    for layer in range(DEPTH):
        i = layer // 2
        if layer % 2 == 0:
            z_gm = _mm([hb], hy_w_in[i], col0=0, n_cols=2 * GM_WIDTH, tm=1024, tn=256,
                       out_dtype=BF16, epilogue="gelu", name="hy_in_gm")
            z_ssm = _mm([hb], hy_w_in[i], col0=2 * GM_WIDTH, n_cols=SSM_WIDTH, tm=1024, tn=256,
                        out_dtype=F32, name="hy_in_ssm")
            y_gm = _spatial_gating(z_gm, gm_ln_g[i], gm_ln_b[i], gm_w_s[i], gm_b_s[i])
            tables = _s5_tables(ssm_lam_re[i], ssm_lam_im[i], ssm_log_dt[i], ssm_b_re[i], ssm_b_im[i],
                                ssm_c_re[i], ssm_c_im[i], ssm_d[i], seq)
            y_act = _s5_mixer(z_ssm, batch, seq, tables)
            y_ssm = _mm([y_act], ssm_w_glu[i], tm=1024, tn=256, out_dtype=BF16, epilogue="glu",
                        bias=ssm_b_glu[i], mul=y_act, name="s5_glu")
            mix = _mm([y_gm, y_ssm], hy_w_out[i], tm=1024, tn=256, out_dtype=F32, name="hy_out")
        else:
            w_in = mla_w_in[i]
            n_main = MLA_Q_RANK + MLA_KV_RANK
            c_main = _mm([hb], w_in, col0=0, n_cols=n_main, tm=1024, tn=256, out_dtype=F32,
                         name="mla_in")
            k_rope = _mm([hb], w_in[:, n_main:], tm=1024, tn=MLA_ROPE, out_dtype=F32, name="mla_in_rope")
            tab, k_rope = _rope_tables(positions, k_rope)
            q = _mm([(c_main, MLA_Q_RANK, 0)], mla_w_uq[i], tm=1024, tn=384, out_dtype=BF16,
                    prologue="rms", gain=mla_q_norm_g[i], name="mla_uq")
            kv = _mm([(c_main, MLA_KV_RANK, MLA_Q_RANK // MLA_KV_RANK)], mla_w_ukv[i], tm=1024, tn=512,
                     out_dtype=BF16, prologue="rms", gain=mla_kv_norm_g[i], name="mla_ukv")
            o = _attention(q, kv, k_rope, tab, batch, seq)
            mix = _mm([o], mla_w_o[i], tm=1024, tn=256, out_dtype=F32, name="mla_out")
        h, hq, top_idx, gates, rank, counts = _ln_router(
            h, mix, ln_g[layer, 0], ln_b[layer, 0], moe_w_router[layer], moe_b_router[layer])
        h, hb = _moe_layer(h, hq, top_idx, gates, rank, counts, moe_w_gu, moe_b_gu,
                           moe_w_down, moe_b_down, layer, ln_g[layer, 1], ln_b[layer, 1])
    return h.reshape(batch, seq, d)
```

```python
import functools
import math

import numpy as np
import jax
import jax.numpy as jnp
from jax import lax
from jax.experimental import pallas as pl
from jax.experimental.pallas import tpu as pltpu

F32 = jnp.float32
BF16 = jnp.bfloat16
U32 = jnp.uint32
HI_MASK = np.uint32(0xFFFF0000)

D_MODEL = 4096
DEPTH = 2
CHUNK = 64
DEEPNORM_ALPHA = (2 * DEPTH) ** 0.25
LN_EPS = 1e-5
RMS_EPS = 1e-6
LANES = 128

GM_WIDTH = 2048
GM_GROUPS = 8
GM_GROUP_DIM = 256
GM_BLOCK = 128
SSM_WIDTH = 2048
SSM_P = 16
SSM_GROUPS = 128
SSM_N = 64
SSM_T = 16
SSM_OCT = LANES // SSM_P
SSM_NOCT = SSM_GROUPS // SSM_OCT
SSM_PAIRS = SSM_OCT // 2
SSM_SW = SSM_OCT * SSM_N

MLA_HEADS = 32
MLA_Q_RANK = 1024
MLA_KV_RANK = 512
MLA_NOPE = 128
MLA_ROPE = 64
MLA_V = 128
ROPE_THETA = 10000.0

N_EXPERTS = 32
TOP_K = 4
D_EXPERT = 1024
SWIGLU_LIMIT = 7.0
SWIGLU_ALPHA = 1.702

VMEM_LIMIT_BYTES = 56 * 1024 * 1024


def _params(*sem):
    return pltpu.CompilerParams(dimension_semantics=sem, vmem_limit_bytes=VMEM_LIMIT_BYTES)


def _dot(a, b):
    return jnp.dot(a, b, preferred_element_type=F32)


def _dot_nt(a, b):
    return lax.dot_general(a, b, (((1,), (1,)), ((), ())), preferred_element_type=F32)


def _mm_kernel(*refs, prologue, epilogue, n_x):
    it = iter(refs)
    x_refs = [next(it) for _ in range(n_x)]
    w_ref = next(it)
    g_ref = next(it) if prologue == "rms" else None
    b_ref = next(it) if epilogue == "glu" else None
    y_ref = next(it) if epilogue == "glu" else None
    o_ref = next(it)
    xs_ref = next(it) if prologue == "rms" else None

    if prologue == "rms":
        @pl.when(pl.program_id(1) == 0)
        def _():
            xf = x_refs[0][...].astype(F32)
            ms = jnp.mean(xf * xf, axis=-1, keepdims=True)
            xs_ref[...] = (xf * lax.rsqrt(ms + RMS_EPS) * g_ref[...]).astype(BF16)
        xs = [xs_ref[...]]
    else:
        xs = [r[...].astype(BF16) for r in x_refs]

    acc = None
    k0 = 0
    for x in xs:
        kk = x.shape[1]
        part = _dot(x, w_ref[k0:k0 + kk, :].astype(BF16))
        acc = part if acc is None else acc + part
        k0 += kk
    if epilogue == "gelu":
        acc = jax.nn.gelu(acc)
    elif epilogue == "glu":
        acc = y_ref[...].astype(F32) * jax.nn.sigmoid(acc + b_ref[...])
    o_ref[...] = acc.astype(o_ref.dtype)


def _mm(xs, w, *, col0=0, n_cols=None, tm, tn, out_dtype, prologue=None, gain=None,
        epilogue=None, bias=None, mul=None, name):
    xs = [(x, x.shape[1], 0) if not isinstance(x, tuple) else x for x in xs]
    m = xs[0][0].shape[0]
    k_total = w.shape[0]
    n_cols = w.shape[1] - col0 if n_cols is None else n_cols
    tn = min(tn, n_cols)
    assert m % tm == 0 and n_cols % tn == 0 and col0 % tn == 0
    cb0 = col0 // tn
    assert sum(kw for _, kw, _ in xs) == k_total
    in_specs = [pl.BlockSpec((tm, kw), functools.partial(lambda i, j, cb: (i, cb), cb=cb))
                for _, kw, cb in xs]
    in_specs.append(pl.BlockSpec((k_total, tn), lambda i, j: (0, cb0 + j)))
    args = [x for x, _, _ in xs] + [w]
    scratch = []
    if prologue == "rms":
        in_specs.append(pl.BlockSpec((1, k_total), lambda i, j: (0, 0)))
        args.append(gain.reshape(1, k_total))
        scratch.append(pltpu.VMEM((tm, k_total), BF16))
    if epilogue == "glu":
        in_specs.append(pl.BlockSpec((1, tn), lambda i, j: (0, j)))
        in_specs.append(pl.BlockSpec((tm, tn), lambda i, j: (i, j)))
        args += [bias.reshape(1, n_cols), mul]
    return pl.pallas_call(
        functools.partial(_mm_kernel, prologue=prologue, epilogue=epilogue, n_x=len(xs)),
        grid=(m // tm, n_cols // tn),
        in_specs=in_specs,
        out_specs=pl.BlockSpec((tm, tn), lambda i, j: (i, j)),
        out_shape=jax.ShapeDtypeStruct((m, n_cols), out_dtype),
        scratch_shapes=scratch,
        compiler_params=_params("arbitrary", "arbitrary"),
        name=name,
    )(*args)


def _gating_kernel(u_ref, v_ref, g_ref, b_ref, ws_ref, bst_ref, o_ref):
    row = lax.broadcasted_iota(jnp.int32, (GM_BLOCK, GM_BLOCK), 0)
    col = lax.broadcasted_iota(jnp.int32, (GM_BLOCK, GM_BLOCK), 1)
    visible = (col // CHUNK) <= (row // CHUNK)
    for g in range(GM_GROUPS):
        sl = slice(g * GM_GROUP_DIM, (g + 1) * GM_GROUP_DIM)
        v = v_ref[:, sl].astype(F32)
        mu = jnp.mean(v, axis=-1, keepdims=True)
        vc = v - mu
        var = jnp.mean(vc * vc, axis=-1, keepdims=True)
        vn = vc * lax.rsqrt(var + LN_EPS) * g_ref[:, sl] + b_ref[:, sl]
        w = jnp.where(visible, ws_ref[g], 0.0).astype(BF16)
        s = _dot(w, vn.astype(BF16)) + bst_ref[:, g:g + 1]
        o_ref[:, sl] = (u_ref[:, sl].astype(F32) * s).astype(o_ref.dtype)


def _spatial_gating(z_gm, ln_g, ln_b, w_s, b_s):
    n_tok = z_gm.shape[0]
    return pl.pallas_call(
        _gating_kernel,
        grid=(n_tok // GM_BLOCK,),
        in_specs=[
            pl.BlockSpec((GM_BLOCK, GM_WIDTH), lambda i: (i, 0)),
            pl.BlockSpec((GM_BLOCK, GM_WIDTH), lambda i: (i, 1)),
            pl.BlockSpec((1, GM_WIDTH), lambda i: (0, 0)),
            pl.BlockSpec((1, GM_WIDTH), lambda i: (0, 0)),
            pl.BlockSpec((GM_GROUPS, GM_BLOCK, GM_BLOCK), lambda i: (0, 0, 0)),
            pl.BlockSpec((GM_BLOCK, GM_GROUPS), lambda i: (0, 0)),
        ],
        out_specs=pl.BlockSpec((GM_BLOCK, GM_WIDTH), lambda i: (i, 0)),
        out_shape=jax.ShapeDtypeStruct((n_tok, GM_WIDTH), BF16),
        compiler_params=_params("arbitrary"),
        name="spatial_gating",
    )(z_gm, z_gm, ln_g.reshape(1, GM_WIDTH), ln_b.reshape(1, GM_WIDTH), w_s, b_s.T)


def _s5_tables(lam_re, lam_im, log_dt, b_re, b_im, c_re, c_im, d_skip, seq):
    hi = lax.Precision.HIGHEST
    n, p, t = SSM_N, SSM_P, SSM_T
    no, npr = SSM_NOCT, SSM_PAIRS
    dt = jnp.exp(log_dt)[:, None]
    mag = jnp.exp(lam_re * dt)
    ab_re = mag * jnp.cos(lam_im * dt)
    ab_im = mag * jnp.sin(lam_im * dt)
    den = lam_re * lam_re + lam_im * lam_im
    num_re = ab_re - 1.0
    coef_re = (num_re * lam_re + ab_im * lam_im) / den
    coef_im = (ab_im * lam_re - num_re * lam_im) / den
    bb_re = coef_re[..., None] * b_re - coef_im[..., None] * b_im
    bb_im = coef_re[..., None] * b_im + coef_im[..., None] * b_re

    def power(k):
        k = jnp.asarray(k, F32)[..., None, None]
        mk = jnp.exp(k * (lam_re * dt))
        return mk * jnp.cos(k * (lam_im * dt)), mk * jnp.sin(k * (lam_im * dt))

    pw_re, pw_im = power(np.arange(t + 1))
    cp_re = c_re[None] * pw_re[:, :, None, :] - c_im[None] * pw_im[:, :, None, :]
    cp_im = c_re[None] * pw_im[:, :, None, :] + c_im[None] * pw_re[:, :, None, :]
    kern = (jnp.einsum('tgpn,gnq->gtpq', cp_re[:t], bb_re, precision=hi)
            - jnp.einsum('tgpn,gnq->gtpq', cp_im[:t], bb_im, precision=hi))
    eye_o = jnp.eye(SSM_OCT, dtype=F32)
    eye_2 = jnp.eye(2, dtype=F32)
    kbd = jnp.einsum('oatpq,ab->otaqbp', kern.reshape(no, SSM_OCT, t, p, p), eye_o)
    kbd = kbd.reshape(no, t, LANES, LANES)
    rv_re, rv_im = pw_re[t - 1::-1], pw_im[t - 1::-1]
    st_re = rv_re[..., None] * bb_re[None] - rv_im[..., None] * bb_im[None]
    st_im = rv_re[..., None] * bb_im[None] + rv_im[..., None] * bb_re[None]
    st = jnp.stack([st_re, st_im], 0).reshape(2, t, no, npr, 2, n, p)
    w_st = jnp.einsum('rjoxanq,ab->ojxraqbn', st, eye_2).reshape(no, t, npr, 2, 2 * p, 2 * n)
    wc = jnp.stack([cp_re[1:], -cp_im[1:]], 0).reshape(2, t, no, npr, 2, p, n)
    w_ct = jnp.einsum('rioxapn,ab->oixrapbn', wc, eye_2).reshape(no, t, npr, 2, 2 * p, 2 * n)
    n_lvl = int(math.log2(seq // t))
    lv_re, lv_im = power(t * 2 ** np.arange(n_lvl))
    lr = lv_re.reshape(n_lvl, no, SSM_SW)
    li = lv_im.reshape(n_lvl, no, SSM_SW)
    lvl = jnp.stack([jnp.concatenate([lr, lr], -1), jnp.concatenate([-li, li], -1)], axis=1)
    lvl = lvl.transpose(2, 0, 1, 3).reshape(no, 2 * n_lvl, 2 * SSM_SW)
    d_vec = jnp.tile(d_skip.reshape(no, 1, LANES), (1, 1, t))
    return kbd.astype(BF16), w_st.astype(BF16), w_ct.astype(BF16), lvl, d_vec


def _s5_kernel(u_ref, kbd_ref, wst_ref, wct_ref, lvl_ref, d_ref, o_ref, toe_ref, st_ref, ct_ref,
               *, n_chunks, n_lvl):
    t = SSM_T

    @pl.when(pl.program_id(0) == 0)
    def _():
        toe_ref[...] = jnp.zeros_like(toe_ref)
        st_ref[...] = jnp.zeros_like(st_ref)
        ct_ref[...] = jnp.zeros_like(ct_ref)

    for i in range(t):
        for j in range(i + 1):
            toe_ref[j * LANES:(j + 1) * LANES, i * LANES:(i + 1) * LANES] = kbd_ref[i - j]
    for j in range(t):
        for x in range(SSM_PAIRS):
            for r in range(2):
                rows = slice(j * LANES + x * 2 * SSM_P, j * LANES + (x + 1) * 2 * SSM_P)
                lanes = slice(r * SSM_SW + x * LANES, r * SSM_SW + (x + 1) * LANES)
                st_ref[rows, lanes] = wst_ref[j, x, r]
                ct_ref[rows, lanes] = wct_ref[j, x, r]

    u = u_ref[...]
    ub = u.astype(BF16)
    h = _dot(ub, st_ref[...])
    cidx = lax.broadcasted_iota(jnp.int32, (u.shape[0], 1), 0) % n_chunks
    for lv in range(n_lvl):
        d = 1 << lv
        a_rr = lvl_ref[2 * lv:2 * lv + 1, :]
        a_is = lvl_ref[2 * lv + 1:2 * lv + 2, :]
        sh = jnp.where(cidx >= d, pltpu.roll(h, d, 0), 0.0)
        h = h + sh * a_rr + pltpu.roll(sh, SSM_SW, 1) * a_is
    h_prev = jnp.where(cidx >= 1, pltpu.roll(h, 1, 0), 0.0)
    y = _dot(ub, toe_ref[...]) + _dot_nt(h_prev.astype(BF16), ct_ref[...]) + d_ref[...] * u
    o_ref[...] = jax.nn.gelu(y).astype(o_ref.dtype)


def _s5_mixer(z_ssm, batch, seq, tables):
    kbd, w_st, w_ct, lvl, d_vec = tables
    t, no = SSM_T, SSM_NOCT
    n_chunks = seq // t
    rows = batch * n_chunks
    width = t * LANES
    n_lvl = lvl.shape[1] // 2
    u = z_ssm.reshape(rows, t, no, LANES).transpose(2, 0, 1, 3).reshape(no, rows, width)
    y = pl.pallas_call(
        functools.partial(_s5_kernel, n_chunks=n_chunks, n_lvl=n_lvl),
        grid=(no,),
        in_specs=[
            pl.BlockSpec((None, rows, width), lambda i: (i, 0, 0)),
            pl.BlockSpec((None, t, LANES, LANES), lambda i: (i, 0, 0, 0)),
            pl.BlockSpec((None, t, SSM_PAIRS, 2, 2 * SSM_P, LANES), lambda i: (i, 0, 0, 0, 0, 0)),
            pl.BlockSpec((None, t, SSM_PAIRS, 2, 2 * SSM_P, LANES), lambda i: (i, 0, 0, 0, 0, 0)),
            pl.BlockSpec((None, 2 * n_lvl, 2 * SSM_SW), lambda i: (i, 0, 0)),
            pl.BlockSpec((None, 1, width), lambda i: (i, 0, 0)),
        ],
        out_specs=pl.BlockSpec((None, rows, width), lambda i: (i, 0, 0)),
        out_shape=jax.ShapeDtypeStruct((no, rows, width), BF16),
        scratch_shapes=[
            pltpu.VMEM((width, width), BF16),
            pltpu.VMEM((width, 2 * SSM_SW), BF16),
            pltpu.VMEM((width, 2 * SSM_SW), BF16),
        ],
        compiler_params=_params("arbitrary"),
        name="s5_mixer",
    )(u, kbd, w_st, w_ct, lvl, d_vec)
    return y.reshape(no, rows, t, LANES).transpose(1, 2, 0, 3).reshape(batch * seq, SSM_WIDTH)


def _rope_kernel(pos_ref, freq_ref, kr_ref, tab_ref, kro_ref):
    ang = pos_ref[...].astype(F32) * freq_ref[...]
    lane = lax.broadcasted_iota(jnp.int32, ang.shape, 1)
    sin = jnp.sin(ang)
    tab = jnp.where(lane < MLA_ROPE, jnp.cos(ang), jnp.where(lane < MLA_ROPE + MLA_ROPE // 2, -sin, sin))
    tab_ref[...] = tab
    kr = kr_ref[...]
    half = MLA_ROPE // 2
    sw = jnp.concatenate([kr[:, half:], kr[:, :half]], axis=1)
    kro_ref[...] = (kr * tab[:, :MLA_ROPE] + sw * tab[:, MLA_ROPE:]).astype(kro_ref.dtype)


def _rope_tables(positions, k_rope):
    n_tok = k_rope.shape[0]
    tm = 512
    inv_freq = ROPE_THETA ** (-np.arange(0, MLA_ROPE, 2, dtype=np.float64) / MLA_ROPE)
    freq = jnp.asarray(np.tile(inv_freq, 4)[None, :], F32)
    return pl.pallas_call(
        _rope_kernel,
        grid=(n_tok // tm,),
        in_specs=[
            pl.BlockSpec((tm, 1), lambda i: (i, 0)),
            pl.BlockSpec((1, 2 * MLA_ROPE), lambda i: (0, 0)),
            pl.BlockSpec((tm, MLA_ROPE), lambda i: (i, 0)),
        ],
        out_specs=[
            pl.BlockSpec((tm, 2 * MLA_ROPE), lambda i: (i, 0)),
            pl.BlockSpec((tm, MLA_ROPE), lambda i: (i, 0)),
        ],
        out_shape=[
            jax.ShapeDtypeStruct((n_tok, 2 * MLA_ROPE), F32),
            jax.ShapeDtypeStruct((n_tok, MLA_ROPE), BF16),
        ],
        compiler_params=_params("arbitrary"),
        name="rope_tables",
    )(positions.reshape(n_tok, 1), freq, k_rope)


ATT_TK = 256
ATT_TQ = 2 * ATT_TK
ATT_DK = 2 * MLA_NOPE


def _attn_kernel(q_ref, tab_ref, k0_ref, v0_ref, k1_ref, v1_ref, kr_ref, o_ref, kc_ref):
    qi = pl.program_id(2)
    hd = MLA_NOPE + MLA_ROPE
    scale = hd ** -0.5
    half = MLA_ROPE // 2
    tk = ATT_TK

    @pl.when(qi == 0)
    def _():
        for hh, k_ref in enumerate((k0_ref, k1_ref)):
            kc_ref[hh, :, 0:MLA_NOPE] = k_ref[...]
            kc_ref[hh, :, MLA_NOPE:hd] = kr_ref[...]
            kc_ref[hh, :, hd:] = jnp.zeros((kr_ref.shape[0], ATT_DK - hd), BF16)

    q = q_ref[...]
    cosf = tab_ref[:, :MLA_ROPE]
    sinf = tab_ref[:, MLA_ROPE:]
    qc = []
    for hh in range(2):
        qn = q[:, hh * hd:hh * hd + MLA_NOPE].astype(F32) * scale
        qr = q[:, hh * hd + MLA_NOPE:(hh + 1) * hd].astype(F32)
        sw = jnp.concatenate([qr[:, half:], qr[:, :half]], axis=1)
        qr = (qr * cosf + sw * sinf) * scale
        pad = jnp.zeros((ATT_TQ, ATT_DK - hd), F32)
        qc.append(jnp.concatenate([qn, qr, pad], axis=1).astype(BF16))
    v_refs = (v0_ref, v1_ref)
    row = lax.broadcasted_iota(jnp.int32, (tk, tk), 0)
    col = lax.broadcasted_iota(jnp.int32, (tk, tk), 1)
    visible = (col // CHUNK) <= (row // CHUNK)

    def update(state, hh, sub, j, masked):
        m, l, acc = state
        start = pl.multiple_of(j * tk, tk)
        ks = kc_ref[hh, pl.ds(start, tk), :]
        vs = v_refs[hh][pl.ds(start, tk), :]
        s = _dot_nt(qc[hh][sub * tk:(sub + 1) * tk], ks)
        if masked:
            s = jnp.where(visible, s, -1e30)
        m_new = jnp.maximum(m, jnp.max(s, axis=-1, keepdims=True))
        alpha = jnp.exp(m - m_new)
        p = jnp.exp(s - m_new)
        l = alpha * l + jnp.sum(p, axis=-1, keepdims=True)
        acc = alpha * acc + _dot(p.astype(BF16), vs)
        return m_new, l, acc

    chains = [(hh, sub) for hh in range(2) for sub in range(2)]

    def body(j, states):
        return tuple(update(st, hh, sub, j, False) for st, (hh, sub) in zip(states, chains))

    init = tuple((jnp.full((tk, 1), -1e30, F32), jnp.zeros((tk, 1), F32), jnp.zeros((tk, MLA_V), F32))
                 for _ in chains)
    states = list(lax.fori_loop(0, 2 * qi, body, init))
    for c, (hh, sub) in enumerate(chains):
        st = update(states[c], hh, sub, 2 * qi, masked=(sub == 0))
        if sub == 1:
            st = update(st, hh, sub, 2 * qi + 1, masked=True)
        _, l, acc = st
        o_ref[sub * tk:(sub + 1) * tk, hh * MLA_V:(hh + 1) * MLA_V] = (acc / l).astype(o_ref.dtype)


def _attention(q, kv, k_rope, tab, batch, seq):
    n_tok = batch * seq
    nq = seq // ATT_TQ
    kvb = lambda off: pl.BlockSpec((seq, MLA_NOPE), lambda b, hp, qi: (b, 4 * hp + off))
    return pl.pallas_call(
        _attn_kernel,
        grid=(batch, MLA_HEADS // 2, nq),
        in_specs=[
            pl.BlockSpec((ATT_TQ, 2 * (MLA_NOPE + MLA_ROPE)), lambda b, hp, qi: (b * nq + qi, hp)),
            pl.BlockSpec((ATT_TQ, 2 * MLA_ROPE), lambda b, hp, qi: (b * nq + qi, 0)),
            kvb(0), kvb(1), kvb(2), kvb(3),
            pl.BlockSpec((seq, MLA_ROPE), lambda b, hp, qi: (b, 0)),
        ],
        out_specs=pl.BlockSpec((ATT_TQ, 2 * MLA_V), lambda b, hp, qi: (b * nq + qi, hp)),
        out_shape=jax.ShapeDtypeStruct((n_tok, MLA_HEADS * MLA_V), BF16),
        scratch_shapes=[pltpu.VMEM((2, seq, ATT_DK), BF16)],
        compiler_params=_params("arbitrary", "arbitrary", "arbitrary"),
        name="mla_attention",
    )(q, tab, kv, kv, kv, kv, k_rope)


LNR_TM = 256
SLOT_LANES = 128


def _pack_halves(x):
    w = x.shape[1] // 2
    bits = lax.bitcast_convert_type(x.astype(BF16).astype(F32), U32)
    return (bits[:, :w] >> 16) | (bits[:, w:] & HI_MASK)


def _unpack_halves(p):
    return (lax.bitcast_convert_type(p << 16, F32), lax.bitcast_convert_type(p & HI_MASK, F32))


def _ln_router_kernel(h_ref, mix_ref, g_ref, b_ref, wr_ref, br_ref,
                      ho_ref, hq_ref, idx_ref, gate_ref, rank_ref, cnt_ref, carry_ref):
    @pl.when(pl.program_id(0) == 0)
    def _():
        carry_ref[...] = jnp.zeros_like(carry_ref)

    x = DEEPNORM_ALPHA * h_ref[...] + mix_ref[...]
    mu = jnp.mean(x, axis=-1, keepdims=True)
    xc = x - mu
    var = jnp.mean(xc * xc, axis=-1, keepdims=True)
    hn = xc * lax.rsqrt(var + LN_EPS) * g_ref[...] + b_ref[...]
    ho_ref[...] = hn
    hq_ref[...] = _pack_halves(hn)

    logits =jnp.dot(hn, wr_ref[...], precision=lax.Precision.HIGHEST,
                     preferred_element_type=F32) + br_ref[...]
    tm = logits.shape[0]
    lane_e = lax.broadcasted_iota(jnp.int32, (tm, N_EXPERTS), 1)
    lane_s = lax.broadcasted_iota(jnp.int32, (tm, SLOT_LANES), 1)
    work = logits
    sel = jnp.zeros((tm, N_EXPERTS), F32)
    top_v, top_i, hot = [], [], []
    for _ in range(TOP_K):
        mx = jnp.max(work, axis=-1, keepdims=True)
        ix = jnp.min(jnp.where(work == mx, lane_e, N_EXPERTS), axis=-1, keepdims=True)
        oh = lane_e == ix
        work = jnp.where(oh, -jnp.inf, work)
        sel = sel + oh.astype(F32)
        top_v.append(mx)
        top_i.append(ix)
        hot.append(oh)
    ex = [jnp.exp(v - top_v[0]) for v in top_v]
    den = ex[0] + ex[1] + ex[2] + ex[3]

    r = lax.broadcasted_iota(jnp.int32, (tm, tm), 0)
    c = lax.broadcasted_iota(jnp.int32, (tm, tm), 1)
    strict = jnp.where(c < r, 1.0, 0.0).astype(BF16)
    prefix = _dot(strict, sel.astype(BF16)) + carry_ref[0:1, :]
    carry_ref[0:1, :] = carry_ref[0:1, :] + jnp.sum(sel, axis=0, keepdims=True)
    cnt_ref[...] = jnp.broadcast_to(carry_ref[0:1, :], cnt_ref.shape)

    idx_o = jnp.zeros((tm, SLOT_LANES), jnp.int32)
    gate_o = jnp.zeros((tm, SLOT_LANES), F32)
    rank_o = jnp.zeros((tm, SLOT_LANES), jnp.int32)
    for k in range(TOP_K):
        rk = jnp.sum(jnp.where(hot[k], prefix, 0.0), axis=-1, keepdims=True)
        idx_o = jnp.where(lane_s == k, top_i[k], idx_o)
        gate_o = jnp.where(lane_s == k, ex[k] / den, gate_o)
        rank_o = jnp.where(lane_s == k, rk.astype(jnp.int32), rank_o)
    idx_ref[...] = idx_o
    gate_ref[...] = gate_o
    rank_ref[...] = rank_o


def _ln_router(h, mix, ln_g, ln_b, w_router, b_router):
    n_tok, d = h.shape
    tm = LNR_TM
    row = lambda i: (i, 0)
    fixed = lambda i: (0, 0)
    return pl.pallas_call(
        _ln_router_kernel,
        grid=(n_tok // tm,),
        in_specs=[
            pl.BlockSpec((tm, d), row), pl.BlockSpec((tm, d), row),
            pl.BlockSpec((1, d), fixed), pl.BlockSpec((1, d), fixed),
            pl.BlockSpec((d, N_EXPERTS), fixed), pl.BlockSpec((1, N_EXPERTS), fixed),
        ],
        out_specs=[
            pl.BlockSpec((tm, d), row), pl.BlockSpec((tm, d // 2), row),
            pl.BlockSpec((tm, SLOT_LANES), row), pl.BlockSpec((tm, SLOT_LANES), row),
            pl.BlockSpec((tm, SLOT_LANES), row), pl.BlockSpec((8, N_EXPERTS), fixed),
        ],
        out_shape=[
            jax.ShapeDtypeStruct((n_tok, d), F32), jax.ShapeDtypeStruct((n_tok, d // 2), U32),
            jax.ShapeDtypeStruct((n_tok, SLOT_LANES), jnp.int32),
            jax.ShapeDtypeStruct((n_tok, SLOT_LANES), F32),
            jax.ShapeDtypeStruct((n_tok, SLOT_LANES), jnp.int32),
            jax.ShapeDtypeStruct((8, N_EXPERTS), F32),
        ],
        scratch_shapes=[pltpu.VMEM((8, N_EXPERTS), F32)],
        compiler_params=_params("arbitrary"),
        name="ln_router",
    )(h, mix, ln_g.reshape(1, d), ln_b.reshape(1, d), w_router, b_router.reshape(1, N_EXPERTS))


MOE_TM = 1280
MOE_SB = 256
MOE_NSUB = MOE_TM // MOE_SB
MOE_TH = 128
MOE_TN = 512
MOE_T1 = D_EXPERT // MOE_TH
MOE_T2 = D_MODEL // 2 // MOE_TN
MOE_UNIT = 32
MOE_UPS1 = 4
MOE_UPS2 = 2
assert MOE_T1 * MOE_UPS1 + MOE_T2 * MOE_UPS2 == MOE_TM // MOE_UNIT
MOE_KD = 2 * MOE_TH


def _moe_kernel(e_ref, n_ref, prow_ref, nv_ref, ids_cur, ids_nxt, xq_hbm, wg_ref, wl_ref, bg_ref, bl_ref,
                wdl_ref, wdh_ref, bdl_ref, bdh_ref, y_hbm, xbuf, act_ref, wcat_ref, obuf, xsem, osem):
    i = pl.program_id(0)
    t = pl.program_id(1)
    n_i = n_ref[i]
    nsub_i = (n_i + MOE_SB - 1) // MOE_SB
    slot = i % 2
    half = D_MODEL // 2

    def issue_unit(ids_ref, slot_, u):
        base = u * MOE_UNIT
        for r in range(MOE_UNIT):
            pltpu.make_async_copy(xq_hbm.at[pl.ds(ids_ref[0, base + r], 1)],
                                  xbuf.at[slot_, pl.ds(base + r, 1)], xsem.at[slot_]).start()

    def slot_rows(slot_):
        return pltpu.make_async_copy(xq_hbm.at[pl.ds(0, MOE_TM)], xbuf.at[slot_], xsem.at[slot_])

    @pl.when(jnp.logical_and(i == 0, t == 0))
    def _first_block_rows():
        def body(u, c):
            issue_unit(ids_cur, 0, u)
            return c
        lax.fori_loop(0, MOE_TM // MOE_UNIT, body, 0)

    @pl.when(jnp.logical_and(t == 0, n_i > 0))
    def _wait_rows():
        slot_rows(slot).wait()

    def sub_blocks(body):
        body(0)
        for s in range(1, MOE_NSUB):
            pl.when(s < nsub_i)(functools.partial(body, s))

    @pl.when(jnp.logical_and(t < MOE_T1, n_i > 0))
    def _gate_up():
        wcat_ref[:, 0:MOE_TH] = wg_ref[...].astype(BF16)
        wcat_ref[:, MOE_TH:] = wl_ref[...].astype(BF16)

        @sub_blocks
        def _(s):
                if s == 0:
                    for uu in range(MOE_UPS1):
                        issue_unit(ids_nxt, 1 - slot, t * MOE_UPS1 + uu)
                rows = slice(s * MOE_SB, (s + 1) * MOE_SB)
                x_lo, x_hi = _unpack_halves(xbuf[slot, rows, :])
                hcat = (_dot(x_lo.astype(BF16), wcat_ref[0:half, :])
                        + _dot(x_hi.astype(BF16), wcat_ref[half:, :]))
                hg = jnp.minimum(hcat[:, :MOE_TH] + bg_ref[...], SWIGLU_LIMIT)
                hl = jnp.clip(hcat[:, MOE_TH:] + bl_ref[...], -SWIGLU_LIMIT, SWIGLU_LIMIT)
                a = (hg * jax.nn.sigmoid(SWIGLU_ALPHA * hg) * (hl + 1.0)).astype(BF16)

                @pl.when(t % 2 == 0)
                def _():
                    act_ref[t // 2, rows, 0:MOE_TH] = a

                @pl.when(t % 2 == 1)
                def _():
                    act_ref[t // 2, rows, MOE_TH:] = a

    @pl.when(jnp.logical_and(t >= MOE_T1, n_i > 0))
    def _down():
        t2 = t - MOE_T1
        oslot = t2 % 2
        wdl = wdl_ref[...].astype(BF16)
        wdh = wdh_ref[...].astype(BF16)
        col0 = pl.multiple_of(t2 * MOE_TN, MOE_TN)

        def result_copy(slot_, s):
            return pltpu.make_async_copy(
                obuf.at[slot_, pl.ds(s * MOE_SB, MOE_SB)],
                y_hbm.at[pl.ds(pl.multiple_of(prow_ref[i] + s * MOE_SB, MOE_SB), MOE_SB),
                         pl.ds(col0, MOE_TN)], osem.at[slot_])

        @sub_blocks
        def _(s):
                if s == 0:
                    for uu in range(MOE_UPS2):
                        issue_unit(ids_nxt, 1 - slot, MOE_T1 * MOE_UPS1 + t2 * MOE_UPS2 + uu)
                rows = slice(s * MOE_SB, (s + 1) * MOE_SB)
                y_lo = bdl_ref[...]
                y_hi = bdh_ref[...]
                for k in range(D_EXPERT // MOE_KD):
                    a = act_ref[k, rows, :]
                    y_lo = y_lo + _dot(a, wdl[k * MOE_KD:(k + 1) * MOE_KD, :])
                    y_hi = y_hi + _dot(a, wdh[k * MOE_KD:(k + 1) * MOE_KD, :])
                obuf[oslot, rows, :] = _pack_halves(jnp.concatenate([y_lo, y_hi], axis=1))
                result_copy(oslot, s).start()

        for s in range(MOE_NSUB):
            @pl.when(jnp.logical_and(s < nsub_i, t2 >= 1))
            def _():
                result_copy(1 - oslot, s).wait()

            @pl.when(jnp.logical_and(s < nsub_i, t2 == MOE_T2 - 1))
            def _():
                result_copy(oslot, s).wait()

    @pl.when(jnp.logical_and(i == nv_ref[0] - 1, t == MOE_T1 + MOE_T2 - 1))
    def _zero_tail():
        zslot = 1 - slot
        slot_rows(zslot).wait()
        xbuf[zslot, 0:MOE_SB, :] = jnp.zeros((MOE_SB, half), U32)

        def tail_copy(j):
            return pltpu.make_async_copy(
                xbuf.at[zslot, pl.ds(0, MOE_SB)],
                y_hbm.at[pl.ds(pl.multiple_of(j * MOE_SB, MOE_SB), MOE_SB)], xsem.at[zslot])

        def start(j, c):
            tail_copy(j).start()
            return c

        def wait(j, c):
            tail_copy(j).wait()
            return c

        first = nv_ref[1] // MOE_SB
        total = y_hbm.shape[0] // MOE_SB
        lax.fori_loop(first, total, start, 0)
        lax.fori_loop(first, total, wait, 0)


def _moe_experts(xq, blk_exp, blk_n, blk_prow, n_valid_blocks, ids, w_gu, b_gu, w_down, b_down, layer,
                 n_rows_out):
    nb = ids.shape[0]
    t_last = MOE_T1 + MOE_T2 - 1
    half = D_MODEL // 2
    hi_blk = half // MOE_TN

    def src(i, nv):
        return jnp.maximum(jnp.minimum(i, nv[0] - 1), 0)

    def tt(i, t, nv):
        return jnp.where(i < nv[0], t, t_last)

    def t1(i, t, nv):
        return jnp.minimum(tt(i, t, nv), MOE_T1 - 1)

    def t2(i, t, nv):
        return jnp.maximum(tt(i, t, nv) - MOE_T1, 0)

    ids_cur_map = lambda i, t, e, n, p, nv: (src(i, nv), 0, 0)
    ids_nxt_map = lambda i, t, e, n, p, nv: (jnp.minimum(src(i, nv) + 1, nb - 1), 0, 0)
    wg_map = lambda i, t, e, n, p, nv: (layer, e[i], 0, t1(i, t, nv))
    wl_map = lambda i, t, e, n, p, nv: (layer, e[i], 0, MOE_T1 + t1(i, t, nv))
    wdl_map = lambda i, t, e, n, p, nv: (layer, e[i], 0, t2(i, t, nv))
    wdh_map = lambda i, t, e, n, p, nv: (layer, e[i], 0, hi_blk + t2(i, t, nv))
    grid_spec = pltpu.PrefetchScalarGridSpec(
        num_scalar_prefetch=4,
        grid=(nb, MOE_T1 + MOE_T2),
        in_specs=[
            pl.BlockSpec((None, 1, MOE_TM), ids_cur_map, memory_space=pltpu.SMEM),
            pl.BlockSpec((None, 1, MOE_TM), ids_nxt_map, memory_space=pltpu.SMEM),
            pl.BlockSpec(memory_space=pl.ANY),
            pl.BlockSpec((None, None, D_MODEL, MOE_TH), wg_map),
            pl.BlockSpec((None, None, D_MODEL, MOE_TH), wl_map),
            pl.BlockSpec((None, None, 1, MOE_TH), wg_map),
            pl.BlockSpec((None, None, 1, MOE_TH), wl_map),
            pl.BlockSpec((None, None, D_EXPERT, MOE_TN), wdl_map),
            pl.BlockSpec((None, None, D_EXPERT, MOE_TN), wdh_map),
            pl.BlockSpec((None, None, 1, MOE_TN), wdl_map),
            pl.BlockSpec((None, None, 1, MOE_TN), wdh_map),
        ],
        out_specs=pl.BlockSpec(memory_space=pl.ANY),
        scratch_shapes=[
            pltpu.VMEM((2, MOE_TM, half), U32),
            pltpu.VMEM((D_EXPERT // MOE_KD, MOE_TM, MOE_KD), BF16),
            pltpu.VMEM((D_MODEL, 2 * MOE_TH), BF16),
            pltpu.VMEM((2, MOE_TM, MOE_TN), U32),
            pltpu.SemaphoreType.DMA((2,)),
            pltpu.SemaphoreType.DMA((2,)),
        ],
    )
    bgu = b_gu.reshape(DEPTH, N_EXPERTS, 1, 2 * D_EXPERT)
    ids3 = ids.reshape(nb, 1, MOE_TM)
    bdn = b_down.reshape(DEPTH, N_EXPERTS, 1, D_MODEL)
    return pl.pallas_call(
        _moe_kernel,
        grid_spec=grid_spec,
        out_shape=jax.ShapeDtypeStruct((n_rows_out, half), U32),
        compiler_params=_params("arbitrary", "arbitrary"),
        name="moe_experts",
    )(blk_exp, blk_n, blk_prow, n_valid_blocks, ids3, ids3, xq, w_gu, w_gu, bgu, bgu,
      w_down, w_down, bdn, bdn)


LNC_TM = 128


def _ln_combine_kernel(ids_cur, ids_nxt, h_ref, gate_ref, g_ref, b_ref, y_hbm, ho_ref, hb_ref, ybuf, sem):
    i = pl.program_id(0)
    tm = h_ref.shape[0]
    w = h_ref.shape[1] // 2
    rows = TOP_K * tm
    slot = i % 2

    def row_copy(ids_ref, slot_, idx):
        return pltpu.make_async_copy(y_hbm.at[pl.ds(ids_ref[0, idx], 1)],
                                     ybuf.at[slot_, pl.ds(idx, 1)], sem.at[slot_])

    def tile_rows(slot_):
        return pltpu.make_async_copy(y_hbm.at[pl.ds(0, rows)], ybuf.at[slot_], sem.at[slot_])

    @pl.when(i == 0)
    def _():
        def body(u, c):
            for r in range(MOE_UNIT):
                row_copy(ids_cur, 0, u * MOE_UNIT + r).start()
            return c
        lax.fori_loop(0, rows // MOE_UNIT, body, 0)

    tile_rows(slot).wait()
    for idx in range(rows):
        row_copy(ids_nxt, 1 - slot, idx).start()

    x_lo = DEEPNORM_ALPHA * h_ref[:, :w]
    x_hi = DEEPNORM_ALPHA * h_ref[:, w:]
    for k in range(TOP_K):
        lo, hi = _unpack_halves(ybuf[slot, k * tm:(k + 1) * tm, :])
        gk = gate_ref[:, k:k + 1]
        x_lo = x_lo + gk * lo
        x_hi = x_hi + gk * hi
    x = jnp.concatenate([x_lo, x_hi], axis=1)
    mu = jnp.mean(x, axis=-1, keepdims=True)
    xc = x - mu
    var = jnp.mean(xc * xc, axis=-1, keepdims=True)
    hn = xc * lax.rsqrt(var + LN_EPS) * g_ref[...] + b_ref[...]
    ho_ref[...] = hn
    hb_ref[...] = hn.astype(BF16)

    @pl.when(i == pl.num_programs(0) - 1)
    def _():
        tile_rows(1 - slot).wait()


def _ln_combine(h, yq, dest_tiles, gates, ln_g, ln_b):
    n_tok, d = h.shape
    tm = LNC_TM
    nblk = n_tok // tm
    row = lambda i: (i, 0)
    fixed = lambda i: (0, 0)
    dest3 = dest_tiles.reshape(nblk, 1, TOP_K * tm)
    return pl.pallas_call(
        _ln_combine_kernel,
        grid=(nblk,),
        in_specs=[
            pl.BlockSpec((None, 1, TOP_K * tm), lambda i: (i, 0, 0), memory_space=pltpu.SMEM),
            pl.BlockSpec((None, 1, TOP_K * tm), lambda i: (jnp.minimum(i + 1, nblk - 1), 0, 0),
                         memory_space=pltpu.SMEM),
            pl.BlockSpec((tm, d), row),
            pl.BlockSpec((tm, SLOT_LANES), row),
            pl.BlockSpec((1, d), fixed), pl.BlockSpec((1, d), fixed),
            pl.BlockSpec(memory_space=pl.ANY),
        ],
        out_specs=[pl.BlockSpec((tm, d), row), pl.BlockSpec((tm, d), row)],
        out_shape=[jax.ShapeDtypeStruct((n_tok, d), F32), jax.ShapeDtypeStruct((n_tok, d), BF16)],
        scratch_shapes=[pltpu.VMEM((2, TOP_K * tm, d // 2), U32), pltpu.SemaphoreType.DMA((2,))],
        compiler_params=_params("arbitrary"),
        name="ln_combine",
    )(dest3, dest3, h, gates, ln_g.reshape(1, d), ln_b.reshape(1, d), yq)


def _moe_layer(h, hq, top_idx, gates, rank, counts, w_gu, b_gu, w_down, b_down, layer, ln_g, ln_b):
    n_tok = h.shape[0]
    n_assign = n_tok * TOP_K
    nb = N_EXPERTS + n_assign // MOE_TM
    n_rows_out = n_assign + N_EXPERTS * MOE_SB
    i32 = jnp.int32
    cnt = counts[0].astype(i32)
    pcnt = (cnt + MOE_SB - 1) // MOE_SB * MOE_SB
    pstart = jnp.cumsum(pcnt) - pcnt
    nblk = (cnt + MOE_TM - 1) // MOE_TM
    bend = jnp.cumsum(nblk)
    bfirst = bend - nblk
    n_valid = bend[-1]
    e_flat = top_idx[:, :TOP_K].reshape(-1)
    rank_flat = rank[:, :TOP_K].reshape(-1)
    tok = jnp.arange(n_assign, dtype=i32) // TOP_K
    dest = pstart[e_flat] + rank_flat
    blk = jnp.arange(nb, dtype=i32)
    src = jnp.maximum(jnp.minimum(blk, n_valid - 1), 0)
    be = jnp.minimum(jnp.sum((bend[None, :] <= src[:, None]).astype(i32), axis=1), N_EXPERTS - 1)
    b_in = src - bfirst[be]
    blk_n = jnp.where(blk < n_valid, jnp.clip(cnt[be] - b_in * MOE_TM, 0, MOE_TM), 0).astype(i32)
    blk_prow = (pstart[be] + b_in * MOE_TM).astype(i32)
    id_pos = (bfirst[e_flat] + rank_flat // MOE_TM) * MOE_TM + rank_flat % MOE_TM
    ids = jnp.zeros((nb * MOE_TM,), i32).at[id_pos].set(tok).reshape(nb, MOE_TM)
    used_rows = pstart[-1] + pcnt[-1]
    yq = _moe_experts(hq, be.astype(i32), blk_n, blk_prow, jnp.stack([n_valid, used_rows]).astype(i32), ids,
                      w_gu, b_gu, w_down, b_down, layer, n_rows_out)
    tm = LNC_TM
    dest_tiles = dest.reshape(n_tok // tm, tm, TOP_K).transpose(0, 2, 1).reshape(n_tok // tm, TOP_K * tm)
    return _ln_combine(h, yq, dest_tiles, gates, ln_g, ln_b)


def kernel(x, positions, ln_g, ln_b, hy_w_in, hy_w_out, gm_ln_g, gm_ln_b, gm_w_s, gm_b_s, ssm_lam_re, ssm_lam_im, ssm_log_dt, ssm_b_re, ssm_b_im, ssm_c_re, ssm_c_im, ssm_d, ssm_w_glu, ssm_b_glu, mla_w_in, mla_q_norm_g, mla_kv_norm_g, mla_w_uq, mla_w_ukv, mla_w_o, moe_w_router, moe_b_router, moe_w_gu, moe_b_gu, moe_w_down, moe_b_down):
    batch, seq, d = x.shape
    n_tok = batch * seq
    h = x.reshape(n_tok, d)
    hb = h.astype(BF16)
    for layer in range(DEPTH):
        i = layer // 2
        if layer % 2 == 0:
            z_gm = _mm([hb], hy_w_in[i], col0=0, n_cols=2 * GM_WIDTH, tm=1024, tn=256,
                       out_dtype=BF16, epilogue="gelu", name="hy_in_gm")
            z_ssm = _mm([hb], hy_w_in[i], col0=2 * GM_WIDTH, n_cols=SSM_WIDTH, tm=1024, tn=256,
                        out_dtype=F32, name="hy_in_ssm")
            y_gm = _spatial_gating(z_gm, gm_ln_g[i], gm_ln_b[i], gm_w_s[i], gm_b_s[i])
            tables = _s5_tables(ssm_lam_re[i], ssm_lam_im[i], ssm_log_dt[i], ssm_b_re[i], ssm_b_im[i],
                                ssm_c_re[i], ssm_c_im[i], ssm_d[i], seq)
            y_act = _s5_mixer(z_ssm, batch, seq, tables)
            y_ssm = _mm([y_act], ssm_w_glu[i], tm=1024, tn=512, out_dtype=BF16, epilogue="glu",
                        bias=ssm_b_glu[i], mul=y_act, name="s5_glu")
            mix = _mm([y_gm, y_ssm], hy_w_out[i], tm=1024, tn=256, out_dtype=F32, name="hy_out")
        else:
            w_in = mla_w_in[i]
            n_main = MLA_Q_RANK + MLA_KV_RANK
            c_main = _mm([hb], w_in, col0=0, n_cols=n_main, tm=1024, tn=256, out_dtype=F32,
                         name="mla_in")
            k_rope = _mm([hb], w_in[:, n_main:], tm=1024, tn=MLA_ROPE, out_dtype=F32, name="mla_in_rope")
            tab, k_rope = _rope_tables(positions, k_rope)
            q = _mm([(c_main, MLA_Q_RANK, 0)], mla_w_uq[i], tm=1024, tn=768, out_dtype=BF16,
                    prologue="rms", gain=mla_q_norm_g[i], name="mla_uq")
            kv = _mm([(c_main, MLA_KV_RANK, MLA_Q_RANK // MLA_KV_RANK)], mla_w_ukv[i], tm=1024, tn=1024,
                     out_dtype=BF16, prologue="rms", gain=mla_kv_norm_g[i], name="mla_ukv")
            o = _attention(q, kv, k_rope, tab, batch, seq)
            mix = _mm([o], mla_w_o[i], tm=1024, tn=256, out_dtype=F32, name="mla_out")
        h, hq, top_idx, gates, rank, counts = _ln_router(
            h, mix, ln_g[layer, 0], ln_b[layer, 0], moe_w_router[layer], moe_b_router[layer])
        h, hb = _moe_layer(h, hq, top_idx, gates, rank, counts, moe_w_gu, moe_b_gu,
                           moe_w_down, moe_b_down, layer, ln_g[layer, 1], ln_b[layer, 1])
    return h.reshape(batch, seq, d)
```

```python
import functools
import math

import numpy as np
import jax
import jax.numpy as jnp
from jax import lax
from jax.experimental import pallas as pl
from jax.experimental.pallas import tpu as pltpu

F32 = jnp.float32
BF16 = jnp.bfloat16
U32 = jnp.uint32
HI_MASK = np.uint32(0xFFFF0000)

D_MODEL = 4096
DEPTH = 2
CHUNK = 64
DEEPNORM_ALPHA = (2 * DEPTH) ** 0.25
LN_EPS = 1e-5
RMS_EPS = 1e-6
LANES = 128

GM_WIDTH = 2048
GM_GROUPS = 8
GM_GROUP_DIM = 256
GM_BLOCK = 128
SSM_WIDTH = 2048
SSM_P = 16
SSM_GROUPS = 128
SSM_N = 64
SSM_T = 16
SSM_OCT = LANES // SSM_P
SSM_NOCT = SSM_GROUPS // SSM_OCT
SSM_PAIRS = SSM_OCT // 2
SSM_SW = SSM_OCT * SSM_N

MLA_HEADS = 32
MLA_Q_RANK = 1024
MLA_KV_RANK = 512
MLA_NOPE = 128
MLA_ROPE = 64
MLA_V = 128
ROPE_THETA = 10000.0

N_EXPERTS = 32
TOP_K = 4
D_EXPERT = 1024
SWIGLU_LIMIT = 7.0
SWIGLU_ALPHA = 1.702

VMEM_LIMIT_BYTES = 56 * 1024 * 1024


def _params(*sem):
    return pltpu.CompilerParams(dimension_semantics=sem, vmem_limit_bytes=VMEM_LIMIT_BYTES)


def _dot(a, b):
    return jnp.dot(a, b, preferred_element_type=F32)


def _dot_nt(a, b):
    return lax.dot_general(a, b, (((1,), (1,)), ((), ())), preferred_element_type=F32)


def _mm_kernel(*refs, prologue, epilogue, n_x):
    it = iter(refs)
    x_refs = [next(it) for _ in range(n_x)]
    w_ref = next(it)
    g_ref = next(it) if prologue == "rms" else None
    b_ref = next(it) if epilogue == "glu" else None
    y_ref = next(it) if epilogue == "glu" else None
    o_ref = next(it)
    xs_ref = next(it) if prologue == "rms" else None

    if prologue == "rms":
        @pl.when(pl.program_id(1) == 0)
        def _():
            xf = x_refs[0][...].astype(F32)
            ms = jnp.mean(xf * xf, axis=-1, keepdims=True)
            xs_ref[...] = (xf * lax.rsqrt(ms + RMS_EPS) * g_ref[...]).astype(BF16)
        xs = [xs_ref[...]]
    else:
        xs = [r[...].astype(BF16) for r in x_refs]

    acc = None
    k0 = 0
    for x in xs:
        kk = x.shape[1]
        part = _dot(x, w_ref[k0:k0 + kk, :].astype(BF16))
        acc = part if acc is None else acc + part
        k0 += kk
    if epilogue == "gelu":
        acc = jax.nn.gelu(acc)
    elif epilogue == "glu":
        acc = y_ref[...].astype(F32) * jax.nn.sigmoid(acc + b_ref[...])
    o_ref[...] = acc.astype(o_ref.dtype)


def _mm(xs, w, *, col0=0, n_cols=None, tm, tn, out_dtype, prologue=None, gain=None,
        epilogue=None, bias=None, mul=None, name):
    xs = [(x, x.shape[1], 0) if not isinstance(x, tuple) else x for x in xs]
    m = xs[0][0].shape[0]
    k_total = w.shape[0]
    n_cols = w.shape[1] - col0 if n_cols is None else n_cols
    tn = min(tn, n_cols)
    assert m % tm == 0 and n_cols % tn == 0 and col0 % tn == 0
    cb0 = col0 // tn
    assert sum(kw for _, kw, _ in xs) == k_total
    in_specs = [pl.BlockSpec((tm, kw), functools.partial(lambda i, j, cb: (i, cb), cb=cb))
                for _, kw, cb in xs]
    in_specs.append(pl.BlockSpec((k_total, tn), lambda i, j: (0, cb0 + j)))
    args = [x for x, _, _ in xs] + [w]
    scratch = []
    if prologue == "rms":
        in_specs.append(pl.BlockSpec((1, k_total), lambda i, j: (0, 0)))
        args.append(gain.reshape(1, k_total))
        scratch.append(pltpu.VMEM((tm, k_total), BF16))
    if epilogue == "glu":
        in_specs.append(pl.BlockSpec((1, tn), lambda i, j: (0, j)))
        in_specs.append(pl.BlockSpec((tm, tn), lambda i, j: (i, j)))
        args += [bias.reshape(1, n_cols), mul]
    return pl.pallas_call(
        functools.partial(_mm_kernel, prologue=prologue, epilogue=epilogue, n_x=len(xs)),
        grid=(m // tm, n_cols // tn),
        in_specs=in_specs,
        out_specs=pl.BlockSpec((tm, tn), lambda i, j: (i, j)),
        out_shape=jax.ShapeDtypeStruct((m, n_cols), out_dtype),
        scratch_shapes=scratch,
        compiler_params=_params("arbitrary", "arbitrary"),
        name=name,
    )(*args)


def _gating_kernel(u_ref, v_ref, g_ref, b_ref, ws_ref, bst_ref, o_ref):
    row = lax.broadcasted_iota(jnp.int32, (GM_BLOCK, GM_BLOCK), 0)
    col = lax.broadcasted_iota(jnp.int32, (GM_BLOCK, GM_BLOCK), 1)
    visible = (col // CHUNK) <= (row // CHUNK)
    for g in range(GM_GROUPS):
        sl = slice(g * GM_GROUP_DIM, (g + 1) * GM_GROUP_DIM)
        v = v_ref[:, sl].astype(F32)
        mu = jnp.mean(v, axis=-1, keepdims=True)
        vc = v - mu
        var = jnp.mean(vc * vc, axis=-1, keepdims=True)
        vn = vc * lax.rsqrt(var + LN_EPS) * g_ref[:, sl] + b_ref[:, sl]
        w = jnp.where(visible, ws_ref[g], 0.0).astype(BF16)
        s = _dot(w, vn.astype(BF16)) + bst_ref[:, g:g + 1]
        o_ref[:, sl] = (u_ref[:, sl].astype(F32) * s).astype(o_ref.dtype)


def _spatial_gating(z_gm, ln_g, ln_b, w_s, b_s):
    n_tok = z_gm.shape[0]
    return pl.pallas_call(
        _gating_kernel,
        grid=(n_tok // GM_BLOCK,),
        in_specs=[
            pl.BlockSpec((GM_BLOCK, GM_WIDTH), lambda i: (i, 0)),
            pl.BlockSpec((GM_BLOCK, GM_WIDTH), lambda i: (i, 1)),
            pl.BlockSpec((1, GM_WIDTH), lambda i: (0, 0)),
            pl.BlockSpec((1, GM_WIDTH), lambda i: (0, 0)),
            pl.BlockSpec((GM_GROUPS, GM_BLOCK, GM_BLOCK), lambda i: (0, 0, 0)),
            pl.BlockSpec((GM_BLOCK, GM_GROUPS), lambda i: (0, 0)),
        ],
        out_specs=pl.BlockSpec((GM_BLOCK, GM_WIDTH), lambda i: (i, 0)),
        out_shape=jax.ShapeDtypeStruct((n_tok, GM_WIDTH), BF16),
        compiler_params=_params("arbitrary"),
        name="spatial_gating",
    )(z_gm, z_gm, ln_g.reshape(1, GM_WIDTH), ln_b.reshape(1, GM_WIDTH), w_s, b_s.T)


def _s5_tables(lam_re, lam_im, log_dt, b_re, b_im, c_re, c_im, d_skip, seq):
    hi = lax.Precision.HIGHEST
    n, p, t = SSM_N, SSM_P, SSM_T
    no, npr = SSM_NOCT, SSM_PAIRS
    dt = jnp.exp(log_dt)[:, None]
    mag = jnp.exp(lam_re * dt)
    ab_re = mag * jnp.cos(lam_im * dt)
    ab_im = mag * jnp.sin(lam_im * dt)
    den = lam_re * lam_re + lam_im * lam_im
    num_re = ab_re - 1.0
    coef_re = (num_re * lam_re + ab_im * lam_im) / den
    coef_im = (ab_im * lam_re - num_re * lam_im) / den
    bb_re = coef_re[..., None] * b_re - coef_im[..., None] * b_im
    bb_im = coef_re[..., None] * b_im + coef_im[..., None] * b_re

    def power(k):
        k = jnp.asarray(k, F32)[..., None, None]
        mk = jnp.exp(k * (lam_re * dt))
        return mk * jnp.cos(k * (lam_im * dt)), mk * jnp.sin(k * (lam_im * dt))

    pw_re, pw_im = power(np.arange(t + 1))
    cp_re = c_re[None] * pw_re[:, :, None, :] - c_im[None] * pw_im[:, :, None, :]
    cp_im = c_re[None] * pw_im[:, :, None, :] + c_im[None] * pw_re[:, :, None, :]
    kern = (jnp.einsum('tgpn,gnq->gtpq', cp_re[:t], bb_re, precision=hi)
            - jnp.einsum('tgpn,gnq->gtpq', cp_im[:t], bb_im, precision=hi))
    eye_o = jnp.eye(SSM_OCT, dtype=F32)
    eye_2 = jnp.eye(2, dtype=F32)
    kbd = jnp.einsum('oatpq,ab->otaqbp', kern.reshape(no, SSM_OCT, t, p, p), eye_o)
    kbd = kbd.reshape(no, t, LANES, LANES)
    rv_re, rv_im = pw_re[t - 1::-1], pw_im[t - 1::-1]
    st_re = rv_re[..., None] * bb_re[None] - rv_im[..., None] * bb_im[None]
    st_im = rv_re[..., None] * bb_im[None] + rv_im[..., None] * bb_re[None]
    st = jnp.stack([st_re, st_im], 0).reshape(2, t, no, npr, 2, n, p)
    w_st = jnp.einsum('rjoxanq,ab->ojxraqbn', st, eye_2).reshape(no, t, npr, 2, 2 * p, 2 * n)
    wc = jnp.stack([cp_re[1:], -cp_im[1:]], 0).reshape(2, t, no, npr, 2, p, n)
    w_ct = jnp.einsum('rioxapn,ab->oixrapbn', wc, eye_2).reshape(no, t, npr, 2, 2 * p, 2 * n)
    n_lvl = int(math.log2(seq // t))
    lv_re, lv_im = power(t * 2 ** np.arange(n_lvl))
    lr = lv_re.reshape(n_lvl, no, SSM_SW)
    li = lv_im.reshape(n_lvl, no, SSM_SW)
    lvl = jnp.stack([jnp.concatenate([lr, lr], -1), jnp.concatenate([-li, li], -1)], axis=1)
    lvl = lvl.transpose(2, 0, 1, 3).reshape(no, 2 * n_lvl, 2 * SSM_SW)
    d_vec = jnp.tile(d_skip.reshape(no, 1, LANES), (1, 1, t))
    return kbd.astype(BF16), w_st.astype(BF16), w_ct.astype(BF16), lvl, d_vec


def _s5_kernel(u_ref, kbd_ref, wst_ref, wct_ref, lvl_ref, d_ref, o_ref, toe_ref, st_ref, ct_ref,
               *, n_chunks, n_lvl):
    t = SSM_T

    @pl.when(pl.program_id(0) == 0)
    def _():
        toe_ref[...] = jnp.zeros_like(toe_ref)
        st_ref[...] = jnp.zeros_like(st_ref)
        ct_ref[...] = jnp.zeros_like(ct_ref)

    for i in range(t):
        for j in range(i + 1):
            toe_ref[j * LANES:(j + 1) * LANES, i * LANES:(i + 1) * LANES] = kbd_ref[i - j]
    for j in range(t):
        for x in range(SSM_PAIRS):
            for r in range(2):
                rows = slice(j * LANES + x * 2 * SSM_P, j * LANES + (x + 1) * 2 * SSM_P)
                lanes = slice(r * SSM_SW + x * LANES, r * SSM_SW + (x + 1) * LANES)
                st_ref[rows, lanes] = wst_ref[j, x, r]
                ct_ref[rows, lanes] = wct_ref[j, x, r]

    u = u_ref[...]
    ub = u.astype(BF16)
    h = _dot(ub, st_ref[...])
    cidx = lax.broadcasted_iota(jnp.int32, (u.shape[0], 1), 0) % n_chunks
    for lv in range(n_lvl):
        d = 1 << lv
        a_rr = lvl_ref[2 * lv:2 * lv + 1, :]
        a_is = lvl_ref[2 * lv + 1:2 * lv + 2, :]
        sh = jnp.where(cidx >= d, pltpu.roll(h, d, 0), 0.0)
        h = h + sh * a_rr + pltpu.roll(sh, SSM_SW, 1) * a_is
    h_prev = jnp.where(cidx >= 1, pltpu.roll(h, 1, 0), 0.0)
    y = _dot(ub, toe_ref[...]) + _dot_nt(h_prev.astype(BF16), ct_ref[...]) + d_ref[...] * u
    o_ref[...] = jax.nn.gelu(y).astype(o_ref.dtype)


def _s5_mixer(z_ssm, batch, seq, tables):
    kbd, w_st, w_ct, lvl, d_vec = tables
    t, no = SSM_T, SSM_NOCT
    n_chunks = seq // t
    rows = batch * n_chunks
    width = t * LANES
    n_lvl = lvl.shape[1] // 2
    u = z_ssm.reshape(rows, t, no, LANES).transpose(2, 0, 1, 3).reshape(no, rows, width)
    y = pl.pallas_call(
        functools.partial(_s5_kernel, n_chunks=n_chunks, n_lvl=n_lvl),
        grid=(no,),
        in_specs=[
            pl.BlockSpec((None, rows, width), lambda i: (i, 0, 0)),
            pl.BlockSpec((None, t, LANES, LANES), lambda i: (i, 0, 0, 0)),
            pl.BlockSpec((None, t, SSM_PAIRS, 2, 2 * SSM_P, LANES), lambda i: (i, 0, 0, 0, 0, 0)),
            pl.BlockSpec((None, t, SSM_PAIRS, 2, 2 * SSM_P, LANES), lambda i: (i, 0, 0, 0, 0, 0)),
            pl.BlockSpec((None, 2 * n_lvl, 2 * SSM_SW), lambda i: (i, 0, 0)),
            pl.BlockSpec((None, 1, width), lambda i: (i, 0, 0)),
        ],
        out_specs=pl.BlockSpec((None, rows, width), lambda i: (i, 0, 0)),
        out_shape=jax.ShapeDtypeStruct((no, rows, width), BF16),
        scratch_shapes=[
            pltpu.VMEM((width, width), BF16),
            pltpu.VMEM((width, 2 * SSM_SW), BF16),
            pltpu.VMEM((width, 2 * SSM_SW), BF16),
        ],
        compiler_params=_params("arbitrary"),
        name="s5_mixer",
    )(u, kbd, w_st, w_ct, lvl, d_vec)
    return y.reshape(no, rows, t, LANES).transpose(1, 2, 0, 3).reshape(batch * seq, SSM_WIDTH)


def _rope_kernel(pos_ref, freq_ref, kr_ref, tab_ref, kro_ref):
    ang = pos_ref[...].astype(F32) * freq_ref[...]
    lane = lax.broadcasted_iota(jnp.int32, ang.shape, 1)
    sin = jnp.sin(ang)
    tab = jnp.where(lane < MLA_ROPE, jnp.cos(ang), jnp.where(lane < MLA_ROPE + MLA_ROPE // 2, -sin, sin))
    tab_ref[...] = tab
    kr = kr_ref[...]
    half = MLA_ROPE // 2
    sw = jnp.concatenate([kr[:, half:], kr[:, :half]], axis=1)
    kro_ref[...] = (kr * tab[:, :MLA_ROPE] + sw * tab[:, MLA_ROPE:]).astype(kro_ref.dtype)


def _rope_tables(positions, k_rope):
    n_tok = k_rope.shape[0]
    tm = 512
    inv_freq = ROPE_THETA ** (-np.arange(0, MLA_ROPE, 2, dtype=np.float64) / MLA_ROPE)
    freq = jnp.asarray(np.tile(inv_freq, 4)[None, :], F32)
    return pl.pallas_call(
        _rope_kernel,
        grid=(n_tok // tm,),
        in_specs=[
            pl.BlockSpec((tm, 1), lambda i: (i, 0)),
            pl.BlockSpec((1, 2 * MLA_ROPE), lambda i: (0, 0)),
            pl.BlockSpec((tm, MLA_ROPE), lambda i: (i, 0)),
        ],
        out_specs=[
            pl.BlockSpec((tm, 2 * MLA_ROPE), lambda i: (i, 0)),
            pl.BlockSpec((tm, MLA_ROPE), lambda i: (i, 0)),
        ],
        out_shape=[
            jax.ShapeDtypeStruct((n_tok, 2 * MLA_ROPE), F32),
            jax.ShapeDtypeStruct((n_tok, MLA_ROPE), BF16),
        ],
        compiler_params=_params("arbitrary"),
        name="rope_tables",
    )(positions.reshape(n_tok, 1), freq, k_rope)


ATT_TK = 256
ATT_TQ = 2 * ATT_TK
ATT_DK = 2 * MLA_NOPE


def _attn_kernel(q_ref, tab_ref, k0_ref, v0_ref, k1_ref, v1_ref, kr_ref, o_ref, kc_ref):
    qi = pl.program_id(2)
    hd = MLA_NOPE + MLA_ROPE
    scale = hd ** -0.5 * math.log2(math.e)
    half = MLA_ROPE // 2
    tk = ATT_TK

    @pl.when(qi == 0)
    def _():
        for hh, k_ref in enumerate((k0_ref, k1_ref)):
            kc_ref[hh, :, 0:MLA_NOPE] = k_ref[...]
            kc_ref[hh, :, MLA_NOPE:hd] = kr_ref[...]
            kc_ref[hh, :, hd:] = jnp.zeros((kr_ref.shape[0], ATT_DK - hd), BF16)

    q = q_ref[...]
    cosf = tab_ref[:, :MLA_ROPE]
    sinf = tab_ref[:, MLA_ROPE:]
    qc = []
    for hh in range(2):
        qn = q[:, hh * hd:hh * hd + MLA_NOPE].astype(F32) * scale
        qr = q[:, hh * hd + MLA_NOPE:(hh + 1) * hd].astype(F32)
        sw = jnp.concatenate([qr[:, half:], qr[:, :half]], axis=1)
        qr = (qr * cosf + sw * sinf) * scale
        pad = jnp.zeros((ATT_TQ, ATT_DK - hd), F32)
        qc.append(jnp.concatenate([qn, qr, pad], axis=1).astype(BF16))
    v_refs = (v0_ref, v1_ref)
    row = lax.broadcasted_iota(jnp.int32, (tk, tk), 0)
    col = lax.broadcasted_iota(jnp.int32, (tk, tk), 1)
    visible = (col // CHUNK) <= (row // CHUNK)

    def update(state, hh, sub, j, masked):
        m, l, acc = state
        start = pl.multiple_of(j * tk, tk)
        ks = kc_ref[hh, pl.ds(start, tk), :]
        vs = v_refs[hh][pl.ds(start, tk), :]
        s = _dot_nt(qc[hh][sub * tk:(sub + 1) * tk], ks)
        if masked:
            s = jnp.where(visible, s, -1e30)
        m_new = jnp.maximum(m, jnp.max(s, axis=-1, keepdims=True))
        alpha = jnp.exp2(m - m_new)
        p = jnp.exp2(s - m_new)
        l = alpha * l + jnp.sum(p, axis=-1, keepdims=True)
        acc = alpha * acc + _dot(p.astype(BF16), vs)
        return m_new, l, acc

    chains = [(hh, sub) for hh in range(2) for sub in range(2)]

    def body(jj, states):
        states = tuple(update(st, hh, sub, 2 * jj, False) for st, (hh, sub) in zip(states, chains))
        return tuple(update(st, hh, sub, 2 * jj + 1, False) for st, (hh, sub) in zip(states, chains))

    init = tuple((jnp.full((tk, 1), -1e30, F32), jnp.zeros((tk, 1), F32), jnp.zeros((tk, MLA_V), F32))
                 for _ in chains)
    states = list(lax.fori_loop(0, qi, body, init))
    for c, (hh, sub) in enumerate(chains):
        st = update(states[c], hh, sub, 2 * qi, masked=(sub == 0))
        if sub == 1:
            st = update(st, hh, sub, 2 * qi + 1, masked=True)
        _, l, acc = st
        o_ref[sub * tk:(sub + 1) * tk, hh * MLA_V:(hh + 1) * MLA_V] = (acc / l).astype(o_ref.dtype)


def _attention(q, kv, k_rope, tab, batch, seq):
    n_tok = batch * seq
    nq = seq // ATT_TQ
    kvb = lambda off: pl.BlockSpec((seq, MLA_NOPE), lambda b, hp, qi: (b, 4 * hp + off))
    return pl.pallas_call(
        _attn_kernel,
        grid=(batch, MLA_HEADS // 2, nq),
        in_specs=[
            pl.BlockSpec((ATT_TQ, 2 * (MLA_NOPE + MLA_ROPE)), lambda b, hp, qi: (b * nq + qi, hp)),
            pl.BlockSpec((ATT_TQ, 2 * MLA_ROPE), lambda b, hp, qi: (b * nq + qi, 0)),
            kvb(0), kvb(1), kvb(2), kvb(3),
            pl.BlockSpec((seq, MLA_ROPE), lambda b, hp, qi: (b, 0)),
        ],
        out_specs=pl.BlockSpec((ATT_TQ, 2 * MLA_V), lambda b, hp, qi: (b * nq + qi, hp)),
        out_shape=jax.ShapeDtypeStruct((n_tok, MLA_HEADS * MLA_V), BF16),
        scratch_shapes=[pltpu.VMEM((2, seq, ATT_DK), BF16)],
        compiler_params=_params("arbitrary", "arbitrary", "arbitrary"),
        name="mla_attention",
    )(q, tab, kv, kv, kv, kv, k_rope)


LNR_TM = 256
SLOT_LANES = 128


def _pack_halves(x):
    w = x.shape[1] // 2
    bits = lax.bitcast_convert_type(x.astype(BF16).astype(F32), U32)
    return (bits[:, :w] >> 16) | (bits[:, w:] & HI_MASK)


def _unpack_halves(p):
    return (lax.bitcast_convert_type(p << 16, F32), lax.bitcast_convert_type(p & HI_MASK, F32))


def _ln_router_kernel(h_ref, mix_ref, g_ref, b_ref, wr_ref, br_ref,
                      ho_ref, hq_ref, idx_ref, gate_ref, rank_ref, cnt_ref, carry_ref):
    @pl.when(pl.program_id(0) == 0)
    def _():
        carry_ref[...] = jnp.zeros_like(carry_ref)

    x = DEEPNORM_ALPHA * h_ref[...] + mix_ref[...]
    mu = jnp.mean(x, axis=-1, keepdims=True)
    xc = x - mu
    var = jnp.mean(xc * xc, axis=-1, keepdims=True)
    hn = xc * lax.rsqrt(var + LN_EPS) * g_ref[...] + b_ref[...]
    ho_ref[...] = hn
    hq_ref[...] = _pack_halves(hn)

    logits =jnp.dot(hn, wr_ref[...], precision=lax.Precision.HIGHEST,
                     preferred_element_type=F32) + br_ref[...]
    tm = logits.shape[0]
    lane_e = lax.broadcasted_iota(jnp.int32, (tm, N_EXPERTS), 1)
    lane_s = lax.broadcasted_iota(jnp.int32, (tm, SLOT_LANES), 1)
    work = logits
    sel = jnp.zeros((tm, N_EXPERTS), F32)
    top_v, top_i, hot = [], [], []
    for _ in range(TOP_K):
        mx = jnp.max(work, axis=-1, keepdims=True)
        ix = jnp.min(jnp.where(work == mx, lane_e, N_EXPERTS), axis=-1, keepdims=True)
        oh = lane_e == ix
        work = jnp.where(oh, -jnp.inf, work)
        sel = sel + oh.astype(F32)
        top_v.append(mx)
        top_i.append(ix)
        hot.append(oh)
    ex = [jnp.exp(v - top_v[0]) for v in top_v]
    den = ex[0] + ex[1] + ex[2] + ex[3]

    r = lax.broadcasted_iota(jnp.int32, (tm, tm), 0)
    c = lax.broadcasted_iota(jnp.int32, (tm, tm), 1)
    strict = jnp.where(c < r, 1.0, 0.0).astype(BF16)
    prefix = _dot(strict, sel.astype(BF16)) + carry_ref[0:1, :]
    carry_ref[0:1, :] = carry_ref[0:1, :] + jnp.sum(sel, axis=0, keepdims=True)
    cnt_ref[...] = jnp.broadcast_to(carry_ref[0:1, :], cnt_ref.shape)

    idx_o = jnp.zeros((tm, SLOT_LANES), jnp.int32)
    gate_o = jnp.zeros((tm, SLOT_LANES), F32)
    rank_o = jnp.zeros((tm, SLOT_LANES), jnp.int32)
    for k in range(TOP_K):
        rk = jnp.sum(jnp.where(hot[k], prefix, 0.0), axis=-1, keepdims=True)
        idx_o = jnp.where(lane_s == k, top_i[k], idx_o)
        gate_o = jnp.where(lane_s == k, ex[k] / den, gate_o)
        rank_o = jnp.where(lane_s == k, rk.astype(jnp.int32), rank_o)
    idx_ref[...] = idx_o
    gate_ref[...] = gate_o
    rank_ref[...] = rank_o


def _ln_router(h, mix, ln_g, ln_b, w_router, b_router):
    n_tok, d = h.shape
    tm = LNR_TM
    row = lambda i: (i, 0)
    fixed = lambda i: (0, 0)
    return pl.pallas_call(
        _ln_router_kernel,
        grid=(n_tok // tm,),
        in_specs=[
            pl.BlockSpec((tm, d), row), pl.BlockSpec((tm, d), row),
            pl.BlockSpec((1, d), fixed), pl.BlockSpec((1, d), fixed),
            pl.BlockSpec((d, N_EXPERTS), fixed), pl.BlockSpec((1, N_EXPERTS), fixed),
        ],
        out_specs=[
            pl.BlockSpec((tm, d), row), pl.BlockSpec((tm, d // 2), row),
            pl.BlockSpec((tm, SLOT_LANES), row), pl.BlockSpec((tm, SLOT_LANES), row),
            pl.BlockSpec((tm, SLOT_LANES), row), pl.BlockSpec((8, N_EXPERTS), fixed),
        ],
        out_shape=[
            jax.ShapeDtypeStruct((n_tok, d), F32), jax.ShapeDtypeStruct((n_tok, d // 2), U32),
            jax.ShapeDtypeStruct((n_tok, SLOT_LANES), jnp.int32),
            jax.ShapeDtypeStruct((n_tok, SLOT_LANES), F32),
            jax.ShapeDtypeStruct((n_tok, SLOT_LANES), jnp.int32),
            jax.ShapeDtypeStruct((8, N_EXPERTS), F32),
        ],
        scratch_shapes=[pltpu.VMEM((8, N_EXPERTS), F32)],
        compiler_params=_params("arbitrary"),
        name="ln_router",
    )(h, mix, ln_g.reshape(1, d), ln_b.reshape(1, d), w_router, b_router.reshape(1, N_EXPERTS))


MOE_TM = 1280
MOE_SB = 256
MOE_NSUB = MOE_TM // MOE_SB
MOE_TH = 128
MOE_TN = 512
MOE_T1 = D_EXPERT // MOE_TH
MOE_T2 = D_MODEL // 2 // MOE_TN
MOE_UNIT = 32
MOE_ISUB = 4
MOE_R1 = 32
MOE_R2 = 16
assert MOE_ISUB * (MOE_T1 * MOE_R1 + MOE_T2 * MOE_R2) == MOE_TM
MOE_KD = 2 * MOE_TH


def _moe_kernel(e_ref, n_ref, prow_ref, nv_ref, ids_cur, ids_nxt, xq_hbm, wg_ref, wl_ref, bg_ref, bl_ref,
                wdl_ref, wdh_ref, bdl_ref, bdh_ref, y_hbm, xbuf, act_ref, wcat_ref, obuf, xsem, osem):
    i = pl.program_id(0)
    t = pl.program_id(1)
    n_i = n_ref[i]
    nsub_i = (n_i + MOE_SB - 1) // MOE_SB
    slot = i % 2
    half = D_MODEL // 2

    def issue_rows(ids_ref, slot_, base, count):
        for r in range(count):
            pltpu.make_async_copy(xq_hbm.at[pl.ds(ids_ref[0, base + r], 1)],
                                  xbuf.at[slot_, pl.ds(base + r, 1)], xsem.at[slot_]).start()

    def issue_gate_up(t1, s):
        issue_rows(ids_nxt, 1 - slot, (t1 * MOE_ISUB + s) * MOE_R1, MOE_R1)

    def issue_down(t2, s):
        issue_rows(ids_nxt, 1 - slot, MOE_T1 * MOE_ISUB * MOE_R1 + (t2 * MOE_ISUB + s) * MOE_R2, MOE_R2)

    def slot_rows(slot_):
        return pltpu.make_async_copy(xq_hbm.at[pl.ds(0, MOE_TM)], xbuf.at[slot_], xsem.at[slot_])

    @pl.when(jnp.logical_and(i == 0, t == 0))
    def _first_block_rows():
        def body(u, c):
            issue_rows(ids_cur, 0, u * MOE_UNIT, MOE_UNIT)
            return c
        lax.fori_loop(0, MOE_TM // MOE_UNIT, body, 0)

    @pl.when(jnp.logical_and(t == 0, n_i > 0))
    def _wait_rows():
        slot_rows(slot).wait()

    def sub_blocks(body):
        body(0)
        for s in range(1, MOE_NSUB):
            pl.when(s < nsub_i)(functools.partial(body, s))

    @pl.when(jnp.logical_and(t < MOE_T1, n_i > 0))
    def _gate_up():
        wcat_ref[:, 0:MOE_TH] = wg_ref[...].astype(BF16)
        wcat_ref[:, MOE_TH:] = wl_ref[...].astype(BF16)

        @sub_blocks
        def _(s):
                if s < MOE_ISUB:
                    issue_gate_up(t, s)
                rows = slice(s * MOE_SB, (s + 1) * MOE_SB)
                x_lo, x_hi = _unpack_halves(xbuf[slot, rows, :])
                hcat = (_dot(x_lo.astype(BF16), wcat_ref[0:half, :])
                        + _dot(x_hi.astype(BF16), wcat_ref[half:, :]))
                hg = jnp.minimum(hcat[:, :MOE_TH] + bg_ref[...], SWIGLU_LIMIT)
                hl = jnp.clip(hcat[:, MOE_TH:] + bl_ref[...], -SWIGLU_LIMIT, SWIGLU_LIMIT)
                a = (hg * jax.nn.sigmoid(SWIGLU_ALPHA * hg) * (hl + 1.0)).astype(BF16)

                @pl.when(t % 2 == 0)
                def _():
                    act_ref[t // 2, rows, 0:MOE_TH] = a

                @pl.when(t % 2 == 1)
                def _():
                    act_ref[t // 2, rows, MOE_TH:] = a

    @pl.when(jnp.logical_and(t >= MOE_T1, n_i > 0))
    def _down():
        t2 = t - MOE_T1
        oslot = t2 % 2
        wdl = wdl_ref[...].astype(BF16)
        wdh = wdh_ref[...].astype(BF16)
        col0 = pl.multiple_of(t2 * MOE_TN, MOE_TN)

        def result_copy(slot_, s):
            return pltpu.make_async_copy(
                obuf.at[slot_, pl.ds(s * MOE_SB, MOE_SB)],
                y_hbm.at[pl.ds(pl.multiple_of(prow_ref[i] + s * MOE_SB, MOE_SB), MOE_SB),
                         pl.ds(col0, MOE_TN)], osem.at[slot_])

        @sub_blocks
        def _(s):
                if s < MOE_ISUB:
                    issue_down(t2, s)
                rows = slice(s * MOE_SB, (s + 1) * MOE_SB)
                y_lo = bdl_ref[...]
                y_hi = bdh_ref[...]
                for k in range(D_EXPERT // MOE_KD):
                    a = act_ref[k, rows, :]
                    y_lo = y_lo + _dot(a, wdl[k * MOE_KD:(k + 1) * MOE_KD, :])
                    y_hi = y_hi + _dot(a, wdh[k * MOE_KD:(k + 1) * MOE_KD, :])
                obuf[oslot, rows, :] = _pack_halves(jnp.concatenate([y_lo, y_hi], axis=1))
                result_copy(oslot, s).start()

        for s in range(MOE_NSUB):
            @pl.when(jnp.logical_and(s < nsub_i, t2 >= 1))
            def _():
                result_copy(1 - oslot, s).wait()

            @pl.when(jnp.logical_and(s < nsub_i, t2 == MOE_T2 - 1))
            def _():
                result_copy(oslot, s).wait()

    @pl.when(jnp.logical_and(t == MOE_T1 + MOE_T2 - 1, n_i > 0))
    def _missing_sub_block_shares():
        for s in range(1, MOE_ISUB):
            @pl.when(s >= nsub_i)
            def _():
                def gate_up_share(t1, c):
                    issue_gate_up(t1, s)
                    return c

                def down_share(t2, c):
                    issue_down(t2, s)
                    return c
                lax.fori_loop(0, MOE_T1, gate_up_share, 0)
                lax.fori_loop(0, MOE_T2, down_share, 0)

    @pl.when(jnp.logical_and(i == nv_ref[0] - 1, t == MOE_T1 + MOE_T2 - 1))
    def _zero_tail():
        zslot = 1 - slot
        slot_rows(zslot).wait()
        xbuf[zslot, 0:MOE_SB, :] = jnp.zeros((MOE_SB, half), U32)

        def tail_copy(j):
            return pltpu.make_async_copy(
                xbuf.at[zslot, pl.ds(0, MOE_SB)],
                y_hbm.at[pl.ds(pl.multiple_of(j * MOE_SB, MOE_SB), MOE_SB)], xsem.at[zslot])

        def start(j, c):
            tail_copy(j).start()
            return c

        def wait(j, c):
            tail_copy(j).wait()
            return c

        first = nv_ref[1] // MOE_SB
        total = y_hbm.shape[0] // MOE_SB
        lax.fori_loop(first, total, start, 0)
        lax.fori_loop(first, total, wait, 0)


def _moe_experts(xq, blk_exp, blk_n, blk_prow, n_valid_blocks, ids, w_gu, b_gu, w_down, b_down, layer,
                 n_rows_out):
    nb = ids.shape[0]
    t_last = MOE_T1 + MOE_T2 - 1
    half = D_MODEL // 2
    hi_blk = half // MOE_TN

    def src(i, nv):
        return jnp.maximum(jnp.minimum(i, nv[0] - 1), 0)

    def tt(i, t, nv):
        return jnp.where(i < nv[0], t, t_last)

    def t1(i, t, nv):
        return jnp.minimum(tt(i, t, nv), MOE_T1 - 1)

    def t2(i, t, nv):
        return jnp.maximum(tt(i, t, nv) - MOE_T1, 0)

    ids_cur_map = lambda i, t, e, n, p, nv: (src(i, nv), 0, 0)
    ids_nxt_map = lambda i, t, e, n, p, nv: (jnp.minimum(src(i, nv) + 1, nb - 1), 0, 0)
    wg_map = lambda i, t, e, n, p, nv: (layer, e[i], 0, t1(i, t, nv))
    wl_map = lambda i, t, e, n, p, nv: (layer, e[i], 0, MOE_T1 + t1(i, t, nv))
    wdl_map = lambda i, t, e, n, p, nv: (layer, e[i], 0, t2(i, t, nv))
    wdh_map = lambda i, t, e, n, p, nv: (layer, e[i], 0, hi_blk + t2(i, t, nv))
    grid_spec = pltpu.PrefetchScalarGridSpec(
        num_scalar_prefetch=4,
        grid=(nb, MOE_T1 + MOE_T2),
        in_specs=[
            pl.BlockSpec((None, 1, MOE_TM), ids_cur_map, memory_space=pltpu.SMEM),
            pl.BlockSpec((None, 1, MOE_TM), ids_nxt_map, memory_space=pltpu.SMEM),
            pl.BlockSpec(memory_space=pl.ANY),
            pl.BlockSpec((None, None, D_MODEL, MOE_TH), wg_map),
            pl.BlockSpec((None, None, D_MODEL, MOE_TH), wl_map),
            pl.BlockSpec((None, None, 1, MOE_TH), wg_map),
            pl.BlockSpec((None, None, 1, MOE_TH), wl_map),
            pl.BlockSpec((None, None, D_EXPERT, MOE_TN), wdl_map),
            pl.BlockSpec((None, None, D_EXPERT, MOE_TN), wdh_map),
            pl.BlockSpec((None, None, 1, MOE_TN), wdl_map),
            pl.BlockSpec((None, None, 1, MOE_TN), wdh_map),
        ],
        out_specs=pl.BlockSpec(memory_space=pl.ANY),
        scratch_shapes=[
            pltpu.VMEM((2, MOE_TM, half), U32),
            pltpu.VMEM((D_EXPERT // MOE_KD, MOE_TM, MOE_KD), BF16),
            pltpu.VMEM((D_MODEL, 2 * MOE_TH), BF16),
            pltpu.VMEM((2, MOE_TM, MOE_TN), U32),
            pltpu.SemaphoreType.DMA((2,)),
            pltpu.SemaphoreType.DMA((2,)),
        ],
    )
    bgu = b_gu.reshape(DEPTH, N_EXPERTS, 1, 2 * D_EXPERT)
    ids3 = ids.reshape(nb, 1, MOE_TM)
    bdn = b_down.reshape(DEPTH, N_EXPERTS, 1, D_MODEL)
    return pl.pallas_call(
        _moe_kernel,
        grid_spec=grid_spec,
        out_shape=jax.ShapeDtypeStruct((n_rows_out, half), U32),
        compiler_params=_params("arbitrary", "arbitrary"),
        name="moe_experts",
    )(blk_exp, blk_n, blk_prow, n_valid_blocks, ids3, ids3, xq, w_gu, w_gu, bgu, bgu,
      w_down, w_down, bdn, bdn)


LNC_TM = 128


def _ln_combine_kernel(ids_cur, ids_nxt, h_ref, gate_ref, g_ref, b_ref, y_hbm, ho_ref, hb_ref, ybuf, sem):
    i = pl.program_id(0)
    tm = h_ref.shape[0]
    w = h_ref.shape[1] // 2
    rows = TOP_K * tm
    slot = i % 2

    def row_copy(ids_ref, slot_, idx):
        return pltpu.make_async_copy(y_hbm.at[pl.ds(ids_ref[0, idx], 1)],
                                     ybuf.at[slot_, pl.ds(idx, 1)], sem.at[slot_])

    def tile_rows(slot_):
        return pltpu.make_async_copy(y_hbm.at[pl.ds(0, rows)], ybuf.at[slot_], sem.at[slot_])

    @pl.when(i == 0)
    def _():
        def body(u, c):
            for r in range(MOE_UNIT):
                row_copy(ids_cur, 0, u * MOE_UNIT + r).start()
            return c
        lax.fori_loop(0, rows // MOE_UNIT, body, 0)

    tile_rows(slot).wait()
    for idx in range(rows):
        row_copy(ids_nxt, 1 - slot, idx).start()

    x_lo = DEEPNORM_ALPHA * h_ref[:, :w]
    x_hi = DEEPNORM_ALPHA * h_ref[:, w:]
    for k in range(TOP_K):
        lo, hi = _unpack_halves(ybuf[slot, k * tm:(k + 1) * tm, :])
        gk = gate_ref[:, k:k + 1]
        x_lo = x_lo + gk * lo
        x_hi = x_hi + gk * hi
    x = jnp.concatenate([x_lo, x_hi], axis=1)
    mu = jnp.mean(x, axis=-1, keepdims=True)
    xc = x - mu
    var = jnp.mean(xc * xc, axis=-1, keepdims=True)
    hn = xc * lax.rsqrt(var + LN_EPS) * g_ref[...] + b_ref[...]
    ho_ref[...] = hn
    hb_ref[...] = hn.astype(BF16)

    @pl.when(i == pl.num_programs(0) - 1)
    def _():
        tile_rows(1 - slot).wait()


def _ln_combine(h, yq, dest_tiles, gates, ln_g, ln_b):
    n_tok, d = h.shape
    tm = LNC_TM
    nblk = n_tok // tm
    row = lambda i: (i, 0)
    fixed = lambda i: (0, 0)
    dest3 = dest_tiles.reshape(nblk, 1, TOP_K * tm)
    return pl.pallas_call(
        _ln_combine_kernel,
        grid=(nblk,),
        in_specs=[
            pl.BlockSpec((None, 1, TOP_K * tm), lambda i: (i, 0, 0), memory_space=pltpu.SMEM),
            pl.BlockSpec((None, 1, TOP_K * tm), lambda i: (jnp.minimum(i + 1, nblk - 1), 0, 0),
                         memory_space=pltpu.SMEM),
            pl.BlockSpec((tm, d), row),
            pl.BlockSpec((tm, SLOT_LANES), row),
            pl.BlockSpec((1, d), fixed), pl.BlockSpec((1, d), fixed),
            pl.BlockSpec(memory_space=pl.ANY),
        ],
        out_specs=[pl.BlockSpec((tm, d), row), pl.BlockSpec((tm, d), row)],
        out_shape=[jax.ShapeDtypeStruct((n_tok, d), F32), jax.ShapeDtypeStruct((n_tok, d), BF16)],
        scratch_shapes=[pltpu.VMEM((2, TOP_K * tm, d // 2), U32), pltpu.SemaphoreType.DMA((2,))],
        compiler_params=_params("arbitrary"),
        name="ln_combine",
    )(dest3, dest3, h, gates, ln_g.reshape(1, d), ln_b.reshape(1, d), yq)


def _moe_layer(h, hq, top_idx, gates, rank, counts, w_gu, b_gu, w_down, b_down, layer, ln_g, ln_b):
    n_tok = h.shape[0]
    n_assign = n_tok * TOP_K
    nb = N_EXPERTS + n_assign // MOE_TM
    n_rows_out = n_assign + N_EXPERTS * MOE_SB
    i32 = jnp.int32
    cnt = counts[0].astype(i32)
    pcnt = (cnt + MOE_SB - 1) // MOE_SB * MOE_SB
    pstart = jnp.cumsum(pcnt) - pcnt
    nblk = (cnt + MOE_TM - 1) // MOE_TM
    bend = jnp.cumsum(nblk)
    bfirst = bend - nblk
    n_valid = bend[-1]
    e_flat = top_idx[:, :TOP_K].reshape(-1)
    rank_flat = rank[:, :TOP_K].reshape(-1)
    tok = jnp.arange(n_assign, dtype=i32) // TOP_K
    dest = pstart[e_flat] + rank_flat
    blk = jnp.arange(nb, dtype=i32)
    src = jnp.maximum(jnp.minimum(blk, n_valid - 1), 0)
    be = jnp.minimum(jnp.sum((bend[None, :] <= src[:, None]).astype(i32), axis=1), N_EXPERTS - 1)
    b_in = src - bfirst[be]
    blk_n = jnp.where(blk < n_valid, jnp.clip(cnt[be] - b_in * MOE_TM, 0, MOE_TM), 0).astype(i32)
    blk_prow = (pstart[be] + b_in * MOE_TM).astype(i32)
    id_pos = (bfirst[e_flat] + rank_flat // MOE_TM) * MOE_TM + rank_flat % MOE_TM
    ids = jnp.zeros((nb * MOE_TM,), i32).at[id_pos].set(tok).reshape(nb, MOE_TM)
    used_rows = pstart[-1] + pcnt[-1]
    yq = _moe_experts(hq, be.astype(i32), blk_n, blk_prow, jnp.stack([n_valid, used_rows]).astype(i32), ids,
                      w_gu, b_gu, w_down, b_down, layer, n_rows_out)
    tm = LNC_TM
    dest_tiles = dest.reshape(n_tok // tm, tm, TOP_K).transpose(0, 2, 1).reshape(n_tok // tm, TOP_K * tm)
    return _ln_combine(h, yq, dest_tiles, gates, ln_g, ln_b)


def kernel(x, positions, ln_g, ln_b, hy_w_in, hy_w_out, gm_ln_g, gm_ln_b, gm_w_s, gm_b_s, ssm_lam_re, ssm_lam_im, ssm_log_dt, ssm_b_re, ssm_b_im, ssm_c_re, ssm_c_im, ssm_d, ssm_w_glu, ssm_b_glu, mla_w_in, mla_q_norm_g, mla_kv_norm_g, mla_w_uq, mla_w_ukv, mla_w_o, moe_w_router, moe_b_router, moe_w_gu, moe_b_gu, moe_w_down, moe_b_down):
    batch, seq, d = x.shape
    n_tok = batch * seq
    h = x.reshape(n_tok, d)
    hb = h.astype(BF16)
    for layer in range(DEPTH):
        i = layer // 2
        if layer % 2 == 0:
            z_gm = _mm([hb], hy_w_in[i], col0=0, n_cols=2 * GM_WIDTH, tm=1024, tn=256,
                       out_dtype=BF16, epilogue="gelu", name="hy_in_gm")
            z_ssm = _mm([hb], hy_w_in[i], col0=2 * GM_WIDTH, n_cols=SSM_WIDTH, tm=1024, tn=256,
                        out_dtype=F32, name="hy_in_ssm")
            y_gm = _spatial_gating(z_gm, gm_ln_g[i], gm_ln_b[i], gm_w_s[i], gm_b_s[i])
            tables = _s5_tables(ssm_lam_re[i], ssm_lam_im[i], ssm_log_dt[i], ssm_b_re[i], ssm_b_im[i],
                                ssm_c_re[i], ssm_c_im[i], ssm_d[i], seq)
            y_act = _s5_mixer(z_ssm, batch, seq, tables)
            y_ssm = _mm([y_act], ssm_w_glu[i], tm=1024, tn=512, out_dtype=BF16, epilogue="glu",
                        bias=ssm_b_glu[i], mul=y_act, name="s5_glu")
            mix = _mm([y_gm, y_ssm], hy_w_out[i], tm=1024, tn=256, out_dtype=F32, name="hy_out")
        else:
            w_in = mla_w_in[i]
            n_main = MLA_Q_RANK + MLA_KV_RANK
            c_main = _mm([hb], w_in, col0=0, n_cols=n_main, tm=1024, tn=256, out_dtype=F32,
                         name="mla_in")
            k_rope = _mm([hb], w_in[:, n_main:], tm=1024, tn=MLA_ROPE, out_dtype=F32, name="mla_in_rope")
            tab, k_rope = _rope_tables(positions, k_rope)
            q = _mm([(c_main, MLA_Q_RANK, 0)], mla_w_uq[i], tm=1024, tn=768, out_dtype=BF16,
                    prologue="rms", gain=mla_q_norm_g[i], name="mla_uq")
            kv = _mm([(c_main, MLA_KV_RANK, MLA_Q_RANK // MLA_KV_RANK)], mla_w_ukv[i], tm=1024, tn=1024,
                     out_dtype=BF16, prologue="rms", gain=mla_kv_norm_g[i], name="mla_ukv")
            o = _attention(q, kv, k_rope, tab, batch, seq)
            mix = _mm([o], mla_w_o[i], tm=1024, tn=256, out_dtype=F32, name="mla_out")
        h, hq, top_idx, gates, rank, counts = _ln_router(
            h, mix, ln_g[layer, 0], ln_b[layer, 0], moe_w_router[layer], moe_b_router[layer])
        h, hb = _moe_layer(h, hq, top_idx, gates, rank, counts, moe_w_gu, moe_b_gu,
                           moe_w_down, moe_b_down, layer, ln_g[layer, 1], ln_b[layer, 1])
    return h.reshape(batch, seq, d)
```

```python
import functools
import math

import numpy as np
import jax
import jax.numpy as jnp
from jax import lax
from jax.experimental import pallas as pl
from jax.experimental.pallas import tpu as pltpu

F32 = jnp.float32
BF16 = jnp.bfloat16
U32 = jnp.uint32
HI_MASK = np.uint32(0xFFFF0000)

D_MODEL = 4096
DEPTH = 2
CHUNK = 64
DEEPNORM_ALPHA = (2 * DEPTH) ** 0.25
LN_EPS = 1e-5
RMS_EPS = 1e-6
LANES = 128

GM_WIDTH = 2048
GM_GROUPS = 8
GM_GROUP_DIM = 256
GM_BLOCK = 128
SSM_WIDTH = 2048
SSM_P = 16
SSM_GROUPS = 128
SSM_N = 64
SSM_T = 16
SSM_OCT = LANES // SSM_P
SSM_NOCT = SSM_GROUPS // SSM_OCT
SSM_PAIRS = SSM_OCT // 2
SSM_SW = SSM_OCT * SSM_N

MLA_HEADS = 32
MLA_Q_RANK = 1024
MLA_KV_RANK = 512
MLA_NOPE = 128
MLA_ROPE = 64
MLA_V = 128
ROPE_THETA = 10000.0

N_EXPERTS = 32
TOP_K = 4
D_EXPERT = 1024
SWIGLU_LIMIT = 7.0
SWIGLU_ALPHA = 1.702

VMEM_LIMIT_BYTES = 56 * 1024 * 1024


def _params(*sem):
    return pltpu.CompilerParams(dimension_semantics=sem, vmem_limit_bytes=VMEM_LIMIT_BYTES)


def _dot(a, b):
    return jnp.dot(a, b, preferred_element_type=F32)


def _dot_nt(a, b):
    return lax.dot_general(a, b, (((1,), (1,)), ((), ())), preferred_element_type=F32)


def _mm_kernel(*refs, prologue, epilogue, n_x):
    it = iter(refs)
    x_refs = [next(it) for _ in range(n_x)]
    w_ref = next(it)
    g_ref = next(it) if prologue == "rms" else None
    b_ref = next(it) if epilogue == "glu" else None
    y_ref = next(it) if epilogue == "glu" else None
    o_ref = next(it)
    xs_ref = next(it) if prologue == "rms" else None

    if prologue == "rms":
        @pl.when(pl.program_id(1) == 0)
        def _():
            xf = x_refs[0][...].astype(F32)
            ms = jnp.mean(xf * xf, axis=-1, keepdims=True)
            xs_ref[...] = (xf * lax.rsqrt(ms + RMS_EPS) * g_ref[...]).astype(BF16)
        xs = [xs_ref[...]]
    else:
        xs = [r[...].astype(BF16) for r in x_refs]

    acc = None
    k0 = 0
    for x in xs:
        kk = x.shape[1]
        part = _dot(x, w_ref[k0:k0 + kk, :].astype(BF16))
        acc = part if acc is None else acc + part
        k0 += kk
    if epilogue == "gelu":
        acc = jax.nn.gelu(acc)
    elif epilogue == "glu":
        acc = y_ref[...].astype(F32) * jax.nn.sigmoid(acc + b_ref[...])
    o_ref[...] = acc.astype(o_ref.dtype)


def _mm(xs, w, *, col0=0, n_cols=None, tm, tn, out_dtype, prologue=None, gain=None,
        epilogue=None, bias=None, mul=None, name):
    xs = [(x, x.shape[1], 0) if not isinstance(x, tuple) else x for x in xs]
    m = xs[0][0].shape[0]
    k_total = w.shape[0]
    n_cols = w.shape[1] - col0 if n_cols is None else n_cols
    tn = min(tn, n_cols)
    assert m % tm == 0 and n_cols % tn == 0 and col0 % tn == 0
    cb0 = col0 // tn
    assert sum(kw for _, kw, _ in xs) == k_total
    in_specs = [pl.BlockSpec((tm, kw), functools.partial(lambda i, j, cb: (i, cb), cb=cb))
                for _, kw, cb in xs]
    in_specs.append(pl.BlockSpec((k_total, tn), lambda i, j: (0, cb0 + j)))
    args = [x for x, _, _ in xs] + [w]
    scratch = []
    if prologue == "rms":
        in_specs.append(pl.BlockSpec((1, k_total), lambda i, j: (0, 0)))
        args.append(gain.reshape(1, k_total))
        scratch.append(pltpu.VMEM((tm, k_total), BF16))
    if epilogue == "glu":
        in_specs.append(pl.BlockSpec((1, tn), lambda i, j: (0, j)))
        in_specs.append(pl.BlockSpec((tm, tn), lambda i, j: (i, j)))
        args += [bias.reshape(1, n_cols), mul]
    return pl.pallas_call(
        functools.partial(_mm_kernel, prologue=prologue, epilogue=epilogue, n_x=len(xs)),
        grid=(m // tm, n_cols // tn),
        in_specs=in_specs,
        out_specs=pl.BlockSpec((tm, tn), lambda i, j: (i, j)),
        out_shape=jax.ShapeDtypeStruct((m, n_cols), out_dtype),
        scratch_shapes=scratch,
        compiler_params=_params("arbitrary", "arbitrary"),
        name=name,
    )(*args)


def _gating_kernel(u_ref, v_ref, g_ref, b_ref, ws_ref, bst_ref, o_ref):
    row = lax.broadcasted_iota(jnp.int32, (GM_BLOCK, GM_BLOCK), 0)
    col = lax.broadcasted_iota(jnp.int32, (GM_BLOCK, GM_BLOCK), 1)
    visible = (col // CHUNK) <= (row // CHUNK)
    for g in range(GM_GROUPS):
        sl = slice(g * GM_GROUP_DIM, (g + 1) * GM_GROUP_DIM)
        v = v_ref[:, sl].astype(F32)
        mu = jnp.mean(v, axis=-1, keepdims=True)
        vc = v - mu
        var = jnp.mean(vc * vc, axis=-1, keepdims=True)
        vn = vc * lax.rsqrt(var + LN_EPS) * g_ref[:, sl] + b_ref[:, sl]
        w = jnp.where(visible, ws_ref[g], 0.0).astype(BF16)
        s = _dot(w, vn.astype(BF16)) + bst_ref[:, g:g + 1]
        o_ref[:, sl] = (u_ref[:, sl].astype(F32) * s).astype(o_ref.dtype)


def _spatial_gating(z_gm, ln_g, ln_b, w_s, b_s):
    n_tok = z_gm.shape[0]
    return pl.pallas_call(
        _gating_kernel,
        grid=(n_tok // GM_BLOCK,),
        in_specs=[
            pl.BlockSpec((GM_BLOCK, GM_WIDTH), lambda i: (i, 0)),
            pl.BlockSpec((GM_BLOCK, GM_WIDTH), lambda i: (i, 1)),
            pl.BlockSpec((1, GM_WIDTH), lambda i: (0, 0)),
            pl.BlockSpec((1, GM_WIDTH), lambda i: (0, 0)),
            pl.BlockSpec((GM_GROUPS, GM_BLOCK, GM_BLOCK), lambda i: (0, 0, 0)),
            pl.BlockSpec((GM_BLOCK, GM_GROUPS), lambda i: (0, 0)),
        ],
        out_specs=pl.BlockSpec((GM_BLOCK, GM_WIDTH), lambda i: (i, 0)),
        out_shape=jax.ShapeDtypeStruct((n_tok, GM_WIDTH), BF16),
        compiler_params=_params("arbitrary"),
        name="spatial_gating",
    )(z_gm, z_gm, ln_g.reshape(1, GM_WIDTH), ln_b.reshape(1, GM_WIDTH), w_s, b_s.T)


def _s5_tables(lam_re, lam_im, log_dt, b_re, b_im, c_re, c_im, d_skip, seq):
    hi = lax.Precision.HIGHEST
    n, p, t = SSM_N, SSM_P, SSM_T
    no, npr = SSM_NOCT, SSM_PAIRS
    dt = jnp.exp(log_dt)[:, None]
    mag = jnp.exp(lam_re * dt)
    ab_re = mag * jnp.cos(lam_im * dt)
    ab_im = mag * jnp.sin(lam_im * dt)
    den = lam_re * lam_re + lam_im * lam_im
    num_re = ab_re - 1.0
    coef_re = (num_re * lam_re + ab_im * lam_im) / den
    coef_im = (ab_im * lam_re - num_re * lam_im) / den
    bb_re = coef_re[..., None] * b_re - coef_im[..., None] * b_im
    bb_im = coef_re[..., None] * b_im + coef_im[..., None] * b_re

    def power(k):
        k = jnp.asarray(k, F32)[..., None, None]
        mk = jnp.exp(k * (lam_re * dt))
        return mk * jnp.cos(k * (lam_im * dt)), mk * jnp.sin(k * (lam_im * dt))

    pw_re, pw_im = power(np.arange(t + 1))
    cp_re = c_re[None] * pw_re[:, :, None, :] - c_im[None] * pw_im[:, :, None, :]
    cp_im = c_re[None] * pw_im[:, :, None, :] + c_im[None] * pw_re[:, :, None, :]
    kern = (jnp.einsum('tgpn,gnq->gtpq', cp_re[:t], bb_re, precision=hi)
            - jnp.einsum('tgpn,gnq->gtpq', cp_im[:t], bb_im, precision=hi))
    eye_o = jnp.eye(SSM_OCT, dtype=F32)
    eye_2 = jnp.eye(2, dtype=F32)
    kbd = jnp.einsum('oatpq,ab->otaqbp', kern.reshape(no, SSM_OCT, t, p, p), eye_o)
    kbd = kbd.reshape(no, t, LANES, LANES)
    rv_re, rv_im = pw_re[t - 1::-1], pw_im[t - 1::-1]
    st_re = rv_re[..., None] * bb_re[None] - rv_im[..., None] * bb_im[None]
    st_im = rv_re[..., None] * bb_im[None] + rv_im[..., None] * bb_re[None]
    st = jnp.stack([st_re, st_im], 0).reshape(2, t, no, npr, 2, n, p)
    w_st = jnp.einsum('rjoxanq,ab->ojxraqbn', st, eye_2).reshape(no, t, npr, 2, 2 * p, 2 * n)
    wc = jnp.stack([cp_re[1:], -cp_im[1:]], 0).reshape(2, t, no, npr, 2, p, n)
    w_ct = jnp.einsum('rioxapn,ab->oixrapbn', wc, eye_2).reshape(no, t, npr, 2, 2 * p, 2 * n)
    n_lvl = int(math.log2(seq // t))
    lv_re, lv_im = power(t * 2 ** np.arange(n_lvl))
    lr = lv_re.reshape(n_lvl, no, SSM_SW)
    li = lv_im.reshape(n_lvl, no, SSM_SW)
    lvl = jnp.stack([jnp.concatenate([lr, lr], -1), jnp.concatenate([-li, li], -1)], axis=1)
    lvl = lvl.transpose(2, 0, 1, 3).reshape(no, 2 * n_lvl, 2 * SSM_SW)
    d_vec = jnp.tile(d_skip.reshape(no, 1, LANES), (1, 1, t))
    return kbd.astype(BF16), w_st.astype(BF16), w_ct.astype(BF16), lvl, d_vec


def _s5_kernel(u_ref, kbd_ref, wst_ref, wct_ref, lvl_ref, d_ref, o_ref, toe_ref, st_ref, ct_ref,
               *, n_chunks, n_lvl):
    t = SSM_T

    @pl.when(pl.program_id(0) == 0)
    def _():
        toe_ref[...] = jnp.zeros_like(toe_ref)
        st_ref[...] = jnp.zeros_like(st_ref)
        ct_ref[...] = jnp.zeros_like(ct_ref)

    for i in range(t):
        for j in range(i + 1):
            toe_ref[j * LANES:(j + 1) * LANES, i * LANES:(i + 1) * LANES] = kbd_ref[i - j]
    for j in range(t):
        for x in range(SSM_PAIRS):
            for r in range(2):
                rows = slice(j * LANES + x * 2 * SSM_P, j * LANES + (x + 1) * 2 * SSM_P)
                lanes = slice(r * SSM_SW + x * LANES, r * SSM_SW + (x + 1) * LANES)
                st_ref[rows, lanes] = wst_ref[j, x, r]
                ct_ref[rows, lanes] = wct_ref[j, x, r]

    u = u_ref[...]
    ub = u.astype(BF16)
    h = _dot(ub, st_ref[...])
    cidx = lax.broadcasted_iota(jnp.int32, (u.shape[0], 1), 0) % n_chunks
    for lv in range(n_lvl):
        d = 1 << lv
        a_rr = lvl_ref[2 * lv:2 * lv + 1, :]
        a_is = lvl_ref[2 * lv + 1:2 * lv + 2, :]
        sh = jnp.where(cidx >= d, pltpu.roll(h, d, 0), 0.0)
        h = h + sh * a_rr + pltpu.roll(sh, SSM_SW, 1) * a_is
    h_prev = jnp.where(cidx >= 1, pltpu.roll(h, 1, 0), 0.0)
    y = _dot(ub, toe_ref[...]) + _dot_nt(h_prev.astype(BF16), ct_ref[...]) + d_ref[...] * u
    o_ref[...] = jax.nn.gelu(y).astype(o_ref.dtype)


def _s5_mixer(z_ssm, batch, seq, tables):
    kbd, w_st, w_ct, lvl, d_vec = tables
    t, no = SSM_T, SSM_NOCT
    n_chunks = seq // t
    rows = batch * n_chunks
    width = t * LANES
    n_lvl = lvl.shape[1] // 2
    u = z_ssm.reshape(rows, t, no, LANES).transpose(2, 0, 1, 3).reshape(no, rows, width)
    y = pl.pallas_call(
        functools.partial(_s5_kernel, n_chunks=n_chunks, n_lvl=n_lvl),
        grid=(no,),
        in_specs=[
            pl.BlockSpec((None, rows, width), lambda i: (i, 0, 0)),
            pl.BlockSpec((None, t, LANES, LANES), lambda i: (i, 0, 0, 0)),
            pl.BlockSpec((None, t, SSM_PAIRS, 2, 2 * SSM_P, LANES), lambda i: (i, 0, 0, 0, 0, 0)),
            pl.BlockSpec((None, t, SSM_PAIRS, 2, 2 * SSM_P, LANES), lambda i: (i, 0, 0, 0, 0, 0)),
            pl.BlockSpec((None, 2 * n_lvl, 2 * SSM_SW), lambda i: (i, 0, 0)),
            pl.BlockSpec((None, 1, width), lambda i: (i, 0, 0)),
        ],
        out_specs=pl.BlockSpec((None, rows, width), lambda i: (i, 0, 0)),
        out_shape=jax.ShapeDtypeStruct((no, rows, width), BF16),
        scratch_shapes=[
            pltpu.VMEM((width, width), BF16),
            pltpu.VMEM((width, 2 * SSM_SW), BF16),
            pltpu.VMEM((width, 2 * SSM_SW), BF16),
        ],
        compiler_params=_params("arbitrary"),
        name="s5_mixer",
    )(u, kbd, w_st, w_ct, lvl, d_vec)
    return y.reshape(no, rows, t, LANES).transpose(1, 2, 0, 3).reshape(batch * seq, SSM_WIDTH)


def _rope_kernel(pos_ref, freq_ref, kr_ref, tab_ref, kro_ref):
    ang = pos_ref[...].astype(F32) * freq_ref[...]
    lane = lax.broadcasted_iota(jnp.int32, ang.shape, 1)
    sin = jnp.sin(ang)
    tab = jnp.where(lane < MLA_ROPE, jnp.cos(ang), jnp.where(lane < MLA_ROPE + MLA_ROPE // 2, -sin, sin))
    tab_ref[...] = tab
    kr = kr_ref[...]
    half = MLA_ROPE // 2
    sw = jnp.concatenate([kr[:, half:], kr[:, :half]], axis=1)
    kro_ref[...] = (kr * tab[:, :MLA_ROPE] + sw * tab[:, MLA_ROPE:]).astype(kro_ref.dtype)


def _rope_tables(positions, k_rope):
    n_tok = k_rope.shape[0]
    tm = 512
    inv_freq = ROPE_THETA ** (-np.arange(0, MLA_ROPE, 2, dtype=np.float64) / MLA_ROPE)
    freq = jnp.asarray(np.tile(inv_freq, 4)[None, :], F32)
    return pl.pallas_call(
        _rope_kernel,
        grid=(n_tok // tm,),
        in_specs=[
            pl.BlockSpec((tm, 1), lambda i: (i, 0)),
            pl.BlockSpec((1, 2 * MLA_ROPE), lambda i: (0, 0)),
            pl.BlockSpec((tm, MLA_ROPE), lambda i: (i, 0)),
        ],
        out_specs=[
            pl.BlockSpec((tm, 2 * MLA_ROPE), lambda i: (i, 0)),
            pl.BlockSpec((tm, MLA_ROPE), lambda i: (i, 0)),
        ],
        out_shape=[
            jax.ShapeDtypeStruct((n_tok, 2 * MLA_ROPE), F32),
            jax.ShapeDtypeStruct((n_tok, MLA_ROPE), BF16),
        ],
        compiler_params=_params("arbitrary"),
        name="rope_tables",
    )(positions.reshape(n_tok, 1), freq, k_rope)


ATT_TK = 256
ATT_TQ = 2 * ATT_TK
ATT_DK = 2 * MLA_NOPE


def _attn_kernel(q_ref, tab_ref, k0_ref, v0_ref, k1_ref, v1_ref, kr_ref, o_ref, kc_ref):
    qi = pl.program_id(2)
    hd = MLA_NOPE + MLA_ROPE
    scale = hd ** -0.5 * math.log2(math.e)
    half = MLA_ROPE // 2
    tk = ATT_TK

    @pl.when(qi == 0)
    def _():
        for hh, k_ref in enumerate((k0_ref, k1_ref)):
            kc_ref[hh, :, 0:MLA_NOPE] = k_ref[...]
            kc_ref[hh, :, MLA_NOPE:hd] = kr_ref[...]
            kc_ref[hh, :, hd:] = jnp.zeros((kr_ref.shape[0], ATT_DK - hd), BF16)

    q = q_ref[...]
    cosf = tab_ref[:, :MLA_ROPE]
    sinf = tab_ref[:, MLA_ROPE:]
    qc = []
    for hh in range(2):
        qn = q[:, hh * hd:hh * hd + MLA_NOPE].astype(F32) * scale
        qr = q[:, hh * hd + MLA_NOPE:(hh + 1) * hd].astype(F32)
        sw = jnp.concatenate([qr[:, half:], qr[:, :half]], axis=1)
        qr = (qr * cosf + sw * sinf) * scale
        pad = jnp.zeros((ATT_TQ, ATT_DK - hd), F32)
        qc.append(jnp.concatenate([qn, qr, pad], axis=1).astype(BF16))
    v_refs = (v0_ref, v1_ref)
    row = lax.broadcasted_iota(jnp.int32, (tk, tk), 0)
    col = lax.broadcasted_iota(jnp.int32, (tk, tk), 1)
    visible = (col // CHUNK) <= (row // CHUNK)

    def update(state, hh, sub, j, masked):
        m, l, acc = state
        start = pl.multiple_of(j * tk, tk)
        ks = kc_ref[hh, pl.ds(start, tk), :]
        vs = v_refs[hh][pl.ds(start, tk), :]
        s = _dot_nt(qc[hh][sub * tk:(sub + 1) * tk], ks)
        if masked:
            s = jnp.where(visible, s, -1e30)
        m_new = jnp.maximum(m, jnp.max(s, axis=-1, keepdims=True))
        alpha = jnp.exp2(m - m_new)
        p = jnp.exp2(s - m_new)
        l = alpha * l + jnp.sum(p, axis=-1, keepdims=True)
        acc = alpha * acc + _dot(p.astype(BF16), vs)
        return m_new, l, acc

    chains = [(hh, sub) for hh in range(2) for sub in range(2)]

    def body(jj, states):
        states = tuple(update(st, hh, sub, 2 * jj, False) for st, (hh, sub) in zip(states, chains))
        return tuple(update(st, hh, sub, 2 * jj + 1, False) for st, (hh, sub) in zip(states, chains))

    init = tuple((jnp.full((tk, 1), -1e30, F32), jnp.zeros((tk, 1), F32), jnp.zeros((tk, MLA_V), F32))
                 for _ in chains)
    states = list(lax.fori_loop(0, qi, body, init))
    for c, (hh, sub) in enumerate(chains):
        st = update(states[c], hh, sub, 2 * qi, masked=(sub == 0))
        if sub == 1:
            st = update(st, hh, sub, 2 * qi + 1, masked=True)
        _, l, acc = st
        o_ref[sub * tk:(sub + 1) * tk, hh * MLA_V:(hh + 1) * MLA_V] = (acc / l).astype(o_ref.dtype)


def _attention(q, kv, k_rope, tab, batch, seq):
    n_tok = batch * seq
    nq = seq // ATT_TQ
    kvb = lambda off: pl.BlockSpec((seq, MLA_NOPE), lambda b, hp, qi: (b, 4 * hp + off))
    return pl.pallas_call(
        _attn_kernel,
        grid=(batch, MLA_HEADS // 2, nq),
        in_specs=[
            pl.BlockSpec((ATT_TQ, 2 * (MLA_NOPE + MLA_ROPE)), lambda b, hp, qi: (b * nq + qi, hp)),
            pl.BlockSpec((ATT_TQ, 2 * MLA_ROPE), lambda b, hp, qi: (b * nq + qi, 0)),
            kvb(0), kvb(1), kvb(2), kvb(3),
            pl.BlockSpec((seq, MLA_ROPE), lambda b, hp, qi: (b, 0)),
        ],
        out_specs=pl.BlockSpec((ATT_TQ, 2 * MLA_V), lambda b, hp, qi: (b * nq + qi, hp)),
        out_shape=jax.ShapeDtypeStruct((n_tok, MLA_HEADS * MLA_V), BF16),
        scratch_shapes=[pltpu.VMEM((2, seq, ATT_DK), BF16)],
        compiler_params=_params("arbitrary", "arbitrary", "arbitrary"),
        name="mla_attention",
    )(q, tab, kv, kv, kv, kv, k_rope)


LNR_TM = 256
SLOT_LANES = 128


def _pack_halves(x):
    w = x.shape[1] // 2
    bits = lax.bitcast_convert_type(x.astype(BF16).astype(F32), U32)
    return (bits[:, :w] >> 16) | (bits[:, w:] & HI_MASK)


def _unpack_halves(p):
    return (lax.bitcast_convert_type(p << 16, F32), lax.bitcast_convert_type(p & HI_MASK, F32))


def _ln_router_kernel(h_ref, mix_ref, g_ref, b_ref, wr_ref, br_ref,
                      ho_ref, hq_ref, idx_ref, gate_ref, rank_ref, cnt_ref, carry_ref):
    @pl.when(pl.program_id(0) == 0)
    def _():
        carry_ref[...] = jnp.zeros_like(carry_ref)

    x = DEEPNORM_ALPHA * h_ref[...] + mix_ref[...]
    mu = jnp.mean(x, axis=-1, keepdims=True)
    xc = x - mu
    var = jnp.mean(xc * xc, axis=-1, keepdims=True)
    hn = xc * lax.rsqrt(var + LN_EPS) * g_ref[...] + b_ref[...]
    ho_ref[...] = hn
    hq_ref[...] = _pack_halves(hn)

    logits =jnp.dot(hn, wr_ref[...], precision=lax.Precision.HIGHEST,
                     preferred_element_type=F32) + br_ref[...]
    tm = logits.shape[0]
    lane_e = lax.broadcasted_iota(jnp.int32, (tm, N_EXPERTS), 1)
    lane_s = lax.broadcasted_iota(jnp.int32, (tm, SLOT_LANES), 1)
    work = logits
    sel = jnp.zeros((tm, N_EXPERTS), F32)
    top_v, top_i, hot = [], [], []
    for _ in range(TOP_K):
        mx = jnp.max(work, axis=-1, keepdims=True)
        ix = jnp.min(jnp.where(work == mx, lane_e, N_EXPERTS), axis=-1, keepdims=True)
        oh = lane_e == ix
        work = jnp.where(oh, -jnp.inf, work)
        sel = sel + oh.astype(F32)
        top_v.append(mx)
        top_i.append(ix)
        hot.append(oh)
    ex = [jnp.exp(v - top_v[0]) for v in top_v]
    den = ex[0] + ex[1] + ex[2] + ex[3]

    r = lax.broadcasted_iota(jnp.int32, (tm, tm), 0)
    c = lax.broadcasted_iota(jnp.int32, (tm, tm), 1)
    strict = jnp.where(c < r, 1.0, 0.0).astype(BF16)
    prefix = _dot(strict, sel.astype(BF16)) + carry_ref[0:1, :]
    carry_ref[0:1, :] = carry_ref[0:1, :] + jnp.sum(sel, axis=0, keepdims=True)
    cnt_ref[...] = jnp.broadcast_to(carry_ref[0:1, :], cnt_ref.shape)

    idx_o = jnp.zeros((tm, SLOT_LANES), jnp.int32)
    gate_o = jnp.zeros((tm, SLOT_LANES), F32)
    rank_o = jnp.zeros((tm, SLOT_LANES), jnp.int32)
    for k in range(TOP_K):
        rk = jnp.sum(jnp.where(hot[k], prefix, 0.0), axis=-1, keepdims=True)
        idx_o = jnp.where(lane_s == k, top_i[k], idx_o)
        gate_o = jnp.where(lane_s == k, ex[k] / den, gate_o)
        rank_o = jnp.where(lane_s == k, rk.astype(jnp.int32), rank_o)
    idx_ref[...] = idx_o
    gate_ref[...] = gate_o
    rank_ref[...] = rank_o


def _ln_router(h, mix, ln_g, ln_b, w_router, b_router):
    n_tok, d = h.shape
    tm = LNR_TM
    row = lambda i: (i, 0)
    fixed = lambda i: (0, 0)
    return pl.pallas_call(
        _ln_router_kernel,
        grid=(n_tok // tm,),
        in_specs=[
            pl.BlockSpec((tm, d), row), pl.BlockSpec((tm, d), row),
            pl.BlockSpec((1, d), fixed), pl.BlockSpec((1, d), fixed),
            pl.BlockSpec((d, N_EXPERTS), fixed), pl.BlockSpec((1, N_EXPERTS), fixed),
        ],
        out_specs=[
            pl.BlockSpec((tm, d), row), pl.BlockSpec((tm, d // 2), row),
            pl.BlockSpec((tm, SLOT_LANES), row), pl.BlockSpec((tm, SLOT_LANES), row),
            pl.BlockSpec((tm, SLOT_LANES), row), pl.BlockSpec((8, N_EXPERTS), fixed),
        ],
        out_shape=[
            jax.ShapeDtypeStruct((n_tok, d), F32), jax.ShapeDtypeStruct((n_tok, d // 2), U32),
            jax.ShapeDtypeStruct((n_tok, SLOT_LANES), jnp.int32),
            jax.ShapeDtypeStruct((n_tok, SLOT_LANES), F32),
            jax.ShapeDtypeStruct((n_tok, SLOT_LANES), jnp.int32),
            jax.ShapeDtypeStruct((8, N_EXPERTS), F32),
        ],
        scratch_shapes=[pltpu.VMEM((8, N_EXPERTS), F32)],
        compiler_params=_params("arbitrary"),
        name="ln_router",
    )(h, mix, ln_g.reshape(1, d), ln_b.reshape(1, d), w_router, b_router.reshape(1, N_EXPERTS))


MOE_TM = 1280
MOE_SB = 256
MOE_NSUB = MOE_TM // MOE_SB
MOE_TH = 128
MOE_TN = 512
MOE_T1 = D_EXPERT // MOE_TH
MOE_T2 = D_MODEL // 2 // MOE_TN
MOE_UNIT = 32
MOE_ISUB = 4
MOE_R1 = 32
MOE_R2 = 16
assert MOE_ISUB * (MOE_T1 * MOE_R1 + MOE_T2 * MOE_R2) == MOE_TM
MOE_KD = 2 * MOE_TH


def _moe_kernel(e_ref, n_ref, prow_ref, nv_ref, ids_cur, ids_nxt, xq_hbm, wg_ref, wl_ref, bg_ref, bl_ref,
                wdl_ref, wdh_ref, bdl_ref, bdh_ref, y_hbm, xbuf, act_ref, wcat_ref, obuf, xsem, osem):
    i = pl.program_id(0)
    t = pl.program_id(1)
    n_i = n_ref[i]
    nsub_i = (n_i + MOE_SB - 1) // MOE_SB
    slot = i % 2
    half = D_MODEL // 2

    def issue_rows(ids_ref, slot_, base, count):
        for r in range(count):
            pltpu.make_async_copy(xq_hbm.at[pl.ds(ids_ref[0, base + r], 1)],
                                  xbuf.at[slot_, pl.ds(base + r, 1)], xsem.at[slot_]).start()

    def issue_gate_up(t1, s):
        issue_rows(ids_nxt, 1 - slot, (t1 * MOE_ISUB + s) * MOE_R1, MOE_R1)

    def issue_down(t2, s):
        issue_rows(ids_nxt, 1 - slot, MOE_T1 * MOE_ISUB * MOE_R1 + (t2 * MOE_ISUB + s) * MOE_R2, MOE_R2)

    def slot_rows(slot_):
        return pltpu.make_async_copy(xq_hbm.at[pl.ds(0, MOE_TM)], xbuf.at[slot_], xsem.at[slot_])

    @pl.when(jnp.logical_and(i == 0, t == 0))
    def _first_block_rows():
        def body(u, c):
            issue_rows(ids_cur, 0, u * MOE_UNIT, MOE_UNIT)
            return c
        lax.fori_loop(0, MOE_TM // MOE_UNIT, body, 0)

    @pl.when(jnp.logical_and(t == 0, n_i > 0))
    def _wait_rows():
        slot_rows(slot).wait()

    def sub_blocks(body):
        for s0 in range(0, MOE_NSUB - 1, 2):
            pl.when(s0 + 1 < nsub_i)(functools.partial(body, s0, 2))
            pl.when(s0 + 1 == nsub_i)(functools.partial(body, s0, 1))
        if MOE_NSUB % 2:
            pl.when(MOE_NSUB - 1 < nsub_i)(functools.partial(body, MOE_NSUB - 1, 1))

    @pl.when(jnp.logical_and(t < MOE_T1, n_i > 0))
    def _gate_up():
        wcat_ref[:, 0:MOE_TH] = wg_ref[...].astype(BF16)
        wcat_ref[:, MOE_TH:] = wl_ref[...].astype(BF16)

        @sub_blocks
        def _(s0, ns):
                for s in range(s0, min(s0 + ns, MOE_ISUB)):
                    issue_gate_up(t, s)
                rows = slice(s0 * MOE_SB, (s0 + ns) * MOE_SB)
                x_lo, x_hi = _unpack_halves(xbuf[slot, rows, :])
                hcat = (_dot(x_lo.astype(BF16), wcat_ref[0:half, :])
                        + _dot(x_hi.astype(BF16), wcat_ref[half:, :]))
                hg = jnp.minimum(hcat[:, :MOE_TH] + bg_ref[...], SWIGLU_LIMIT)
                hl = jnp.clip(hcat[:, MOE_TH:] + bl_ref[...], -SWIGLU_LIMIT, SWIGLU_LIMIT)
                a = (hg * jax.nn.sigmoid(SWIGLU_ALPHA * hg) * (hl + 1.0)).astype(BF16)

                @pl.when(t % 2 == 0)
                def _():
                    act_ref[t // 2, rows, 0:MOE_TH] = a

                @pl.when(t % 2 == 1)
                def _():
                    act_ref[t // 2, rows, MOE_TH:] = a

    @pl.when(jnp.logical_and(t >= MOE_T1, n_i > 0))
    def _down():
        t2 = t - MOE_T1
        oslot = t2 % 2
        wdl = wdl_ref[...].astype(BF16)
        wdh = wdh_ref[...].astype(BF16)
        col0 = pl.multiple_of(t2 * MOE_TN, MOE_TN)

        def result_copy(slot_, s):
            return pltpu.make_async_copy(
                obuf.at[slot_, pl.ds(s * MOE_SB, MOE_SB)],
                y_hbm.at[pl.ds(pl.multiple_of(prow_ref[i] + s * MOE_SB, MOE_SB), MOE_SB),
                         pl.ds(col0, MOE_TN)], osem.at[slot_])

        @sub_blocks
        def _(s0, ns):
                for s in range(s0, min(s0 + ns, MOE_ISUB)):
                    issue_down(t2, s)
                rows = slice(s0 * MOE_SB, (s0 + ns) * MOE_SB)
                y_lo = bdl_ref[...]
                y_hi = bdh_ref[...]
                for k in range(D_EXPERT // MOE_KD):
                    a = act_ref[k, rows, :]
                    y_lo = y_lo + _dot(a, wdl[k * MOE_KD:(k + 1) * MOE_KD, :])
                    y_hi = y_hi + _dot(a, wdh[k * MOE_KD:(k + 1) * MOE_KD, :])
                obuf[oslot, rows, :] = _pack_halves(jnp.concatenate([y_lo, y_hi], axis=1))
                for s in range(s0, s0 + ns):
                    result_copy(oslot, s).start()

        for s in range(MOE_NSUB):
            @pl.when(jnp.logical_and(s < nsub_i, t2 >= 1))
            def _():
                result_copy(1 - oslot, s).wait()

            @pl.when(jnp.logical_and(s < nsub_i, t2 == MOE_T2 - 1))
            def _():
                result_copy(oslot, s).wait()

    @pl.when(jnp.logical_and(t == MOE_T1 + MOE_T2 - 1, n_i > 0))
    def _missing_sub_block_shares():
        for s in range(1, MOE_ISUB):
            @pl.when(s >= nsub_i)
            def _():
                def gate_up_share(t1, c):
                    issue_gate_up(t1, s)
                    return c

                def down_share(t2, c):
                    issue_down(t2, s)
                    return c
                lax.fori_loop(0, MOE_T1, gate_up_share, 0)
                lax.fori_loop(0, MOE_T2, down_share, 0)

    @pl.when(jnp.logical_and(i == nv_ref[0] - 1, t == MOE_T1 + MOE_T2 - 1))
    def _zero_tail():
        zslot = 1 - slot
        slot_rows(zslot).wait()
        xbuf[zslot, 0:MOE_SB, :] = jnp.zeros((MOE_SB, half), U32)

        def tail_copy(j):
            return pltpu.make_async_copy(
                xbuf.at[zslot, pl.ds(0, MOE_SB)],
                y_hbm.at[pl.ds(pl.multiple_of(j * MOE_SB, MOE_SB), MOE_SB)], xsem.at[zslot])

        def start(j, c):
            tail_copy(j).start()
            return c

        def wait(j, c):
            tail_copy(j).wait()
            return c

        first = nv_ref[1] // MOE_SB
        total = y_hbm.shape[0] // MOE_SB
        lax.fori_loop(first, total, start, 0)
        lax.fori_loop(first, total, wait, 0)


def _moe_experts(xq, blk_exp, blk_n, blk_prow, n_valid_blocks, ids, w_gu, b_gu, w_down, b_down, layer,
                 n_rows_out):
    nb = ids.shape[0]
    t_last = MOE_T1 + MOE_T2 - 1
    half = D_MODEL // 2
    hi_blk = half // MOE_TN

    def src(i, nv):
        return jnp.maximum(jnp.minimum(i, nv[0] - 1), 0)

    def tt(i, t, nv):
        return jnp.where(i < nv[0], t, t_last)

    def t1(i, t, nv):
        return jnp.minimum(tt(i, t, nv), MOE_T1 - 1)

    def t2(i, t, nv):
        return jnp.maximum(tt(i, t, nv) - MOE_T1, 0)

    ids_cur_map = lambda i, t, e, n, p, nv: (src(i, nv), 0, 0)
    ids_nxt_map = lambda i, t, e, n, p, nv: (jnp.minimum(src(i, nv) + 1, nb - 1), 0, 0)
    wg_map = lambda i, t, e, n, p, nv: (layer, e[i], 0, t1(i, t, nv))
    wl_map = lambda i, t, e, n, p, nv: (layer, e[i], 0, MOE_T1 + t1(i, t, nv))
    wdl_map = lambda i, t, e, n, p, nv: (layer, e[i], 0, t2(i, t, nv))
    wdh_map = lambda i, t, e, n, p, nv: (layer, e[i], 0, hi_blk + t2(i, t, nv))
    grid_spec = pltpu.PrefetchScalarGridSpec(
        num_scalar_prefetch=4,
        grid=(nb, MOE_T1 + MOE_T2),
        in_specs=[
            pl.BlockSpec((None, 1, MOE_TM), ids_cur_map, memory_space=pltpu.SMEM),
            pl.BlockSpec((None, 1, MOE_TM), ids_nxt_map, memory_space=pltpu.SMEM),
            pl.BlockSpec(memory_space=pl.ANY),
            pl.BlockSpec((None, None, D_MODEL, MOE_TH), wg_map),
            pl.BlockSpec((None, None, D_MODEL, MOE_TH), wl_map),
            pl.BlockSpec((None, None, 1, MOE_TH), wg_map),
            pl.BlockSpec((None, None, 1, MOE_TH), wl_map),
            pl.BlockSpec((None, None, D_EXPERT, MOE_TN), wdl_map),
            pl.BlockSpec((None, None, D_EXPERT, MOE_TN), wdh_map),
            pl.BlockSpec((None, None, 1, MOE_TN), wdl_map),
            pl.BlockSpec((None, None, 1, MOE_TN), wdh_map),
        ],
        out_specs=pl.BlockSpec(memory_space=pl.ANY),
        scratch_shapes=[
            pltpu.VMEM((2, MOE_TM, half), U32),
            pltpu.VMEM((D_EXPERT // MOE_KD, MOE_TM, MOE_KD), BF16),
            pltpu.VMEM((D_MODEL, 2 * MOE_TH), BF16),
            pltpu.VMEM((2, MOE_TM, MOE_TN), U32),
            pltpu.SemaphoreType.DMA((2,)),
            pltpu.SemaphoreType.DMA((2,)),
        ],
    )
    bgu = b_gu.reshape(DEPTH, N_EXPERTS, 1, 2 * D_EXPERT)
    ids3 = ids.reshape(nb, 1, MOE_TM)
    bdn = b_down.reshape(DEPTH, N_EXPERTS, 1, D_MODEL)
    return pl.pallas_call(
        _moe_kernel,
        grid_spec=grid_spec,
        out_shape=jax.ShapeDtypeStruct((n_rows_out, half), U32),
        compiler_params=_params("arbitrary", "arbitrary"),
        name="moe_experts",
    )(blk_exp, blk_n, blk_prow, n_valid_blocks, ids3, ids3, xq, w_gu, w_gu, bgu, bgu,
      w_down, w_down, bdn, bdn)


LNC_TM = 128


def _ln_combine_kernel(ids_cur, ids_nxt, h_ref, gate_ref, g_ref, b_ref, y_hbm, ho_ref, hb_ref, ybuf, sem):
    i = pl.program_id(0)
    tm = h_ref.shape[0]
    w = h_ref.shape[1] // 2
    rows = TOP_K * tm
    slot = i % 2

    def row_copy(ids_ref, slot_, idx):
        return pltpu.make_async_copy(y_hbm.at[pl.ds(ids_ref[0, idx], 1)],
                                     ybuf.at[slot_, pl.ds(idx, 1)], sem.at[slot_])

    def tile_rows(slot_):
        return pltpu.make_async_copy(y_hbm.at[pl.ds(0, rows)], ybuf.at[slot_], sem.at[slot_])

    @pl.when(i == 0)
    def _():
        def body(u, c):
            for r in range(MOE_UNIT):
                row_copy(ids_cur, 0, u * MOE_UNIT + r).start()
            return c
        lax.fori_loop(0, rows // MOE_UNIT, body, 0)

    tile_rows(slot).wait()
    for idx in range(rows):
        row_copy(ids_nxt, 1 - slot, idx).start()

    x_lo = DEEPNORM_ALPHA * h_ref[:, :w]
    x_hi = DEEPNORM_ALPHA * h_ref[:, w:]
    for k in range(TOP_K):
        lo, hi = _unpack_halves(ybuf[slot, k * tm:(k + 1) * tm, :])
        gk = gate_ref[:, k:k + 1]
        x_lo = x_lo + gk * lo
        x_hi = x_hi + gk * hi
    x = jnp.concatenate([x_lo, x_hi], axis=1)
    mu = jnp.mean(x, axis=-1, keepdims=True)
    xc = x - mu
    var = jnp.mean(xc * xc, axis=-1, keepdims=True)
    hn = xc * lax.rsqrt(var + LN_EPS) * g_ref[...] + b_ref[...]
    ho_ref[...] = hn
    hb_ref[...] = hn.astype(BF16)

    @pl.when(i == pl.num_programs(0) - 1)
    def _():
        tile_rows(1 - slot).wait()


def _ln_combine(h, yq, dest_tiles, gates, ln_g, ln_b):
    n_tok, d = h.shape
    tm = LNC_TM
    nblk = n_tok // tm
    row = lambda i: (i, 0)
    fixed = lambda i: (0, 0)
    dest3 = dest_tiles.reshape(nblk, 1, TOP_K * tm)
    return pl.pallas_call(
        _ln_combine_kernel,
        grid=(nblk,),
        in_specs=[
            pl.BlockSpec((None, 1, TOP_K * tm), lambda i: (i, 0, 0), memory_space=pltpu.SMEM),
            pl.BlockSpec((None, 1, TOP_K * tm), lambda i: (jnp.minimum(i + 1, nblk - 1), 0, 0),
                         memory_space=pltpu.SMEM),
            pl.BlockSpec((tm, d), row),
            pl.BlockSpec((tm, SLOT_LANES), row),
            pl.BlockSpec((1, d), fixed), pl.BlockSpec((1, d), fixed),
            pl.BlockSpec(memory_space=pl.ANY),
        ],
        out_specs=[pl.BlockSpec((tm, d), row), pl.BlockSpec((tm, d), row)],
        out_shape=[jax.ShapeDtypeStruct((n_tok, d), F32), jax.ShapeDtypeStruct((n_tok, d), BF16)],
        scratch_shapes=[pltpu.VMEM((2, TOP_K * tm, d // 2), U32), pltpu.SemaphoreType.DMA((2,))],
        compiler_params=_params("arbitrary"),
        name="ln_combine",
    )(dest3, dest3, h, gates, ln_g.reshape(1, d), ln_b.reshape(1, d), yq)


def _moe_layer(h, hq, top_idx, gates, rank, counts, w_gu, b_gu, w_down, b_down, layer, ln_g, ln_b):
    n_tok = h.shape[0]
    n_assign = n_tok * TOP_K
    nb = N_EXPERTS + n_assign // MOE_TM
    n_rows_out = n_assign + N_EXPERTS * MOE_SB
    i32 = jnp.int32
    cnt = counts[0].astype(i32)
    pcnt = (cnt + MOE_SB - 1) // MOE_SB * MOE_SB
    pstart = jnp.cumsum(pcnt) - pcnt
    nblk = (cnt + MOE_TM - 1) // MOE_TM
    bend = jnp.cumsum(nblk)
    bfirst = bend - nblk
    n_valid = bend[-1]
    e_flat = top_idx[:, :TOP_K].reshape(-1)
    rank_flat = rank[:, :TOP_K].reshape(-1)
    tok = jnp.arange(n_assign, dtype=i32) // TOP_K
    dest = pstart[e_flat] + rank_flat
    blk = jnp.arange(nb, dtype=i32)
    src = jnp.maximum(jnp.minimum(blk, n_valid - 1), 0)
    be = jnp.minimum(jnp.sum((bend[None, :] <= src[:, None]).astype(i32), axis=1), N_EXPERTS - 1)
    b_in = src - bfirst[be]
    blk_n = jnp.where(blk < n_valid, jnp.clip(cnt[be] - b_in * MOE_TM, 0, MOE_TM), 0).astype(i32)
    blk_prow = (pstart[be] + b_in * MOE_TM).astype(i32)
    id_pos = (bfirst[e_flat] + rank_flat // MOE_TM) * MOE_TM + rank_flat % MOE_TM
    ids = jnp.zeros((nb * MOE_TM,), i32).at[id_pos].set(tok).reshape(nb, MOE_TM)
    used_rows = pstart[-1] + pcnt[-1]
    yq = _moe_experts(hq, be.astype(i32), blk_n, blk_prow, jnp.stack([n_valid, used_rows]).astype(i32), ids,
                      w_gu, b_gu, w_down, b_down, layer, n_rows_out)
    tm = LNC_TM
    dest_tiles = dest.reshape(n_tok // tm, tm, TOP_K).transpose(0, 2, 1).reshape(n_tok // tm, TOP_K * tm)
    return _ln_combine(h, yq, dest_tiles, gates, ln_g, ln_b)


def kernel(x, positions, ln_g, ln_b, hy_w_in, hy_w_out, gm_ln_g, gm_ln_b, gm_w_s, gm_b_s, ssm_lam_re, ssm_lam_im, ssm_log_dt, ssm_b_re, ssm_b_im, ssm_c_re, ssm_c_im, ssm_d, ssm_w_glu, ssm_b_glu, mla_w_in, mla_q_norm_g, mla_kv_norm_g, mla_w_uq, mla_w_ukv, mla_w_o, moe_w_router, moe_b_router, moe_w_gu, moe_b_gu, moe_w_down, moe_b_down):
    batch, seq, d = x.shape
    n_tok = batch * seq
    h = x.reshape(n_tok, d)
    hb = h.astype(BF16)
    for layer in range(DEPTH):
        i = layer // 2
        if layer % 2 == 0:
            z_gm = _mm([hb], hy_w_in[i], col0=0, n_cols=2 * GM_WIDTH, tm=1024, tn=256,
                       out_dtype=BF16, epilogue="gelu", name="hy_in_gm")
            z_ssm = _mm([hb], hy_w_in[i], col0=2 * GM_WIDTH, n_cols=SSM_WIDTH, tm=1024, tn=256,
                        out_dtype=F32, name="hy_in_ssm")
            y_gm = _spatial_gating(z_gm, gm_ln_g[i], gm_ln_b[i], gm_w_s[i], gm_b_s[i])
            tables = _s5_tables(ssm_lam_re[i], ssm_lam_im[i], ssm_log_dt[i], ssm_b_re[i], ssm_b_im[i],
                                ssm_c_re[i], ssm_c_im[i], ssm_d[i], seq)
            y_act = _s5_mixer(z_ssm, batch, seq, tables)
            y_ssm = _mm([y_act], ssm_w_glu[i], tm=1024, tn=512, out_dtype=BF16, epilogue="glu",
                        bias=ssm_b_glu[i], mul=y_act, name="s5_glu")
            mix = _mm([y_gm, y_ssm], hy_w_out[i], tm=1024, tn=256, out_dtype=F32, name="hy_out")
        else:
            w_in = mla_w_in[i]
            n_main = MLA_Q_RANK + MLA_KV_RANK
            c_main = _mm([hb], w_in, col0=0, n_cols=n_main, tm=1024, tn=256, out_dtype=F32,
                         name="mla_in")
            k_rope = _mm([hb], w_in[:, n_main:], tm=1024, tn=MLA_ROPE, out_dtype=F32, name="mla_in_rope")
            tab, k_rope = _rope_tables(positions, k_rope)
            q = _mm([(c_main, MLA_Q_RANK, 0)], mla_w_uq[i], tm=1024, tn=768, out_dtype=BF16,
                    prologue="rms", gain=mla_q_norm_g[i], name="mla_uq")
            kv = _mm([(c_main, MLA_KV_RANK, MLA_Q_RANK // MLA_KV_RANK)], mla_w_ukv[i], tm=1024, tn=1024,
                     out_dtype=BF16, prologue="rms", gain=mla_kv_norm_g[i], name="mla_ukv")
            o = _attention(q, kv, k_rope, tab, batch, seq)
            mix = _mm([o], mla_w_o[i], tm=1024, tn=256, out_dtype=F32, name="mla_out")
        h, hq, top_idx, gates, rank, counts = _ln_router(
            h, mix, ln_g[layer, 0], ln_b[layer, 0], moe_w_router[layer], moe_b_router[layer])
        h, hb = _moe_layer(h, hq, top_idx, gates, rank, counts, moe_w_gu, moe_b_gu,
                           moe_w_down, moe_b_down, layer, ln_g[layer, 1], ln_b[layer, 1])
    return h.reshape(batch, seq, d)
```

```python
import functools
import math

import numpy as np
import jax
import jax.numpy as jnp
from jax import lax
from jax.experimental import pallas as pl
from jax.experimental.pallas import tpu as pltpu

F32 = jnp.float32
BF16 = jnp.bfloat16
U32 = jnp.uint32
HI_MASK = np.uint32(0xFFFF0000)

D_MODEL = 4096
DEPTH = 2
CHUNK = 64
DEEPNORM_ALPHA = (2 * DEPTH) ** 0.25
LN_EPS = 1e-5
RMS_EPS = 1e-6
LANES = 128

GM_WIDTH = 2048
GM_GROUPS = 8
GM_GROUP_DIM = 256
GM_BLOCK = 128
SSM_WIDTH = 2048
SSM_P = 16
SSM_GROUPS = 128
SSM_N = 64
SSM_T = 16
SSM_OCT = LANES // SSM_P
SSM_NOCT = SSM_GROUPS // SSM_OCT
SSM_PAIRS = SSM_OCT // 2
SSM_SW = SSM_OCT * SSM_N

MLA_HEADS = 32
MLA_Q_RANK = 1024
MLA_KV_RANK = 512
MLA_NOPE = 128
MLA_ROPE = 64
MLA_V = 128
ROPE_THETA = 10000.0

N_EXPERTS = 32
TOP_K = 4
D_EXPERT = 1024
SWIGLU_LIMIT = 7.0
SWIGLU_ALPHA = 1.702

VMEM_LIMIT_BYTES = 56 * 1024 * 1024


def _params(*sem):
    return pltpu.CompilerParams(dimension_semantics=sem, vmem_limit_bytes=VMEM_LIMIT_BYTES)


def _dot(a, b):
    return jnp.dot(a, b, preferred_element_type=F32)


def _dot_nt(a, b):
    return lax.dot_general(a, b, (((1,), (1,)), ((), ())), preferred_element_type=F32)


def _mm_kernel(*refs, prologue, epilogue, n_x):
    it = iter(refs)
    x_refs = [next(it) for _ in range(n_x)]
    w_ref = next(it)
    g_ref = next(it) if prologue == "rms" else None
    b_ref = next(it) if epilogue == "glu" else None
    y_ref = next(it) if epilogue == "glu" else None
    o_ref = next(it)
    xs_ref = next(it) if prologue == "rms" else None

    if prologue == "rms":
        @pl.when(pl.program_id(1) == 0)
        def _():
            xf = x_refs[0][...].astype(F32)
            ms = jnp.mean(xf * xf, axis=-1, keepdims=True)
            xs_ref[...] = (xf * lax.rsqrt(ms + RMS_EPS) * g_ref[...]).astype(BF16)
        xs = [xs_ref[...]]
    else:
        xs = [r[...].astype(BF16) for r in x_refs]

    acc = None
    k0 = 0
    for x in xs:
        kk = x.shape[1]
        part = _dot(x, w_ref[k0:k0 + kk, :].astype(BF16))
        acc = part if acc is None else acc + part
        k0 += kk
    if epilogue == "gelu":
        acc = jax.nn.gelu(acc)
    elif epilogue == "glu":
        acc = y_ref[...].astype(F32) * jax.nn.sigmoid(acc + b_ref[...])
    o_ref[...] = acc.astype(o_ref.dtype)


def _mm(xs, w, *, col0=0, n_cols=None, tm, tn, out_dtype, prologue=None, gain=None,
        epilogue=None, bias=None, mul=None, name):
    xs = [(x, x.shape[1], 0) if not isinstance(x, tuple) else x for x in xs]
    m = xs[0][0].shape[0]
    k_total = w.shape[0]
    n_cols = w.shape[1] - col0 if n_cols is None else n_cols
    tn = min(tn, n_cols)
    assert m % tm == 0 and n_cols % tn == 0 and col0 % tn == 0
    cb0 = col0 // tn
    assert sum(kw for _, kw, _ in xs) == k_total
    in_specs = [pl.BlockSpec((tm, kw), functools.partial(lambda i, j, cb: (i, cb), cb=cb))
                for _, kw, cb in xs]
    in_specs.append(pl.BlockSpec((k_total, tn), lambda i, j: (0, cb0 + j)))
    args = [x for x, _, _ in xs] + [w]
    scratch = []
    if prologue == "rms":
        in_specs.append(pl.BlockSpec((1, k_total), lambda i, j: (0, 0)))
        args.append(gain.reshape(1, k_total))
        scratch.append(pltpu.VMEM((tm, k_total), BF16))
    if epilogue == "glu":
        in_specs.append(pl.BlockSpec((1, tn), lambda i, j: (0, j)))
        in_specs.append(pl.BlockSpec((tm, tn), lambda i, j: (i, j)))
        args += [bias.reshape(1, n_cols), mul]
    return pl.pallas_call(
        functools.partial(_mm_kernel, prologue=prologue, epilogue=epilogue, n_x=len(xs)),
        grid=(m // tm, n_cols // tn),
        in_specs=in_specs,
        out_specs=pl.BlockSpec((tm, tn), lambda i, j: (i, j)),
        out_shape=jax.ShapeDtypeStruct((m, n_cols), out_dtype),
        scratch_shapes=scratch,
        compiler_params=_params("arbitrary", "arbitrary"),
        name=name,
    )(*args)


def _gating_kernel(u_ref, v_ref, g_ref, b_ref, ws_ref, bst_ref, o_ref):
    row = lax.broadcasted_iota(jnp.int32, (GM_BLOCK, GM_BLOCK), 0)
    col = lax.broadcasted_iota(jnp.int32, (GM_BLOCK, GM_BLOCK), 1)
    visible = (col // CHUNK) <= (row // CHUNK)
    for g in range(GM_GROUPS):
        sl = slice(g * GM_GROUP_DIM, (g + 1) * GM_GROUP_DIM)
        v = v_ref[:, sl].astype(F32)
        mu = jnp.mean(v, axis=-1, keepdims=True)
        vc = v - mu
        var = jnp.mean(vc * vc, axis=-1, keepdims=True)
        vn = vc * lax.rsqrt(var + LN_EPS) * g_ref[:, sl] + b_ref[:, sl]
        w = jnp.where(visible, ws_ref[g], 0.0).astype(BF16)
        s = _dot(w, vn.astype(BF16)) + bst_ref[:, g:g + 1]
        o_ref[:, sl] = (u_ref[:, sl].astype(F32) * s).astype(o_ref.dtype)


def _spatial_gating(z_gm, ln_g, ln_b, w_s, b_s):
    n_tok = z_gm.shape[0]
    return pl.pallas_call(
        _gating_kernel,
        grid=(n_tok // GM_BLOCK,),
        in_specs=[
            pl.BlockSpec((GM_BLOCK, GM_WIDTH), lambda i: (i, 0)),
            pl.BlockSpec((GM_BLOCK, GM_WIDTH), lambda i: (i, 1)),
            pl.BlockSpec((1, GM_WIDTH), lambda i: (0, 0)),
            pl.BlockSpec((1, GM_WIDTH), lambda i: (0, 0)),
            pl.BlockSpec((GM_GROUPS, GM_BLOCK, GM_BLOCK), lambda i: (0, 0, 0)),
            pl.BlockSpec((GM_BLOCK, GM_GROUPS), lambda i: (0, 0)),
        ],
        out_specs=pl.BlockSpec((GM_BLOCK, GM_WIDTH), lambda i: (i, 0)),
        out_shape=jax.ShapeDtypeStruct((n_tok, GM_WIDTH), BF16),
        compiler_params=_params("arbitrary"),
        name="spatial_gating",
    )(z_gm, z_gm, ln_g.reshape(1, GM_WIDTH), ln_b.reshape(1, GM_WIDTH), w_s, b_s.T)


def _s5_tables(lam_re, lam_im, log_dt, b_re, b_im, c_re, c_im, d_skip, seq):
    hi = lax.Precision.HIGHEST
    n, p, t = SSM_N, SSM_P, SSM_T
    no, npr = SSM_NOCT, SSM_PAIRS
    dt = jnp.exp(log_dt)[:, None]
    mag = jnp.exp(lam_re * dt)
    ab_re = mag * jnp.cos(lam_im * dt)
    ab_im = mag * jnp.sin(lam_im * dt)
    den = lam_re * lam_re + lam_im * lam_im
    num_re = ab_re - 1.0
    coef_re = (num_re * lam_re + ab_im * lam_im) / den
    coef_im = (ab_im * lam_re - num_re * lam_im) / den
    bb_re = coef_re[..., None] * b_re - coef_im[..., None] * b_im
    bb_im = coef_re[..., None] * b_im + coef_im[..., None] * b_re

    def power(k):
        k = jnp.asarray(k, F32)[..., None, None]
        mk = jnp.exp(k * (lam_re * dt))
        return mk * jnp.cos(k * (lam_im * dt)), mk * jnp.sin(k * (lam_im * dt))

    pw_re, pw_im = power(np.arange(t + 1))
    cp_re = c_re[None] * pw_re[:, :, None, :] - c_im[None] * pw_im[:, :, None, :]
    cp_im = c_re[None] * pw_im[:, :, None, :] + c_im[None] * pw_re[:, :, None, :]
    kern = (jnp.einsum('tgpn,gnq->gtpq', cp_re[:t], bb_re, precision=hi)
            - jnp.einsum('tgpn,gnq->gtpq', cp_im[:t], bb_im, precision=hi))
    eye_o = jnp.eye(SSM_OCT, dtype=F32)
    eye_2 = jnp.eye(2, dtype=F32)
    kbd = jnp.einsum('oatpq,ab->otaqbp', kern.reshape(no, SSM_OCT, t, p, p), eye_o)
    kbd = kbd.reshape(no, t, LANES, LANES)
    rv_re, rv_im = pw_re[t - 1::-1], pw_im[t - 1::-1]
    st_re = rv_re[..., None] * bb_re[None] - rv_im[..., None] * bb_im[None]
    st_im = rv_re[..., None] * bb_im[None] + rv_im[..., None] * bb_re[None]
    st = jnp.stack([st_re, st_im], 0).reshape(2, t, no, npr, 2, n, p)
    w_st = jnp.einsum('rjoxanq,ab->ojxraqbn', st, eye_2).reshape(no, t, npr, 2, 2 * p, 2 * n)
    wc = jnp.stack([cp_re[1:], -cp_im[1:]], 0).reshape(2, t, no, npr, 2, p, n)
    w_ct = jnp.einsum('rioxapn,ab->oixrapbn', wc, eye_2).reshape(no, t, npr, 2, 2 * p, 2 * n)
    n_lvl = int(math.log2(seq // t))
    lv_re, lv_im = power(t * 2 ** np.arange(n_lvl))
    lr = lv_re.reshape(n_lvl, no, SSM_SW)
    li = lv_im.reshape(n_lvl, no, SSM_SW)
    lvl = jnp.stack([jnp.concatenate([lr, lr], -1), jnp.concatenate([-li, li], -1)], axis=1)
    lvl = lvl.transpose(2, 0, 1, 3).reshape(no, 2 * n_lvl, 2 * SSM_SW)
    d_vec = jnp.tile(d_skip.reshape(no, 1, LANES), (1, 1, t))
    return kbd.astype(BF16), w_st.astype(BF16), w_ct.astype(BF16), lvl, d_vec


def _s5_kernel(u_ref, kbd_ref, wst_ref, wct_ref, lvl_ref, d_ref, o_ref, toe_ref, st_ref, ct_ref,
               *, n_chunks, n_lvl):
    t = SSM_T

    @pl.when(pl.program_id(0) == 0)
    def _():
        toe_ref[...] = jnp.zeros_like(toe_ref)
        st_ref[...] = jnp.zeros_like(st_ref)
        ct_ref[...] = jnp.zeros_like(ct_ref)

    for i in range(t):
        for j in range(i + 1):
            toe_ref[j * LANES:(j + 1) * LANES, i * LANES:(i + 1) * LANES] = kbd_ref[i - j]
    for j in range(t):
        for x in range(SSM_PAIRS):
            for r in range(2):
                rows = slice(j * LANES + x * 2 * SSM_P, j * LANES + (x + 1) * 2 * SSM_P)
                lanes = slice(r * SSM_SW + x * LANES, r * SSM_SW + (x + 1) * LANES)
                st_ref[rows, lanes] = wst_ref[j, x, r]
                ct_ref[rows, lanes] = wct_ref[j, x, r]

    u = u_ref[...]
    ub = u.astype(BF16)
    h = _dot(ub, st_ref[...])
    cidx = lax.broadcasted_iota(jnp.int32, (u.shape[0], 1), 0) % n_chunks
    for lv in range(n_lvl):
        d = 1 << lv
        a_rr = lvl_ref[2 * lv:2 * lv + 1, :]
        a_is = lvl_ref[2 * lv + 1:2 * lv + 2, :]
        sh = jnp.where(cidx >= d, pltpu.roll(h, d, 0), 0.0)
        h = h + sh * a_rr + pltpu.roll(sh, SSM_SW, 1) * a_is
    h_prev = jnp.where(cidx >= 1, pltpu.roll(h, 1, 0), 0.0)
    y = _dot(ub, toe_ref[...]) + _dot_nt(h_prev.astype(BF16), ct_ref[...]) + d_ref[...] * u
    o_ref[...] = jax.nn.gelu(y).astype(o_ref.dtype)


def _s5_mixer(z_ssm, batch, seq, tables):
    kbd, w_st, w_ct, lvl, d_vec = tables
    t, no = SSM_T, SSM_NOCT
    n_chunks = seq // t
    rows = batch * n_chunks
    width = t * LANES
    n_lvl = lvl.shape[1] // 2
    u = z_ssm.reshape(rows, t, no, LANES).transpose(2, 0, 1, 3).reshape(no, rows, width)
    y = pl.pallas_call(
        functools.partial(_s5_kernel, n_chunks=n_chunks, n_lvl=n_lvl),
        grid=(no,),
        in_specs=[
            pl.BlockSpec((None, rows, width), lambda i: (i, 0, 0)),
            pl.BlockSpec((None, t, LANES, LANES), lambda i: (i, 0, 0, 0)),
            pl.BlockSpec((None, t, SSM_PAIRS, 2, 2 * SSM_P, LANES), lambda i: (i, 0, 0, 0, 0, 0)),
            pl.BlockSpec((None, t, SSM_PAIRS, 2, 2 * SSM_P, LANES), lambda i: (i, 0, 0, 0, 0, 0)),
            pl.BlockSpec((None, 2 * n_lvl, 2 * SSM_SW), lambda i: (i, 0, 0)),
            pl.BlockSpec((None, 1, width), lambda i: (i, 0, 0)),
        ],
        out_specs=pl.BlockSpec((None, rows, width), lambda i: (i, 0, 0)),
        out_shape=jax.ShapeDtypeStruct((no, rows, width), BF16),
        scratch_shapes=[
            pltpu.VMEM((width, width), BF16),
            pltpu.VMEM((width, 2 * SSM_SW), BF16),
            pltpu.VMEM((width, 2 * SSM_SW), BF16),
        ],
        compiler_params=_params("arbitrary"),
        name="s5_mixer",
    )(u, kbd, w_st, w_ct, lvl, d_vec)
    return y.reshape(no, rows, t, LANES).transpose(1, 2, 0, 3).reshape(batch * seq, SSM_WIDTH)


def _rope_kernel(pos_ref, freq_ref, kr_ref, tab_ref, kro_ref):
    ang = pos_ref[...].astype(F32) * freq_ref[...]
    lane = lax.broadcasted_iota(jnp.int32, ang.shape, 1)
    sin = jnp.sin(ang)
    tab = jnp.where(lane < MLA_ROPE, jnp.cos(ang), jnp.where(lane < MLA_ROPE + MLA_ROPE // 2, -sin, sin))
    tab_ref[...] = tab
    kr = kr_ref[...]
    half = MLA_ROPE // 2
    sw = jnp.concatenate([kr[:, half:], kr[:, :half]], axis=1)
    kro_ref[...] = (kr * tab[:, :MLA_ROPE] + sw * tab[:, MLA_ROPE:]).astype(kro_ref.dtype)


def _rope_tables(positions, k_rope):
    n_tok = k_rope.shape[0]
    tm = 512
    inv_freq = ROPE_THETA ** (-np.arange(0, MLA_ROPE, 2, dtype=np.float64) / MLA_ROPE)
    freq = jnp.asarray(np.tile(inv_freq, 4)[None, :], F32)
    return pl.pallas_call(
        _rope_kernel,
        grid=(n_tok // tm,),
        in_specs=[
            pl.BlockSpec((tm, 1), lambda i: (i, 0)),
            pl.BlockSpec((1, 2 * MLA_ROPE), lambda i: (0, 0)),
            pl.BlockSpec((tm, MLA_ROPE), lambda i: (i, 0)),
        ],
        out_specs=[
            pl.BlockSpec((tm, 2 * MLA_ROPE), lambda i: (i, 0)),
            pl.BlockSpec((tm, MLA_ROPE), lambda i: (i, 0)),
        ],
        out_shape=[
            jax.ShapeDtypeStruct((n_tok, 2 * MLA_ROPE), F32),
            jax.ShapeDtypeStruct((n_tok, MLA_ROPE), BF16),
        ],
        compiler_params=_params("arbitrary"),
        name="rope_tables",
    )(positions.reshape(n_tok, 1), freq, k_rope)


ATT_TK = 256
ATT_TQ = 2 * ATT_TK
ATT_DK = 2 * MLA_NOPE


def _attn_kernel(q_ref, tab_ref, k0_ref, v0_ref, k1_ref, v1_ref, kr_ref, o_ref, kc_ref):
    qi = pl.program_id(2)
    hd = MLA_NOPE + MLA_ROPE
    scale = hd ** -0.5 * math.log2(math.e)
    half = MLA_ROPE // 2
    tk = ATT_TK

    @pl.when(qi == 0)
    def _():
        for hh, k_ref in enumerate((k0_ref, k1_ref)):
            kc_ref[hh, :, 0:MLA_NOPE] = k_ref[...]
            kc_ref[hh, :, MLA_NOPE:hd] = kr_ref[...]
            kc_ref[hh, :, hd:] = jnp.zeros((kr_ref.shape[0], ATT_DK - hd), BF16)

    q = q_ref[...]
    cosf = tab_ref[:, :MLA_ROPE]
    sinf = tab_ref[:, MLA_ROPE:]
    qc = []
    for hh in range(2):
        qn = q[:, hh * hd:hh * hd + MLA_NOPE].astype(F32) * scale
        qr = q[:, hh * hd + MLA_NOPE:(hh + 1) * hd].astype(F32)
        sw = jnp.concatenate([qr[:, half:], qr[:, :half]], axis=1)
        qr = (qr * cosf + sw * sinf) * scale
        pad = jnp.zeros((ATT_TQ, ATT_DK - hd), F32)
        qc.append(jnp.concatenate([qn, qr, pad], axis=1).astype(BF16))
    v_refs = (v0_ref, v1_ref)
    row = lax.broadcasted_iota(jnp.int32, (tk, tk), 0)
    col = lax.broadcasted_iota(jnp.int32, (tk, tk), 1)
    visible = (col // CHUNK) <= (row // CHUNK)

    def update(state, hh, sub, j, masked):
        m, l, acc = state
        start = pl.multiple_of(j * tk, tk)
        ks = kc_ref[hh, pl.ds(start, tk), :]
        vs = v_refs[hh][pl.ds(start, tk), :]
        s = _dot_nt(qc[hh][sub * tk:(sub + 1) * tk], ks)
        if masked:
            s = jnp.where(visible, s, -1e30)
        m_new = jnp.maximum(m, jnp.max(s, axis=-1, keepdims=True))
        alpha = jnp.exp2(m - m_new)
        p = jnp.exp2(s - m_new)
        l = alpha * l + jnp.sum(p, axis=-1, keepdims=True)
        acc = alpha * acc + _dot(p.astype(BF16), vs)
        return m_new, l, acc

    chains = [(hh, sub) for hh in range(2) for sub in range(2)]

    def body(jj, states):
        states = tuple(update(st, hh, sub, 2 * jj, False) for st, (hh, sub) in zip(states, chains))
        return tuple(update(st, hh, sub, 2 * jj + 1, False) for st, (hh, sub) in zip(states, chains))

    init = tuple((jnp.full((tk, 1), -1e30, F32), jnp.zeros((tk, 1), F32), jnp.zeros((tk, MLA_V), F32))
                 for _ in chains)
    states = list(lax.fori_loop(0, qi, body, init))
    for c, (hh, sub) in enumerate(chains):
        st = update(states[c], hh, sub, 2 * qi, masked=(sub == 0))
        if sub == 1:
            st = update(st, hh, sub, 2 * qi + 1, masked=True)
        _, l, acc = st
        o_ref[sub * tk:(sub + 1) * tk, hh * MLA_V:(hh + 1) * MLA_V] = (acc / l).astype(o_ref.dtype)


def _attention(q, kv, k_rope, tab, batch, seq):
    n_tok = batch * seq
    nq = seq // ATT_TQ
    kvb = lambda off: pl.BlockSpec((seq, MLA_NOPE), lambda b, hp, qi: (b, 4 * hp + off))
    return pl.pallas_call(
        _attn_kernel,
        grid=(batch, MLA_HEADS // 2, nq),
        in_specs=[
            pl.BlockSpec((ATT_TQ, 2 * (MLA_NOPE + MLA_ROPE)), lambda b, hp, qi: (b * nq + qi, hp)),
            pl.BlockSpec((ATT_TQ, 2 * MLA_ROPE), lambda b, hp, qi: (b * nq + qi, 0)),
            kvb(0), kvb(1), kvb(2), kvb(3),
            pl.BlockSpec((seq, MLA_ROPE), lambda b, hp, qi: (b, 0)),
        ],
        out_specs=pl.BlockSpec((ATT_TQ, 2 * MLA_V), lambda b, hp, qi: (b * nq + qi, hp)),
        out_shape=jax.ShapeDtypeStruct((n_tok, MLA_HEADS * MLA_V), BF16),
        scratch_shapes=[pltpu.VMEM((2, seq, ATT_DK), BF16)],
        compiler_params=_params("arbitrary", "arbitrary", "arbitrary"),
        name="mla_attention",
    )(q, tab, kv, kv, kv, kv, k_rope)


LNR_TM = 256
SLOT_LANES = 128


def _pack_halves(x):
    w = x.shape[1] // 2
    bits = lax.bitcast_convert_type(x.astype(BF16).astype(F32), U32)
    return (bits[:, :w] >> 16) | (bits[:, w:] & HI_MASK)


def _unpack_halves(p):
    return (lax.bitcast_convert_type(p << 16, F32), lax.bitcast_convert_type(p & HI_MASK, F32))


def _ln_router_kernel(h_ref, mix_ref, g_ref, b_ref, wr_ref, br_ref,
                      ho_ref, hq_ref, idx_ref, gate_ref, rank_ref, cnt_ref, carry_ref):
    @pl.when(pl.program_id(0) == 0)
    def _():
        carry_ref[...] = jnp.zeros_like(carry_ref)

    x = DEEPNORM_ALPHA * h_ref[...] + mix_ref[...]
    mu = jnp.mean(x, axis=-1, keepdims=True)
    xc = x - mu
    var = jnp.mean(xc * xc, axis=-1, keepdims=True)
    hn = xc * lax.rsqrt(var + LN_EPS) * g_ref[...] + b_ref[...]
    ho_ref[...] = hn
    hq_ref[...] = _pack_halves(hn)

    logits =jnp.dot(hn, wr_ref[...], precision=lax.Precision.HIGHEST,
                     preferred_element_type=F32) + br_ref[...]
    tm = logits.shape[0]
    lane_e = lax.broadcasted_iota(jnp.int32, (tm, N_EXPERTS), 1)
    lane_s = lax.broadcasted_iota(jnp.int32, (tm, SLOT_LANES), 1)
    work = logits
    sel = jnp.zeros((tm, N_EXPERTS), F32)
    top_v, top_i, hot = [], [], []
    for _ in range(TOP_K):
        mx = jnp.max(work, axis=-1, keepdims=True)
        ix = jnp.min(jnp.where(work == mx, lane_e, N_EXPERTS), axis=-1, keepdims=True)
        oh = lane_e == ix
        work = jnp.where(oh, -jnp.inf, work)
        sel = sel + oh.astype(F32)
        top_v.append(mx)
        top_i.append(ix)
        hot.append(oh)
    ex = [jnp.exp(v - top_v[0]) for v in top_v]
    den = ex[0] + ex[1] + ex[2] + ex[3]

    r = lax.broadcasted_iota(jnp.int32, (tm, tm), 0)
    c = lax.broadcasted_iota(jnp.int32, (tm, tm), 1)
    strict = jnp.where(c < r, 1.0, 0.0).astype(BF16)
    prefix = _dot(strict, sel.astype(BF16)) + carry_ref[0:1, :]
    carry_ref[0:1, :] = carry_ref[0:1, :] + jnp.sum(sel, axis=0, keepdims=True)
    cnt_ref[...] = jnp.broadcast_to(carry_ref[0:1, :], cnt_ref.shape)

    idx_o = jnp.zeros((tm, SLOT_LANES), jnp.int32)
    gate_o = jnp.zeros((tm, SLOT_LANES), F32)
    rank_o = jnp.zeros((tm, SLOT_LANES), jnp.int32)
    for k in range(TOP_K):
        rk = jnp.sum(jnp.where(hot[k], prefix, 0.0), axis=-1, keepdims=True)
        idx_o = jnp.where(lane_s == k, top_i[k], idx_o)
        gate_o = jnp.where(lane_s == k, ex[k] / den, gate_o)
        rank_o = jnp.where(lane_s == k, rk.astype(jnp.int32), rank_o)
    idx_ref[...] = idx_o
    gate_ref[...] = gate_o
    rank_ref[...] = rank_o


def _ln_router(h, mix, ln_g, ln_b, w_router, b_router):
    n_tok, d = h.shape
    tm = LNR_TM
    row = lambda i: (i, 0)
    fixed = lambda i: (0, 0)
    return pl.pallas_call(
        _ln_router_kernel,
        grid=(n_tok // tm,),
        in_specs=[
            pl.BlockSpec((tm, d), row), pl.BlockSpec((tm, d), row),
            pl.BlockSpec((1, d), fixed), pl.BlockSpec((1, d), fixed),
            pl.BlockSpec((d, N_EXPERTS), fixed), pl.BlockSpec((1, N_EXPERTS), fixed),
        ],
        out_specs=[
            pl.BlockSpec((tm, d), row), pl.BlockSpec((tm, d // 2), row),
            pl.BlockSpec((tm, SLOT_LANES), row), pl.BlockSpec((tm, SLOT_LANES), row),
            pl.BlockSpec((tm, SLOT_LANES), row), pl.BlockSpec((8, N_EXPERTS), fixed),
        ],
        out_shape=[
            jax.ShapeDtypeStruct((n_tok, d), F32), jax.ShapeDtypeStruct((n_tok, d // 2), U32),
            jax.ShapeDtypeStruct((n_tok, SLOT_LANES), jnp.int32),
            jax.ShapeDtypeStruct((n_tok, SLOT_LANES), F32),
            jax.ShapeDtypeStruct((n_tok, SLOT_LANES), jnp.int32),
            jax.ShapeDtypeStruct((8, N_EXPERTS), F32),
        ],
        scratch_shapes=[pltpu.VMEM((8, N_EXPERTS), F32)],
        compiler_params=_params("arbitrary"),
        name="ln_router",
    )(h, mix, ln_g.reshape(1, d), ln_b.reshape(1, d), w_router, b_router.reshape(1, N_EXPERTS))


MOE_TM = 1280
MOE_SB = 256
MOE_NSUB = MOE_TM // MOE_SB
MOE_TH = 256
MOE_TN = 512
MOE_T1 = 2 * (D_EXPERT // MOE_TH)
MOE_T2 = D_MODEL // 2 // MOE_TN
MOE_UNIT = 32
MOE_ISUB = 4
MOE_R1 = 32
MOE_R2 = 16
assert MOE_ISUB * (MOE_T1 * MOE_R1 + MOE_T2 * MOE_R2) == MOE_TM
MOE_KD = MOE_TH


def _moe_kernel(e_ref, n_ref, prow_ref, nv_ref, ids_cur, ids_nxt, xq_hbm, wgu_ref, bgu_ref,
                wdl_ref, wdh_ref, bdl_ref, bdh_ref, y_hbm, xbuf, act_ref, hg_ref, wb_ref, obuf, xsem, osem):
    i = pl.program_id(0)
    t = pl.program_id(1)
    n_i = n_ref[i]
    nsub_i = (n_i + MOE_SB - 1) // MOE_SB
    slot = i % 2
    half = D_MODEL // 2

    def issue_rows(ids_ref, slot_, base, count):
        for r in range(count):
            pltpu.make_async_copy(xq_hbm.at[pl.ds(ids_ref[0, base + r], 1)],
                                  xbuf.at[slot_, pl.ds(base + r, 1)], xsem.at[slot_]).start()

    def issue_gate_up(t1, s):
        issue_rows(ids_nxt, 1 - slot, (t1 * MOE_ISUB + s) * MOE_R1, MOE_R1)

    def issue_down(t2, s):
        issue_rows(ids_nxt, 1 - slot, MOE_T1 * MOE_ISUB * MOE_R1 + (t2 * MOE_ISUB + s) * MOE_R2, MOE_R2)

    def slot_rows(slot_):
        return pltpu.make_async_copy(xq_hbm.at[pl.ds(0, MOE_TM)], xbuf.at[slot_], xsem.at[slot_])

    @pl.when(jnp.logical_and(i == 0, t == 0))
    def _first_block_rows():
        def body(u, c):
            issue_rows(ids_cur, 0, u * MOE_UNIT, MOE_UNIT)
            return c
        lax.fori_loop(0, MOE_TM // MOE_UNIT, body, 0)

    @pl.when(jnp.logical_and(t == 0, n_i > 0))
    def _wait_rows():
        slot_rows(slot).wait()

    def sub_blocks(body):
        for s0 in range(0, MOE_NSUB - 1, 2):
            pl.when(s0 + 1 < nsub_i)(functools.partial(body, s0, 2))
            pl.when(s0 + 1 == nsub_i)(functools.partial(body, s0, 1))
        if MOE_NSUB % 2:
            pl.when(MOE_NSUB - 1 < nsub_i)(functools.partial(body, MOE_NSUB - 1, 1))

    @pl.when(jnp.logical_and(t < MOE_T1, n_i > 0))
    def _gate_up():
        wb_ref[...] = wgu_ref[...].astype(BF16)

        @sub_blocks
        def _(s0, ns):
                for s in range(s0, min(s0 + ns, MOE_ISUB)):
                    issue_gate_up(t, s)
                rows = slice(s0 * MOE_SB, (s0 + ns) * MOE_SB)
                x_lo, x_hi = _unpack_halves(xbuf[slot, rows, :])
                h = (_dot(x_lo.astype(BF16), wb_ref[0:half, :])
                     + _dot(x_hi.astype(BF16), wb_ref[half:, :]) + bgu_ref[...])

                @pl.when(t % 2 == 0)
                def _():
                    hg_ref[rows, :] = jnp.minimum(h, SWIGLU_LIMIT)

                @pl.when(t % 2 == 1)
                def _():
                    hg = hg_ref[rows, :]
                    hl = jnp.clip(h, -SWIGLU_LIMIT, SWIGLU_LIMIT)
                    act_ref[t // 2, rows, :] = (hg * jax.nn.sigmoid(SWIGLU_ALPHA * hg) * (hl + 1.0)).astype(BF16)

    @pl.when(jnp.logical_and(t >= MOE_T1, n_i > 0))
    def _down():
        t2 = t - MOE_T1
        oslot = t2 % 2
        wdl = wdl_ref[...].astype(BF16)
        wdh = wdh_ref[...].astype(BF16)
        col0 = pl.multiple_of(t2 * MOE_TN, MOE_TN)

        def result_copy(slot_, s):
            return pltpu.make_async_copy(
                obuf.at[slot_, pl.ds(s * MOE_SB, MOE_SB)],
                y_hbm.at[pl.ds(pl.multiple_of(prow_ref[i] + s * MOE_SB, MOE_SB), MOE_SB),
                         pl.ds(col0, MOE_TN)], osem.at[slot_])

        @sub_blocks
        def _(s0, ns):
                for s in range(s0, min(s0 + ns, MOE_ISUB)):
                    issue_down(t2, s)
                rows = slice(s0 * MOE_SB, (s0 + ns) * MOE_SB)
                y_lo = bdl_ref[...]
                y_hi = bdh_ref[...]
                for k in range(D_EXPERT // MOE_KD):
                    a = act_ref[k, rows, :]
                    y_lo = y_lo + _dot(a, wdl[k * MOE_KD:(k + 1) * MOE_KD, :])
                    y_hi = y_hi + _dot(a, wdh[k * MOE_KD:(k + 1) * MOE_KD, :])
                obuf[oslot, rows, :] = _pack_halves(jnp.concatenate([y_lo, y_hi], axis=1))
                for s in range(s0, s0 + ns):
                    result_copy(oslot, s).start()

        for s in range(MOE_NSUB):
            @pl.when(jnp.logical_and(s < nsub_i, t2 >= 1))
            def _():
                result_copy(1 - oslot, s).wait()

            @pl.when(jnp.logical_and(s < nsub_i, t2 == MOE_T2 - 1))
            def _():
                result_copy(oslot, s).wait()

    @pl.when(jnp.logical_and(t == MOE_T1 + MOE_T2 - 1, n_i > 0))
    def _missing_sub_block_shares():
        for s in range(1, MOE_ISUB):
            @pl.when(s >= nsub_i)
            def _():
                def gate_up_share(t1, c):
                    issue_gate_up(t1, s)
                    return c

                def down_share(t2, c):
                    issue_down(t2, s)
                    return c
                lax.fori_loop(0, MOE_T1, gate_up_share, 0)
                lax.fori_loop(0, MOE_T2, down_share, 0)

    @pl.when(jnp.logical_and(i == nv_ref[0] - 1, t == MOE_T1 + MOE_T2 - 1))
    def _zero_tail():
        zslot = 1 - slot
        slot_rows(zslot).wait()
        xbuf[zslot, 0:MOE_SB, :] = jnp.zeros((MOE_SB, half), U32)

        def tail_copy(j):
            return pltpu.make_async_copy(
                xbuf.at[zslot, pl.ds(0, MOE_SB)],
                y_hbm.at[pl.ds(pl.multiple_of(j * MOE_SB, MOE_SB), MOE_SB)], xsem.at[zslot])

        def start(j, c):
            tail_copy(j).start()
            return c

        def wait(j, c):
            tail_copy(j).wait()
            return c

        first = nv_ref[1] // MOE_SB
        total = y_hbm.shape[0] // MOE_SB
        lax.fori_loop(first, total, start, 0)
        lax.fori_loop(first, total, wait, 0)


def _moe_experts(xq, blk_exp, blk_n, blk_prow, n_valid_blocks, ids, w_gu, b_gu, w_down, b_down, layer,
                 n_rows_out):
    nb = ids.shape[0]
    t_last = MOE_T1 + MOE_T2 - 1
    half = D_MODEL // 2
    hi_blk = half // MOE_TN

    def src(i, nv):
        return jnp.maximum(jnp.minimum(i, nv[0] - 1), 0)

    def tt(i, t, nv):
        return jnp.where(i < nv[0], t, t_last)

    def t1(i, t, nv):
        return jnp.minimum(tt(i, t, nv), MOE_T1 - 1)

    def t2(i, t, nv):
        return jnp.maximum(tt(i, t, nv) - MOE_T1, 0)

    ids_cur_map = lambda i, t, e, n, p, nv: (src(i, nv), 0, 0)
    ids_nxt_map = lambda i, t, e, n, p, nv: (jnp.minimum(src(i, nv) + 1, nb - 1), 0, 0)
    def gu_col(i, t, nv):
        tt1 = t1(i, t, nv)
        return (tt1 % 2) * (D_EXPERT // MOE_TH) + tt1 // 2

    wgu_map = lambda i, t, e, n, p, nv: (layer, e[i], 0, gu_col(i, t, nv))
    wdl_map = lambda i, t, e, n, p, nv: (layer, e[i], 0, t2(i, t, nv))
    wdh_map = lambda i, t, e, n, p, nv: (layer, e[i], 0, hi_blk + t2(i, t, nv))
    grid_spec = pltpu.PrefetchScalarGridSpec(
        num_scalar_prefetch=4,
        grid=(nb, MOE_T1 + MOE_T2),
        in_specs=[
            pl.BlockSpec((None, 1, MOE_TM), ids_cur_map, memory_space=pltpu.SMEM),
            pl.BlockSpec((None, 1, MOE_TM), ids_nxt_map, memory_space=pltpu.SMEM),
            pl.BlockSpec(memory_space=pl.ANY),
            pl.BlockSpec((None, None, D_MODEL, MOE_TH), wgu_map),
            pl.BlockSpec((None, None, 1, MOE_TH), wgu_map),
            pl.BlockSpec((None, None, D_EXPERT, MOE_TN), wdl_map),
            pl.BlockSpec((None, None, D_EXPERT, MOE_TN), wdh_map),
            pl.BlockSpec((None, None, 1, MOE_TN), wdl_map),
            pl.BlockSpec((None, None, 1, MOE_TN), wdh_map),
        ],
        out_specs=pl.BlockSpec(memory_space=pl.ANY),
        scratch_shapes=[
            pltpu.VMEM((2, MOE_TM, half), U32),
            pltpu.VMEM((D_EXPERT // MOE_KD, MOE_TM, MOE_KD), BF16),
            pltpu.VMEM((MOE_TM, MOE_TH), F32),
            pltpu.VMEM((D_MODEL, MOE_TH), BF16),
            pltpu.VMEM((2, MOE_TM, MOE_TN), U32),
            pltpu.SemaphoreType.DMA((2,)),
            pltpu.SemaphoreType.DMA((2,)),
        ],
    )
    bgu = b_gu.reshape(DEPTH, N_EXPERTS, 1, 2 * D_EXPERT)
    ids3 = ids.reshape(nb, 1, MOE_TM)
    bdn = b_down.reshape(DEPTH, N_EXPERTS, 1, D_MODEL)
    return pl.pallas_call(
        _moe_kernel,
        grid_spec=grid_spec,
        out_shape=jax.ShapeDtypeStruct((n_rows_out, half), U32),
        compiler_params=_params("arbitrary", "arbitrary"),
        name="moe_experts",
    )(blk_exp, blk_n, blk_prow, n_valid_blocks, ids3, ids3, xq, w_gu, bgu,
      w_down, w_down, bdn, bdn)


LNC_TM = 128


def _ln_combine_kernel(ids_cur, ids_nxt, h_ref, gate_ref, g_ref, b_ref, y_hbm, ho_ref, hb_ref, ybuf, sem):
    i = pl.program_id(0)
    tm = h_ref.shape[0]
    w = h_ref.shape[1] // 2
    rows = TOP_K * tm
    slot = i % 2

    def row_copy(ids_ref, slot_, idx):
        return pltpu.make_async_copy(y_hbm.at[pl.ds(ids_ref[0, idx], 1)],
                                     ybuf.at[slot_, pl.ds(idx, 1)], sem.at[slot_])

    def tile_rows(slot_):
        return pltpu.make_async_copy(y_hbm.at[pl.ds(0, rows)], ybuf.at[slot_], sem.at[slot_])

    @pl.when(i == 0)
    def _():
        def body(u, c):
            for r in range(MOE_UNIT):
                row_copy(ids_cur, 0, u * MOE_UNIT + r).start()
            return c
        lax.fori_loop(0, rows // MOE_UNIT, body, 0)

    tile_rows(slot).wait()
    for idx in range(rows):
        row_copy(ids_nxt, 1 - slot, idx).start()

    x_lo = DEEPNORM_ALPHA * h_ref[:, :w]
    x_hi = DEEPNORM_ALPHA * h_ref[:, w:]
    for k in range(TOP_K):
        lo, hi = _unpack_halves(ybuf[slot, k * tm:(k + 1) * tm, :])
        gk = gate_ref[:, k:k + 1]
        x_lo = x_lo + gk * lo
        x_hi = x_hi + gk * hi
    x = jnp.concatenate([x_lo, x_hi], axis=1)
    mu = jnp.mean(x, axis=-1, keepdims=True)
    xc = x - mu
    var = jnp.mean(xc * xc, axis=-1, keepdims=True)
    hn = xc * lax.rsqrt(var + LN_EPS) * g_ref[...] + b_ref[...]
    ho_ref[...] = hn
    hb_ref[...] = hn.astype(BF16)

    @pl.when(i == pl.num_programs(0) - 1)
    def _():
        tile_rows(1 - slot).wait()


def _ln_combine(h, yq, dest_tiles, gates, ln_g, ln_b):
    n_tok, d = h.shape
    tm = LNC_TM
    nblk = n_tok // tm
    row = lambda i: (i, 0)
    fixed = lambda i: (0, 0)
    dest3 = dest_tiles.reshape(nblk, 1, TOP_K * tm)
    return pl.pallas_call(
        _ln_combine_kernel,
        grid=(nblk,),
        in_specs=[
            pl.BlockSpec((None, 1, TOP_K * tm), lambda i: (i, 0, 0), memory_space=pltpu.SMEM),
            pl.BlockSpec((None, 1, TOP_K * tm), lambda i: (jnp.minimum(i + 1, nblk - 1), 0, 0),
                         memory_space=pltpu.SMEM),
            pl.BlockSpec((tm, d), row),
            pl.BlockSpec((tm, SLOT_LANES), row),
            pl.BlockSpec((1, d), fixed), pl.BlockSpec((1, d), fixed),
            pl.BlockSpec(memory_space=pl.ANY),
        ],
        out_specs=[pl.BlockSpec((tm, d), row), pl.BlockSpec((tm, d), row)],
        out_shape=[jax.ShapeDtypeStruct((n_tok, d), F32), jax.ShapeDtypeStruct((n_tok, d), BF16)],
        scratch_shapes=[pltpu.VMEM((2, TOP_K * tm, d // 2), U32), pltpu.SemaphoreType.DMA((2,))],
        compiler_params=_params("arbitrary"),
        name="ln_combine",
    )(dest3, dest3, h, gates, ln_g.reshape(1, d), ln_b.reshape(1, d), yq)


def _moe_layer(h, hq, top_idx, gates, rank, counts, w_gu, b_gu, w_down, b_down, layer, ln_g, ln_b):
    n_tok = h.shape[0]
    n_assign = n_tok * TOP_K
    nb = N_EXPERTS + n_assign // MOE_TM
    n_rows_out = n_assign + N_EXPERTS * MOE_SB
    i32 = jnp.int32
    cnt = counts[0].astype(i32)
    pcnt = (cnt + MOE_SB - 1) // MOE_SB * MOE_SB
    pstart = jnp.cumsum(pcnt) - pcnt
    nblk = (cnt + MOE_TM - 1) // MOE_TM
    bend = jnp.cumsum(nblk)
    bfirst = bend - nblk
    n_valid = bend[-1]
    e_flat = top_idx[:, :TOP_K].reshape(-1)
    rank_flat = rank[:, :TOP_K].reshape(-1)
    tok = jnp.arange(n_assign, dtype=i32) // TOP_K
    dest = pstart[e_flat] + rank_flat
    blk = jnp.arange(nb, dtype=i32)
    src = jnp.maximum(jnp.minimum(blk, n_valid - 1), 0)
    be = jnp.minimum(jnp.sum((bend[None, :] <= src[:, None]).astype(i32), axis=1), N_EXPERTS - 1)
    b_in = src - bfirst[be]
    blk_n = jnp.where(blk < n_valid, jnp.clip(cnt[be] - b_in * MOE_TM, 0, MOE_TM), 0).astype(i32)
    blk_prow = (pstart[be] + b_in * MOE_TM).astype(i32)
    id_pos = (bfirst[e_flat] + rank_flat // MOE_TM) * MOE_TM + rank_flat % MOE_TM
    ids = jnp.zeros((nb * MOE_TM,), i32).at[id_pos].set(tok).reshape(nb, MOE_TM)
    used_rows = pstart[-1] + pcnt[-1]
    yq = _moe_experts(hq, be.astype(i32), blk_n, blk_prow, jnp.stack([n_valid, used_rows]).astype(i32), ids,
                      w_gu, b_gu, w_down, b_down, layer, n_rows_out)
    tm = LNC_TM
    dest_tiles = dest.reshape(n_tok // tm, tm, TOP_K).transpose(0, 2, 1).reshape(n_tok // tm, TOP_K * tm)
    return _ln_combine(h, yq, dest_tiles, gates, ln_g, ln_b)


def kernel(x, positions, ln_g, ln_b, hy_w_in, hy_w_out, gm_ln_g, gm_ln_b, gm_w_s, gm_b_s, ssm_lam_re, ssm_lam_im, ssm_log_dt, ssm_b_re, ssm_b_im, ssm_c_re, ssm_c_im, ssm_d, ssm_w_glu, ssm_b_glu, mla_w_in, mla_q_norm_g, mla_kv_norm_g, mla_w_uq, mla_w_ukv, mla_w_o, moe_w_router, moe_b_router, moe_w_gu, moe_b_gu, moe_w_down, moe_b_down):
    batch, seq, d = x.shape
    n_tok = batch * seq
    h = x.reshape(n_tok, d)
    hb = h.astype(BF16)
    for layer in range(DEPTH):
        i = layer // 2
        if layer % 2 == 0:
            z_gm = _mm([hb], hy_w_in[i], col0=0, n_cols=2 * GM_WIDTH, tm=1024, tn=256,
                       out_dtype=BF16, epilogue="gelu", name="hy_in_gm")
            z_ssm = _mm([hb], hy_w_in[i], col0=2 * GM_WIDTH, n_cols=SSM_WIDTH, tm=1024, tn=256,
                        out_dtype=F32, name="hy_in_ssm")
            y_gm = _spatial_gating(z_gm, gm_ln_g[i], gm_ln_b[i], gm_w_s[i], gm_b_s[i])
            tables = _s5_tables(ssm_lam_re[i], ssm_lam_im[i], ssm_log_dt[i], ssm_b_re[i], ssm_b_im[i],
                                ssm_c_re[i], ssm_c_im[i], ssm_d[i], seq)
            y_act = _s5_mixer(z_ssm, batch, seq, tables)
            y_ssm = _mm([y_act], ssm_w_glu[i], tm=1024, tn=512, out_dtype=BF16, epilogue="glu",
                        bias=ssm_b_glu[i], mul=y_act, name="s5_glu")
            mix = _mm([y_gm, y_ssm], hy_w_out[i], tm=1024, tn=256, out_dtype=F32, name="hy_out")
        else:
            w_in = mla_w_in[i]
            n_main = MLA_Q_RANK + MLA_KV_RANK
            c_main = _mm([hb], w_in, col0=0, n_cols=n_main, tm=1024, tn=256, out_dtype=F32,
                         name="mla_in")
            k_rope = _mm([hb], w_in[:, n_main:], tm=1024, tn=MLA_ROPE, out_dtype=F32, name="mla_in_rope")
            tab, k_rope = _rope_tables(positions, k_rope)
            q = _mm([(c_main, MLA_Q_RANK, 0)], mla_w_uq[i], tm=1024, tn=768, out_dtype=BF16,
                    prologue="rms", gain=mla_q_norm_g[i], name="mla_uq")
            kv = _mm([(c_main, MLA_KV_RANK, MLA_Q_RANK // MLA_KV_RANK)], mla_w_ukv[i], tm=1024, tn=1024,
                     out_dtype=BF16, prologue="rms", gain=mla_kv_norm_g[i], name="mla_ukv")
            o = _attention(q, kv, k_rope, tab, batch, seq)
            mix = _mm([o], mla_w_o[i], tm=1024, tn=256, out_dtype=F32, name="mla_out")
        h, hq, top_idx, gates, rank, counts = _ln_router(
            h, mix, ln_g[layer, 0], ln_b[layer, 0], moe_w_router[layer], moe_b_router[layer])
        h, hb = _moe_layer(h, hq, top_idx, gates, rank, counts, moe_w_gu, moe_b_gu,
                           moe_w_down, moe_b_down, layer, ln_g[layer, 1], ln_b[layer, 1])
    return h.reshape(batch, seq, d)
```

```python
import functools
import math

import numpy as np
import jax
import jax.numpy as jnp
from jax import lax
from jax.experimental import pallas as pl
from jax.experimental.pallas import tpu as pltpu

F32 = jnp.float32
BF16 = jnp.bfloat16
U32 = jnp.uint32
HI_MASK = np.uint32(0xFFFF0000)

D_MODEL = 4096
DEPTH = 2
CHUNK = 64
DEEPNORM_ALPHA = (2 * DEPTH) ** 0.25
LN_EPS = 1e-5
RMS_EPS = 1e-6
LANES = 128

GM_WIDTH = 2048
GM_GROUPS = 8
GM_GROUP_DIM = 256
GM_BLOCK = 128
SSM_WIDTH = 2048
SSM_P = 16
SSM_GROUPS = 128
SSM_N = 64
SSM_T = 16
SSM_OCT = LANES // SSM_P
SSM_NOCT = SSM_GROUPS // SSM_OCT
SSM_PAIRS = SSM_OCT // 2
SSM_SW = SSM_OCT * SSM_N

MLA_HEADS = 32
MLA_Q_RANK = 1024
MLA_KV_RANK = 512
MLA_NOPE = 128
MLA_ROPE = 64
MLA_V = 128
ROPE_THETA = 10000.0

N_EXPERTS = 32
TOP_K = 4
D_EXPERT = 1024
SWIGLU_LIMIT = 7.0
SWIGLU_ALPHA = 1.702

VMEM_LIMIT_BYTES = 56 * 1024 * 1024


def _params(*sem):
    return pltpu.CompilerParams(dimension_semantics=sem, vmem_limit_bytes=VMEM_LIMIT_BYTES)


def _dot(a, b):
    return jnp.dot(a, b, preferred_element_type=F32)


def _dot_nt(a, b):
    return lax.dot_general(a, b, (((1,), (1,)), ((), ())), preferred_element_type=F32)


MM_W_STREAMS = 2


def _mm_kernel(*refs, prologue, epilogue, n_x):
    it = iter(refs)
    x_refs = [next(it) for _ in range(n_x)]
    w_refs = [next(it) for _ in range(MM_W_STREAMS)]
    g_ref = next(it) if prologue == "rms" else None
    b_ref = next(it) if epilogue == "glu" else None
    y_ref = next(it) if epilogue == "glu" else None
    o_ref = next(it)
    xs_ref = next(it) if prologue == "rms" else None

    if prologue == "rms":
        @pl.when(pl.program_id(1) == 0)
        def _():
            xf = x_refs[0][...].astype(F32)
            ms = jnp.mean(xf * xf, axis=-1, keepdims=True)
            xs_ref[...] = (xf * lax.rsqrt(ms + RMS_EPS) * g_ref[...]).astype(BF16)
        xs = [xs_ref[...]]
    else:
        xs = [r[...].astype(BF16) for r in x_refs]

    w = jnp.concatenate([r[...].astype(BF16) for r in w_refs], axis=0)
    acc = None
    k0 = 0
    for x in xs:
        kk = x.shape[1]
        part = _dot(x, w[k0:k0 + kk, :])
        acc = part if acc is None else acc + part
        k0 += kk
    if epilogue == "gelu":
        acc = jax.nn.gelu(acc)
    elif epilogue == "glu":
        acc = y_ref[...].astype(F32) * jax.nn.sigmoid(acc + b_ref[...])
    o_ref[...] = acc.astype(o_ref.dtype)


def _mm(xs, w, *, col0=0, n_cols=None, tm, tn, out_dtype, prologue=None, gain=None,
        epilogue=None, bias=None, mul=None, name):
    xs = [(x, x.shape[1], 0) if not isinstance(x, tuple) else x for x in xs]
    m = xs[0][0].shape[0]
    k_total = w.shape[0]
    n_cols = w.shape[1] - col0 if n_cols is None else n_cols
    tn = min(tn, n_cols)
    assert m % tm == 0 and n_cols % tn == 0 and col0 % tn == 0
    cb0 = col0 // tn
    assert sum(kw for _, kw, _ in xs) == k_total
    in_specs = [pl.BlockSpec((tm, kw), functools.partial(lambda i, j, cb: (i, cb), cb=cb))
                for _, kw, cb in xs]
    for part in range(MM_W_STREAMS):
        in_specs.append(pl.BlockSpec((k_total // MM_W_STREAMS, tn),
                                     functools.partial(lambda i, j, part: (part, cb0 + j), part=part)))
    args = [x for x, _, _ in xs] + [w] * MM_W_STREAMS
    scratch = []
    if prologue == "rms":
        in_specs.append(pl.BlockSpec((1, k_total), lambda i, j: (0, 0)))
        args.append(gain.reshape(1, k_total))
        scratch.append(pltpu.VMEM((tm, k_total), BF16))
    if epilogue == "glu":
        in_specs.append(pl.BlockSpec((1, tn), lambda i, j: (0, j)))
        in_specs.append(pl.BlockSpec((tm, tn), lambda i, j: (i, j)))
        args += [bias.reshape(1, n_cols), mul]
    return pl.pallas_call(
        functools.partial(_mm_kernel, prologue=prologue, epilogue=epilogue, n_x=len(xs)),
        grid=(m // tm, n_cols // tn),
        in_specs=in_specs,
        out_specs=pl.BlockSpec((tm, tn), lambda i, j: (i, j)),
        out_shape=jax.ShapeDtypeStruct((m, n_cols), out_dtype),
        scratch_shapes=scratch,
        compiler_params=_params("arbitrary", "arbitrary"),
        name=name,
    )(*args)


def _gating_kernel(u_ref, v_ref, g_ref, b_ref, ws_ref, bst_ref, o_ref):
    row = lax.broadcasted_iota(jnp.int32, (GM_BLOCK, GM_BLOCK), 0)
    col = lax.broadcasted_iota(jnp.int32, (GM_BLOCK, GM_BLOCK), 1)
    visible = (col // CHUNK) <= (row // CHUNK)
    for g in range(GM_GROUPS):
        sl = slice(g * GM_GROUP_DIM, (g + 1) * GM_GROUP_DIM)
        v = v_ref[:, sl].astype(F32)
        mu = jnp.mean(v, axis=-1, keepdims=True)
        vc = v - mu
        var = jnp.mean(vc * vc, axis=-1, keepdims=True)
        vn = vc * lax.rsqrt(var + LN_EPS) * g_ref[:, sl] + b_ref[:, sl]
        w = jnp.where(visible, ws_ref[g], 0.0).astype(BF16)
        s = _dot(w, vn.astype(BF16)) + bst_ref[:, g:g + 1]
        o_ref[:, sl] = (u_ref[:, sl].astype(F32) * s).astype(o_ref.dtype)


def _spatial_gating(z_gm, ln_g, ln_b, w_s, b_s):
    n_tok = z_gm.shape[0]
    return pl.pallas_call(
        _gating_kernel,
        grid=(n_tok // GM_BLOCK,),
        in_specs=[
            pl.BlockSpec((GM_BLOCK, GM_WIDTH), lambda i: (i, 0)),
            pl.BlockSpec((GM_BLOCK, GM_WIDTH), lambda i: (i, 1)),
            pl.BlockSpec((1, GM_WIDTH), lambda i: (0, 0)),
            pl.BlockSpec((1, GM_WIDTH), lambda i: (0, 0)),
            pl.BlockSpec((GM_GROUPS, GM_BLOCK, GM_BLOCK), lambda i: (0, 0, 0)),
            pl.BlockSpec((GM_BLOCK, GM_GROUPS), lambda i: (0, 0)),
        ],
        out_specs=pl.BlockSpec((GM_BLOCK, GM_WIDTH), lambda i: (i, 0)),
        out_shape=jax.ShapeDtypeStruct((n_tok, GM_WIDTH), BF16),
        compiler_params=_params("arbitrary"),
        name="spatial_gating",
    )(z_gm, z_gm, ln_g.reshape(1, GM_WIDTH), ln_b.reshape(1, GM_WIDTH), w_s, b_s.T)


def _s5_tables(lam_re, lam_im, log_dt, b_re, b_im, c_re, c_im, d_skip, seq):
    hi = lax.Precision.HIGHEST
    n, p, t = SSM_N, SSM_P, SSM_T
    no, npr = SSM_NOCT, SSM_PAIRS
    dt = jnp.exp(log_dt)[:, None]
    mag = jnp.exp(lam_re * dt)
    ab_re = mag * jnp.cos(lam_im * dt)
    ab_im = mag * jnp.sin(lam_im * dt)
    den = lam_re * lam_re + lam_im * lam_im
    num_re = ab_re - 1.0
    coef_re = (num_re * lam_re + ab_im * lam_im) / den
    coef_im = (ab_im * lam_re - num_re * lam_im) / den
    bb_re = coef_re[..., None] * b_re - coef_im[..., None] * b_im
    bb_im = coef_re[..., None] * b_im + coef_im[..., None] * b_re

    def power(k):
        k = jnp.asarray(k, F32)[..., None, None]
        mk = jnp.exp(k * (lam_re * dt))
        return mk * jnp.cos(k * (lam_im * dt)), mk * jnp.sin(k * (lam_im * dt))

    pw_re, pw_im = power(np.arange(t + 1))
    cp_re = c_re[None] * pw_re[:, :, None, :] - c_im[None] * pw_im[:, :, None, :]
    cp_im = c_re[None] * pw_im[:, :, None, :] + c_im[None] * pw_re[:, :, None, :]
    kern = (jnp.einsum('tgpn,gnq->gtpq', cp_re[:t], bb_re, precision=hi)
            - jnp.einsum('tgpn,gnq->gtpq', cp_im[:t], bb_im, precision=hi))
    eye_o = jnp.eye(SSM_OCT, dtype=F32)
    eye_2 = jnp.eye(2, dtype=F32)
    kbd = jnp.einsum('oatpq,ab->otaqbp', kern.reshape(no, SSM_OCT, t, p, p), eye_o)
    kbd = kbd.reshape(no, t, LANES, LANES)
    rv_re, rv_im = pw_re[t - 1::-1], pw_im[t - 1::-1]
    st_re = rv_re[..., None] * bb_re[None] - rv_im[..., None] * bb_im[None]
    st_im = rv_re[..., None] * bb_im[None] + rv_im[..., None] * bb_re[None]
    st = jnp.stack([st_re, st_im], 0).reshape(2, t, no, npr, 2, n, p)
    w_st = jnp.einsum('rjoxanq,ab->ojxraqbn', st, eye_2).reshape(no, t, npr, 2, 2 * p, 2 * n)
    wc = jnp.stack([cp_re[1:], -cp_im[1:]], 0).reshape(2, t, no, npr, 2, p, n)
    w_ct = jnp.einsum('rioxapn,ab->oixrapbn', wc, eye_2).reshape(no, t, npr, 2, 2 * p, 2 * n)
    n_lvl = int(math.log2(seq // t))
    lv_re, lv_im = power(t * 2 ** np.arange(n_lvl))
    lr = lv_re.reshape(n_lvl, no, SSM_SW)
    li = lv_im.reshape(n_lvl, no, SSM_SW)
    lvl = jnp.stack([jnp.concatenate([lr, lr], -1), jnp.concatenate([-li, li], -1)], axis=1)
    lvl = lvl.transpose(2, 0, 1, 3).reshape(no, 2 * n_lvl, 2 * SSM_SW)
    d_vec = jnp.tile(d_skip.reshape(no, 1, LANES), (1, 1, t))
    return kbd.astype(BF16), w_st.astype(BF16), w_ct.astype(BF16), lvl, d_vec


def _s5_kernel(u_ref, kbd_ref, wst_ref, wct_ref, lvl_ref, d_ref, o_ref, toe_ref, st_ref, ct_ref,
               *, n_chunks, n_lvl):
    t = SSM_T

    @pl.when(pl.program_id(0) == 0)
    def _():
        toe_ref[...] = jnp.zeros_like(toe_ref)
        st_ref[...] = jnp.zeros_like(st_ref)
        ct_ref[...] = jnp.zeros_like(ct_ref)

    for i in range(t):
        for j in range(i + 1):
            toe_ref[j * LANES:(j + 1) * LANES, i * LANES:(i + 1) * LANES] = kbd_ref[i - j]
    for j in range(t):
        for x in range(SSM_PAIRS):
            for r in range(2):
                rows = slice(j * LANES + x * 2 * SSM_P, j * LANES + (x + 1) * 2 * SSM_P)
                lanes = slice(r * SSM_SW + x * LANES, r * SSM_SW + (x + 1) * LANES)
                st_ref[rows, lanes] = wst_ref[j, x, r]
                ct_ref[rows, lanes] = wct_ref[j, x, r]

    u = u_ref[...]
    ub = u.astype(BF16)
    h = _dot(ub, st_ref[...])
    cidx = lax.broadcasted_iota(jnp.int32, (u.shape[0], 1), 0) % n_chunks
    for lv in range(n_lvl):
        d = 1 << lv
        a_rr = lvl_ref[2 * lv:2 * lv + 1, :]
        a_is = lvl_ref[2 * lv + 1:2 * lv + 2, :]
        sh = jnp.where(cidx >= d, pltpu.roll(h, d, 0), 0.0)
        h = h + sh * a_rr + pltpu.roll(sh, SSM_SW, 1) * a_is
    h_prev = jnp.where(cidx >= 1, pltpu.roll(h, 1, 0), 0.0)
    y = _dot(ub, toe_ref[...]) + _dot_nt(h_prev.astype(BF16), ct_ref[...]) + d_ref[...] * u
    o_ref[...] = jax.nn.gelu(y).astype(o_ref.dtype)


def _s5_mixer(z_ssm, batch, seq, tables):
    kbd, w_st, w_ct, lvl, d_vec = tables
    t, no = SSM_T, SSM_NOCT
    n_chunks = seq // t
    rows = batch * n_chunks
    width = t * LANES
    n_lvl = lvl.shape[1] // 2
    u = z_ssm.reshape(rows, t, no, LANES).transpose(2, 0, 1, 3).reshape(no, rows, width)
    y = pl.pallas_call(
        functools.partial(_s5_kernel, n_chunks=n_chunks, n_lvl=n_lvl),
        grid=(no,),
        in_specs=[
            pl.BlockSpec((None, rows, width), lambda i: (i, 0, 0)),
            pl.BlockSpec((None, t, LANES, LANES), lambda i: (i, 0, 0, 0)),
            pl.BlockSpec((None, t, SSM_PAIRS, 2, 2 * SSM_P, LANES), lambda i: (i, 0, 0, 0, 0, 0)),
            pl.BlockSpec((None, t, SSM_PAIRS, 2, 2 * SSM_P, LANES), lambda i: (i, 0, 0, 0, 0, 0)),
            pl.BlockSpec((None, 2 * n_lvl, 2 * SSM_SW), lambda i: (i, 0, 0)),
            pl.BlockSpec((None, 1, width), lambda i: (i, 0, 0)),
        ],
        out_specs=pl.BlockSpec((None, rows, width), lambda i: (i, 0, 0)),
        out_shape=jax.ShapeDtypeStruct((no, rows, width), BF16),
        scratch_shapes=[
            pltpu.VMEM((width, width), BF16),
            pltpu.VMEM((width, 2 * SSM_SW), BF16),
            pltpu.VMEM((width, 2 * SSM_SW), BF16),
        ],
        compiler_params=_params("arbitrary"),
        name="s5_mixer",
    )(u, kbd, w_st, w_ct, lvl, d_vec)
    return y.reshape(no, rows, t, LANES).transpose(1, 2, 0, 3).reshape(batch * seq, SSM_WIDTH)


def _rope_kernel(pos_ref, freq_ref, kr_ref, tab_ref, kro_ref):
    ang = pos_ref[...].astype(F32) * freq_ref[...]
    lane = lax.broadcasted_iota(jnp.int32, ang.shape, 1)
    sin = jnp.sin(ang)
    tab = jnp.where(lane < MLA_ROPE, jnp.cos(ang), jnp.where(lane < MLA_ROPE + MLA_ROPE // 2, -sin, sin))
    tab_ref[...] = tab
    kr = kr_ref[...]
    half = MLA_ROPE // 2
    sw = jnp.concatenate([kr[:, half:], kr[:, :half]], axis=1)
    kro_ref[...] = (kr * tab[:, :MLA_ROPE] + sw * tab[:, MLA_ROPE:]).astype(kro_ref.dtype)


def _rope_tables(positions, k_rope):
    n_tok = k_rope.shape[0]
    tm = 512
    inv_freq = ROPE_THETA ** (-np.arange(0, MLA_ROPE, 2, dtype=np.float64) / MLA_ROPE)
    freq = jnp.asarray(np.tile(inv_freq, 4)[None, :], F32)
    return pl.pallas_call(
        _rope_kernel,
        grid=(n_tok // tm,),
        in_specs=[
            pl.BlockSpec((tm, 1), lambda i: (i, 0)),
            pl.BlockSpec((1, 2 * MLA_ROPE), lambda i: (0, 0)),
            pl.BlockSpec((tm, MLA_ROPE), lambda i: (i, 0)),
        ],
        out_specs=[
            pl.BlockSpec((tm, 2 * MLA_ROPE), lambda i: (i, 0)),
            pl.BlockSpec((tm, MLA_ROPE), lambda i: (i, 0)),
        ],
        out_shape=[
            jax.ShapeDtypeStruct((n_tok, 2 * MLA_ROPE), F32),
            jax.ShapeDtypeStruct((n_tok, MLA_ROPE), BF16),
        ],
        compiler_params=_params("arbitrary"),
        name="rope_tables",
    )(positions.reshape(n_tok, 1), freq, k_rope)


ATT_TK = 256
ATT_TQ = 2 * ATT_TK
ATT_DK = 2 * MLA_NOPE


def _attn_kernel(q_ref, tab_ref, k0_ref, v0_ref, k1_ref, v1_ref, kr_ref, o_ref, kc_ref):
    qi = pl.program_id(2)
    hd = MLA_NOPE + MLA_ROPE
    scale = hd ** -0.5 * math.log2(math.e)
    half = MLA_ROPE // 2
    tk = ATT_TK

    @pl.when(qi == 0)
    def _():
        for hh, k_ref in enumerate((k0_ref, k1_ref)):
            kc_ref[hh, :, 0:MLA_NOPE] = k_ref[...]
            kc_ref[hh, :, MLA_NOPE:hd] = kr_ref[...]
            kc_ref[hh, :, hd:] = jnp.zeros((kr_ref.shape[0], ATT_DK - hd), BF16)

    q = q_ref[...]
    cosf = tab_ref[:, :MLA_ROPE]
    sinf = tab_ref[:, MLA_ROPE:]
    qc = []
    for hh in range(2):
        qn = q[:, hh * hd:hh * hd + MLA_NOPE].astype(F32) * scale
        qr = q[:, hh * hd + MLA_NOPE:(hh + 1) * hd].astype(F32)
        sw = jnp.concatenate([qr[:, half:], qr[:, :half]], axis=1)
        qr = (qr * cosf + sw * sinf) * scale
        pad = jnp.zeros((ATT_TQ, ATT_DK - hd), F32)
        qc.append(jnp.concatenate([qn, qr, pad], axis=1).astype(BF16))
    v_refs = (v0_ref, v1_ref)
    row = lax.broadcasted_iota(jnp.int32, (tk, tk), 0)
    col = lax.broadcasted_iota(jnp.int32, (tk, tk), 1)
    visible = (col // CHUNK) <= (row // CHUNK)

    def update(state, hh, sub, j, masked):
        m, l, acc = state
        start = pl.multiple_of(j * tk, tk)
        ks = kc_ref[hh, pl.ds(start, tk), :]
        vs = v_refs[hh][pl.ds(start, tk), :]
        s = _dot_nt(qc[hh][sub * tk:(sub + 1) * tk], ks)
        if masked:
            s = jnp.where(visible, s, -1e30)
        m_new = jnp.maximum(m, jnp.max(s, axis=-1, keepdims=True))
        alpha = jnp.exp2(m - m_new)
        p = jnp.exp2(s - m_new)
        l = alpha * l + jnp.sum(p, axis=-1, keepdims=True)
        acc = alpha * acc + _dot(p.astype(BF16), vs)
        return m_new, l, acc

    chains = [(hh, sub) for hh in range(2) for sub in range(2)]

    def body(jj, states):
        states = tuple(update(st, hh, sub, 2 * jj, False) for st, (hh, sub) in zip(states, chains))
        return tuple(update(st, hh, sub, 2 * jj + 1, False) for st, (hh, sub) in zip(states, chains))

    init = tuple((jnp.full((tk, 1), -1e30, F32), jnp.zeros((tk, 1), F32), jnp.zeros((tk, MLA_V), F32))
                 for _ in chains)
    states = list(lax.fori_loop(0, qi, body, init))
    for c, (hh, sub) in enumerate(chains):
        st = update(states[c], hh, sub, 2 * qi, masked=(sub == 0))
        if sub == 1:
            st = update(st, hh, sub, 2 * qi + 1, masked=True)
        _, l, acc = st
        o_ref[sub * tk:(sub + 1) * tk, hh * MLA_V:(hh + 1) * MLA_V] = (acc / l).astype(o_ref.dtype)


def _attention(q, kv, k_rope, tab, batch, seq):
    n_tok = batch * seq
    nq = seq // ATT_TQ
    kvb = lambda off: pl.BlockSpec((seq, MLA_NOPE), lambda b, hp, qi: (b, 4 * hp + off))
    return pl.pallas_call(
        _attn_kernel,
        grid=(batch, MLA_HEADS // 2, nq),
        in_specs=[
            pl.BlockSpec((ATT_TQ, 2 * (MLA_NOPE + MLA_ROPE)), lambda b, hp, qi: (b * nq + qi, hp)),
            pl.BlockSpec((ATT_TQ, 2 * MLA_ROPE), lambda b, hp, qi: (b * nq + qi, 0)),
            kvb(0), kvb(1), kvb(2), kvb(3),
            pl.BlockSpec((seq, MLA_ROPE), lambda b, hp, qi: (b, 0)),
        ],
        out_specs=pl.BlockSpec((ATT_TQ, 2 * MLA_V), lambda b, hp, qi: (b * nq + qi, hp)),
        out_shape=jax.ShapeDtypeStruct((n_tok, MLA_HEADS * MLA_V), BF16),
        scratch_shapes=[pltpu.VMEM((2, seq, ATT_DK), BF16)],
        compiler_params=_params("arbitrary", "arbitrary", "arbitrary"),
        name="mla_attention",
    )(q, tab, kv, kv, kv, kv, k_rope)


LNR_TM = 256
SLOT_LANES = 128


def _pack_halves(x):
    w = x.shape[1] // 2
    bits = lax.bitcast_convert_type(x.astype(BF16).astype(F32), U32)
    return (bits[:, :w] >> 16) | (bits[:, w:] & HI_MASK)


def _unpack_halves(p):
    return (lax.bitcast_convert_type(p << 16, F32), lax.bitcast_convert_type(p & HI_MASK, F32))


def _ln_router_kernel(h_ref, mix_ref, g_ref, b_ref, wr_ref, br_ref,
                      ho_ref, hq_ref, idx_ref, gate_ref, rank_ref, cnt_ref, carry_ref):
    @pl.when(pl.program_id(0) == 0)
    def _():
        carry_ref[...] = jnp.zeros_like(carry_ref)

    x = DEEPNORM_ALPHA * h_ref[...] + mix_ref[...]
    mu = jnp.mean(x, axis=-1, keepdims=True)
    xc = x - mu
    var = jnp.mean(xc * xc, axis=-1, keepdims=True)
    hn = xc * lax.rsqrt(var + LN_EPS) * g_ref[...] + b_ref[...]
    ho_ref[...] = hn
    hq_ref[...] = _pack_halves(hn)

    logits =jnp.dot(hn, wr_ref[...], precision=lax.Precision.HIGHEST,
                     preferred_element_type=F32) + br_ref[...]
    tm = logits.shape[0]
    lane_e = lax.broadcasted_iota(jnp.int32, (tm, N_EXPERTS), 1)
    lane_s = lax.broadcasted_iota(jnp.int32, (tm, SLOT_LANES), 1)
    work = logits
    sel = jnp.zeros((tm, N_EXPERTS), F32)
    top_v, top_i, hot = [], [], []
    for _ in range(TOP_K):
        mx = jnp.max(work, axis=-1, keepdims=True)
        ix = jnp.min(jnp.where(work == mx, lane_e, N_EXPERTS), axis=-1, keepdims=True)
        oh = lane_e == ix
        work = jnp.where(oh, -jnp.inf, work)
        sel = sel + oh.astype(F32)
        top_v.append(mx)
        top_i.append(ix)
        hot.append(oh)
    ex = [jnp.exp(v - top_v[0]) for v in top_v]
    den = ex[0] + ex[1] + ex[2] + ex[3]

    r = lax.broadcasted_iota(jnp.int32, (tm, tm), 0)
    c = lax.broadcasted_iota(jnp.int32, (tm, tm), 1)
    strict = jnp.where(c < r, 1.0, 0.0).astype(BF16)
    prefix = _dot(strict, sel.astype(BF16)) + carry_ref[0:1, :]
    carry_ref[0:1, :] = carry_ref[0:1, :] + jnp.sum(sel, axis=0, keepdims=True)
    cnt_ref[...] = jnp.broadcast_to(carry_ref[0:1, :], cnt_ref.shape)

    idx_o = jnp.zeros((tm, SLOT_LANES), jnp.int32)
    gate_o = jnp.zeros((tm, SLOT_LANES), F32)
    rank_o = jnp.zeros((tm, SLOT_LANES), jnp.int32)
    for k in range(TOP_K):
        rk = jnp.sum(jnp.where(hot[k], prefix, 0.0), axis=-1, keepdims=True)
        idx_o = jnp.where(lane_s == k, top_i[k], idx_o)
        gate_o = jnp.where(lane_s == k, ex[k] / den, gate_o)
        rank_o = jnp.where(lane_s == k, rk.astype(jnp.int32), rank_o)
    idx_ref[...] = idx_o
    gate_ref[...] = gate_o
    rank_ref[...] = rank_o


def _ln_router(h, mix, ln_g, ln_b, w_router, b_router):
    n_tok, d = h.shape
    tm = LNR_TM
    row = lambda i: (i, 0)
    fixed = lambda i: (0, 0)
    return pl.pallas_call(
        _ln_router_kernel,
        grid=(n_tok // tm,),
        in_specs=[
            pl.BlockSpec((tm, d), row), pl.BlockSpec((tm, d), row),
            pl.BlockSpec((1, d), fixed), pl.BlockSpec((1, d), fixed),
            pl.BlockSpec((d, N_EXPERTS), fixed), pl.BlockSpec((1, N_EXPERTS), fixed),
        ],
        out_specs=[
            pl.BlockSpec((tm, d), row), pl.BlockSpec((tm, d // 2), row),
            pl.BlockSpec((tm, SLOT_LANES), row), pl.BlockSpec((tm, SLOT_LANES), row),
            pl.BlockSpec((tm, SLOT_LANES), row), pl.BlockSpec((8, N_EXPERTS), fixed),
        ],
        out_shape=[
            jax.ShapeDtypeStruct((n_tok, d), F32), jax.ShapeDtypeStruct((n_tok, d // 2), U32),
            jax.ShapeDtypeStruct((n_tok, SLOT_LANES), jnp.int32),
            jax.ShapeDtypeStruct((n_tok, SLOT_LANES), F32),
            jax.ShapeDtypeStruct((n_tok, SLOT_LANES), jnp.int32),
            jax.ShapeDtypeStruct((8, N_EXPERTS), F32),
        ],
        scratch_shapes=[pltpu.VMEM((8, N_EXPERTS), F32)],
        compiler_params=_params("arbitrary"),
        name="ln_router",
    )(h, mix, ln_g.reshape(1, d), ln_b.reshape(1, d), w_router, b_router.reshape(1, N_EXPERTS))


MOE_TM = 1280
MOE_SB = 256
MOE_NSUB = MOE_TM // MOE_SB
MOE_TH = 256
MOE_TN = 512
MOE_T1 = 2 * (D_EXPERT // MOE_TH)
MOE_T2 = D_MODEL // 2 // MOE_TN
MOE_UNIT = 32
MOE_ISUB = 4
MOE_R1 = 32
MOE_R2 = 16
assert MOE_ISUB * (MOE_T1 * MOE_R1 + MOE_T2 * MOE_R2) == MOE_TM
MOE_KD = MOE_TH


def _moe_kernel(e_ref, n_ref, prow_ref, nv_ref, ids_cur, ids_nxt, xq_hbm, wgu0_ref, wgu1_ref, bgu_ref,
                wdl0_ref, wdl1_ref, wdh0_ref, wdh1_ref, bdl_ref, bdh_ref, y_hbm, xbuf, act_ref, hg_ref, wb_ref, obuf, xsem, osem):
    i = pl.program_id(0)
    t = pl.program_id(1)
    n_i = n_ref[i]
    nsub_i = (n_i + MOE_SB - 1) // MOE_SB
    slot = i % 2
    half = D_MODEL // 2

    def issue_rows(ids_ref, slot_, base, count):
        for r in range(count):
            pltpu.make_async_copy(xq_hbm.at[pl.ds(ids_ref[0, base + r], 1)],
                                  xbuf.at[slot_, pl.ds(base + r, 1)], xsem.at[slot_]).start()

    def issue_gate_up(t1, s):
        issue_rows(ids_nxt, 1 - slot, (t1 * MOE_ISUB + s) * MOE_R1, MOE_R1)

    def issue_down(t2, s):
        issue_rows(ids_nxt, 1 - slot, MOE_T1 * MOE_ISUB * MOE_R1 + (t2 * MOE_ISUB + s) * MOE_R2, MOE_R2)

    def slot_rows(slot_):
        return pltpu.make_async_copy(xq_hbm.at[pl.ds(0, MOE_TM)], xbuf.at[slot_], xsem.at[slot_])

    @pl.when(jnp.logical_and(i == 0, t == 0))
    def _first_block_rows():
        def body(u, c):
            issue_rows(ids_cur, 0, u * MOE_UNIT, MOE_UNIT)
            return c
        lax.fori_loop(0, MOE_TM // MOE_UNIT, body, 0)

    @pl.when(jnp.logical_and(t == 0, n_i > 0))
    def _wait_rows():
        slot_rows(slot).wait()

    def sub_blocks(body):
        for s0 in range(0, MOE_NSUB - 1, 2):
            pl.when(s0 + 1 < nsub_i)(functools.partial(body, s0, 2))
            pl.when(s0 + 1 == nsub_i)(functools.partial(body, s0, 1))
        if MOE_NSUB % 2:
            pl.when(MOE_NSUB - 1 < nsub_i)(functools.partial(body, MOE_NSUB - 1, 1))

    @pl.when(jnp.logical_and(t < MOE_T1, n_i > 0))
    def _gate_up():
        wb_ref[0:half, :] = wgu0_ref[...].astype(BF16)
        wb_ref[half:, :] = wgu1_ref[...].astype(BF16)

        @sub_blocks
        def _(s0, ns):
                for s in range(s0, min(s0 + ns, MOE_ISUB)):
                    issue_gate_up(t, s)
                rows = slice(s0 * MOE_SB, (s0 + ns) * MOE_SB)
                x_lo, x_hi = _unpack_halves(xbuf[slot, rows, :])
                h = (_dot(x_lo.astype(BF16), wb_ref[0:half, :])
                     + _dot(x_hi.astype(BF16), wb_ref[half:, :]) + bgu_ref[...])

                @pl.when(t % 2 == 0)
                def _():
                    hg_ref[rows, :] = jnp.minimum(h, SWIGLU_LIMIT)

                @pl.when(t % 2 == 1)
                def _():
                    hg = hg_ref[rows, :]
                    hl = jnp.clip(h, -SWIGLU_LIMIT, SWIGLU_LIMIT)
                    act_ref[t // 2, rows, :] = (hg * jax.nn.sigmoid(SWIGLU_ALPHA * hg) * (hl + 1.0)).astype(BF16)

    @pl.when(jnp.logical_and(t >= MOE_T1, n_i > 0))
    def _down():
        t2 = t - MOE_T1
        oslot = t2 % 2
        wdl = jnp.concatenate([wdl0_ref[...].astype(BF16), wdl1_ref[...].astype(BF16)], axis=0)
        wdh = jnp.concatenate([wdh0_ref[...].astype(BF16), wdh1_ref[...].astype(BF16)], axis=0)
        col0 = pl.multiple_of(t2 * MOE_TN, MOE_TN)

        def result_copy(slot_, s):
            return pltpu.make_async_copy(
                obuf.at[slot_, pl.ds(s * MOE_SB, MOE_SB)],
                y_hbm.at[pl.ds(pl.multiple_of(prow_ref[i] + s * MOE_SB, MOE_SB), MOE_SB),
                         pl.ds(col0, MOE_TN)], osem.at[slot_])

        @sub_blocks
        def _(s0, ns):
                for s in range(s0, min(s0 + ns, MOE_ISUB)):
                    issue_down(t2, s)
                rows = slice(s0 * MOE_SB, (s0 + ns) * MOE_SB)
                y_lo = bdl_ref[...]
                y_hi = bdh_ref[...]
                for k in range(D_EXPERT // MOE_KD):
                    a = act_ref[k, rows, :]
                    y_lo = y_lo + _dot(a, wdl[k * MOE_KD:(k + 1) * MOE_KD, :])
                    y_hi = y_hi + _dot(a, wdh[k * MOE_KD:(k + 1) * MOE_KD, :])
                obuf[oslot, rows, :] = _pack_halves(jnp.concatenate([y_lo, y_hi], axis=1))
                for s in range(s0, s0 + ns):
                    result_copy(oslot, s).start()

        for s in range(MOE_NSUB):
            @pl.when(jnp.logical_and(s < nsub_i, t2 >= 1))
            def _():
                result_copy(1 - oslot, s).wait()

            @pl.when(jnp.logical_and(s < nsub_i, t2 == MOE_T2 - 1))
            def _():
                result_copy(oslot, s).wait()

    @pl.when(jnp.logical_and(t == MOE_T1 + MOE_T2 - 1, n_i > 0))
    def _missing_sub_block_shares():
        for s in range(1, MOE_ISUB):
            @pl.when(s >= nsub_i)
            def _():
                def gate_up_share(t1, c):
                    issue_gate_up(t1, s)
                    return c

                def down_share(t2, c):
                    issue_down(t2, s)
                    return c
                lax.fori_loop(0, MOE_T1, gate_up_share, 0)
                lax.fori_loop(0, MOE_T2, down_share, 0)

    @pl.when(jnp.logical_and(i == nv_ref[0] - 1, t == MOE_T1 + MOE_T2 - 1))
    def _zero_tail():
        zslot = 1 - slot
        slot_rows(zslot).wait()
        xbuf[zslot, 0:MOE_SB, :] = jnp.zeros((MOE_SB, half), U32)

        def tail_copy(j):
            return pltpu.make_async_copy(
                xbuf.at[zslot, pl.ds(0, MOE_SB)],
                y_hbm.at[pl.ds(pl.multiple_of(j * MOE_SB, MOE_SB), MOE_SB)], xsem.at[zslot])

        def start(j, c):
            tail_copy(j).start()
            return c

        def wait(j, c):
            tail_copy(j).wait()
            return c

        first = nv_ref[1] // MOE_SB
        total = y_hbm.shape[0] // MOE_SB
        lax.fori_loop(first, total, start, 0)
        lax.fori_loop(first, total, wait, 0)


def _moe_experts(xq, blk_exp, blk_n, blk_prow, n_valid_blocks, ids, w_gu, b_gu, w_down, b_down, layer,
                 n_rows_out):
    nb = ids.shape[0]
    t_last = MOE_T1 + MOE_T2 - 1
    half = D_MODEL // 2
    hi_blk = half // MOE_TN

    def src(i, nv):
        return jnp.maximum(jnp.minimum(i, nv[0] - 1), 0)

    def tt(i, t, nv):
        return jnp.where(i < nv[0], t, t_last)

    def t1(i, t, nv):
        return jnp.minimum(tt(i, t, nv), MOE_T1 - 1)

    def t2(i, t, nv):
        return jnp.maximum(tt(i, t, nv) - MOE_T1, 0)

    ids_cur_map = lambda i, t, e, n, p, nv: (src(i, nv), 0, 0)
    ids_nxt_map = lambda i, t, e, n, p, nv: (jnp.minimum(src(i, nv) + 1, nb - 1), 0, 0)
    def gu_col(i, t, nv):
        tt1 = t1(i, t, nv)
        return (tt1 % 2) * (D_EXPERT // MOE_TH) + tt1 // 2

    wgu_map = lambda i, t, e, n, p, nv: (layer, e[i], 0, gu_col(i, t, nv))
    k_half = lambda f, part: (lambda *a: f(*a)[:2] + (part,) + f(*a)[3:])
    wdl_map = lambda i, t, e, n, p, nv: (layer, e[i], 0, t2(i, t, nv))
    wdh_map = lambda i, t, e, n, p, nv: (layer, e[i], 0, hi_blk + t2(i, t, nv))
    grid_spec = pltpu.PrefetchScalarGridSpec(
        num_scalar_prefetch=4,
        grid=(nb, MOE_T1 + MOE_T2),
        in_specs=[
            pl.BlockSpec((None, 1, MOE_TM), ids_cur_map, memory_space=pltpu.SMEM),
            pl.BlockSpec((None, 1, MOE_TM), ids_nxt_map, memory_space=pltpu.SMEM),
            pl.BlockSpec(memory_space=pl.ANY),
            pl.BlockSpec((None, None, half, MOE_TH), k_half(wgu_map, 0)),
            pl.BlockSpec((None, None, half, MOE_TH), k_half(wgu_map, 1)),
            pl.BlockSpec((None, None, 1, MOE_TH), wgu_map),
            pl.BlockSpec((None, None, D_EXPERT // 2, MOE_TN), k_half(wdl_map, 0)),
            pl.BlockSpec((None, None, D_EXPERT // 2, MOE_TN), k_half(wdl_map, 1)),
            pl.BlockSpec((None, None, D_EXPERT // 2, MOE_TN), k_half(wdh_map, 0)),
            pl.BlockSpec((None, None, D_EXPERT // 2, MOE_TN), k_half(wdh_map, 1)),
            pl.BlockSpec((None, None, 1, MOE_TN), wdl_map),
            pl.BlockSpec((None, None, 1, MOE_TN), wdh_map),
        ],
        out_specs=pl.BlockSpec(memory_space=pl.ANY),
        scratch_shapes=[
            pltpu.VMEM((2, MOE_TM, half), U32),
            pltpu.VMEM((D_EXPERT // MOE_KD, MOE_TM, MOE_KD), BF16),
            pltpu.VMEM((MOE_TM, MOE_TH), F32),
            pltpu.VMEM((D_MODEL, MOE_TH), BF16),
            pltpu.VMEM((2, MOE_TM, MOE_TN), U32),
            pltpu.SemaphoreType.DMA((2,)),
            pltpu.SemaphoreType.DMA((2,)),
        ],
    )
    bgu = b_gu.reshape(DEPTH, N_EXPERTS, 1, 2 * D_EXPERT)
    ids3 = ids.reshape(nb, 1, MOE_TM)
    bdn = b_down.reshape(DEPTH, N_EXPERTS, 1, D_MODEL)
    return pl.pallas_call(
        _moe_kernel,
        grid_spec=grid_spec,
        out_shape=jax.ShapeDtypeStruct((n_rows_out, half), U32),
        compiler_params=_params("arbitrary", "arbitrary"),
        name="moe_experts",
    )(blk_exp, blk_n, blk_prow, n_valid_blocks, ids3, ids3, xq, w_gu, w_gu, bgu,
      w_down, w_down, w_down, w_down, bdn, bdn)


LNC_TM = 128


def _ln_combine_kernel(ids_cur, ids_nxt, h_ref, gate_ref, g_ref, b_ref, y_hbm, ho_ref, hb_ref, ybuf, sem):
    i = pl.program_id(0)
    tm = h_ref.shape[0]
    w = h_ref.shape[1] // 2
    rows = TOP_K * tm
    slot = i % 2

    def row_copy(ids_ref, slot_, idx):
        return pltpu.make_async_copy(y_hbm.at[pl.ds(ids_ref[0, idx], 1)],
                                     ybuf.at[slot_, pl.ds(idx, 1)], sem.at[slot_])

    def tile_rows(slot_):
        return pltpu.make_async_copy(y_hbm.at[pl.ds(0, rows)], ybuf.at[slot_], sem.at[slot_])

    @pl.when(i == 0)
    def _():
        def body(u, c):
            for r in range(MOE_UNIT):
                row_copy(ids_cur, 0, u * MOE_UNIT + r).start()
            return c
        lax.fori_loop(0, rows // MOE_UNIT, body, 0)

    tile_rows(slot).wait()
    for idx in range(rows):
        row_copy(ids_nxt, 1 - slot, idx).start()

    x_lo = DEEPNORM_ALPHA * h_ref[:, :w]
    x_hi = DEEPNORM_ALPHA * h_ref[:, w:]
    for k in range(TOP_K):
        lo, hi = _unpack_halves(ybuf[slot, k * tm:(k + 1) * tm, :])
        gk = gate_ref[:, k:k + 1]
        x_lo = x_lo + gk * lo
        x_hi = x_hi + gk * hi
    x = jnp.concatenate([x_lo, x_hi], axis=1)
    mu = jnp.mean(x, axis=-1, keepdims=True)
    xc = x - mu
    var = jnp.mean(xc * xc, axis=-1, keepdims=True)
    hn = xc * lax.rsqrt(var + LN_EPS) * g_ref[...] + b_ref[...]
    ho_ref[...] = hn
    hb_ref[...] = hn.astype(BF16)

    @pl.when(i == pl.num_programs(0) - 1)
    def _():
        tile_rows(1 - slot).wait()


def _ln_combine(h, yq, dest_tiles, gates, ln_g, ln_b):
    n_tok, d = h.shape
    tm = LNC_TM
    nblk = n_tok // tm
    row = lambda i: (i, 0)
    fixed = lambda i: (0, 0)
    dest3 = dest_tiles.reshape(nblk, 1, TOP_K * tm)
    return pl.pallas_call(
        _ln_combine_kernel,
        grid=(nblk,),
        in_specs=[
            pl.BlockSpec((None, 1, TOP_K * tm), lambda i: (i, 0, 0), memory_space=pltpu.SMEM),
            pl.BlockSpec((None, 1, TOP_K * tm), lambda i: (jnp.minimum(i + 1, nblk - 1), 0, 0),
                         memory_space=pltpu.SMEM),
            pl.BlockSpec((tm, d), row),
            pl.BlockSpec((tm, SLOT_LANES), row),
            pl.BlockSpec((1, d), fixed), pl.BlockSpec((1, d), fixed),
            pl.BlockSpec(memory_space=pl.ANY),
        ],
        out_specs=[pl.BlockSpec((tm, d), row), pl.BlockSpec((tm, d), row)],
        out_shape=[jax.ShapeDtypeStruct((n_tok, d), F32), jax.ShapeDtypeStruct((n_tok, d), BF16)],
        scratch_shapes=[pltpu.VMEM((2, TOP_K * tm, d // 2), U32), pltpu.SemaphoreType.DMA((2,))],
        compiler_params=_params("arbitrary"),
        name="ln_combine",
    )(dest3, dest3, h, gates, ln_g.reshape(1, d), ln_b.reshape(1, d), yq)


def _moe_layer(h, hq, top_idx, gates, rank, counts, w_gu, b_gu, w_down, b_down, layer, ln_g, ln_b):
    n_tok = h.shape[0]
    n_assign = n_tok * TOP_K
    nb = N_EXPERTS + n_assign // MOE_TM
    n_rows_out = n_assign + N_EXPERTS * MOE_SB
    i32 = jnp.int32
    cnt = counts[0].astype(i32)
    pcnt = (cnt + MOE_SB - 1) // MOE_SB * MOE_SB
    pstart = jnp.cumsum(pcnt) - pcnt
    nblk = (cnt + MOE_TM - 1) // MOE_TM
    bend = jnp.cumsum(nblk)
    bfirst = bend - nblk
    n_valid = bend[-1]
    e_flat = top_idx[:, :TOP_K].reshape(-1)
    rank_flat = rank[:, :TOP_K].reshape(-1)
    tok = jnp.arange(n_assign, dtype=i32) // TOP_K
    dest = pstart[e_flat] + rank_flat
    blk = jnp.arange(nb, dtype=i32)
    src = jnp.maximum(jnp.minimum(blk, n_valid - 1), 0)
    be = jnp.minimum(jnp.sum((bend[None, :] <= src[:, None]).astype(i32), axis=1), N_EXPERTS - 1)
    b_in = src - bfirst[be]
    blk_n = jnp.where(blk < n_valid, jnp.clip(cnt[be] - b_in * MOE_TM, 0, MOE_TM), 0).astype(i32)
    blk_prow = (pstart[be] + b_in * MOE_TM).astype(i32)
    id_pos = (bfirst[e_flat] + rank_flat // MOE_TM) * MOE_TM + rank_flat % MOE_TM
    ids = jnp.zeros((nb * MOE_TM,), i32).at[id_pos].set(tok).reshape(nb, MOE_TM)
    used_rows = pstart[-1] + pcnt[-1]
    yq = _moe_experts(hq, be.astype(i32), blk_n, blk_prow, jnp.stack([n_valid, used_rows]).astype(i32), ids,
                      w_gu, b_gu, w_down, b_down, layer, n_rows_out)
    tm = LNC_TM
    dest_tiles = dest.reshape(n_tok // tm, tm, TOP_K).transpose(0, 2, 1).reshape(n_tok // tm, TOP_K * tm)
    return _ln_combine(h, yq, dest_tiles, gates, ln_g, ln_b)


def kernel(x, positions, ln_g, ln_b, hy_w_in, hy_w_out, gm_ln_g, gm_ln_b, gm_w_s, gm_b_s, ssm_lam_re, ssm_lam_im, ssm_log_dt, ssm_b_re, ssm_b_im, ssm_c_re, ssm_c_im, ssm_d, ssm_w_glu, ssm_b_glu, mla_w_in, mla_q_norm_g, mla_kv_norm_g, mla_w_uq, mla_w_ukv, mla_w_o, moe_w_router, moe_b_router, moe_w_gu, moe_b_gu, moe_w_down, moe_b_down):
    batch, seq, d = x.shape
    n_tok = batch * seq
    h = x.reshape(n_tok, d)
    hb = h.astype(BF16)
    for layer in range(DEPTH):
        i = layer // 2
        if layer % 2 == 0:
            z_gm = _mm([hb], hy_w_in[i], col0=0, n_cols=2 * GM_WIDTH, tm=1024, tn=256,
                       out_dtype=BF16, epilogue="gelu", name="hy_in_gm")
            z_ssm = _mm([hb], hy_w_in[i], col0=2 * GM_WIDTH, n_cols=SSM_WIDTH, tm=1024, tn=256,
                        out_dtype=F32, name="hy_in_ssm")
            y_gm = _spatial_gating(z_gm, gm_ln_g[i], gm_ln_b[i], gm_w_s[i], gm_b_s[i])
            tables = _s5_tables(ssm_lam_re[i], ssm_lam_im[i], ssm_log_dt[i], ssm_b_re[i], ssm_b_im[i],
                                ssm_c_re[i], ssm_c_im[i], ssm_d[i], seq)
            y_act = _s5_mixer(z_ssm, batch, seq, tables)
            y_ssm = _mm([y_act], ssm_w_glu[i], tm=1024, tn=512, out_dtype=BF16, epilogue="glu",
                        bias=ssm_b_glu[i], mul=y_act, name="s5_glu")
            mix = _mm([y_gm, y_ssm], hy_w_out[i], tm=1024, tn=256, out_dtype=F32, name="hy_out")
        else:
            w_in = mla_w_in[i]
            n_main = MLA_Q_RANK + MLA_KV_RANK
            c_main = _mm([hb], w_in, col0=0, n_cols=n_main, tm=1024, tn=256, out_dtype=F32,
                         name="mla_in")
            k_rope = _mm([hb], w_in[:, n_main:], tm=1024, tn=MLA_ROPE, out_dtype=F32, name="mla_in_rope")
            tab, k_rope = _rope_tables(positions, k_rope)
            q = _mm([(c_main, MLA_Q_RANK, 0)], mla_w_uq[i], tm=1024, tn=768, out_dtype=BF16,
                    prologue="rms", gain=mla_q_norm_g[i], name="mla_uq")
            kv = _mm([(c_main, MLA_KV_RANK, MLA_Q_RANK // MLA_KV_RANK)], mla_w_ukv[i], tm=1024, tn=1024,
                     out_dtype=BF16, prologue="rms", gain=mla_kv_norm_g[i], name="mla_ukv")
            o = _attention(q, kv, k_rope, tab, batch, seq)
            mix = _mm([o], mla_w_o[i], tm=1024, tn=256, out_dtype=F32, name="mla_out")
        h, hq, top_idx, gates, rank, counts = _ln_router(
            h, mix, ln_g[layer, 0], ln_b[layer, 0], moe_w_router[layer], moe_b_router[layer])
        h, hb = _moe_layer(h, hq, top_idx, gates, rank, counts, moe_w_gu, moe_b_gu,
                           moe_w_down, moe_b_down, layer, ln_g[layer, 1], ln_b[layer, 1])
    return h.reshape(batch, seq, d)
```

```python
import functools
import math

import numpy as np
import jax
import jax.numpy as jnp
from jax import lax
from jax.experimental import pallas as pl
from jax.experimental.pallas import tpu as pltpu

F32 = jnp.float32
BF16 = jnp.bfloat16
U32 = jnp.uint32
HI_MASK = np.uint32(0xFFFF0000)

D_MODEL = 4096
DEPTH = 2
CHUNK = 64
DEEPNORM_ALPHA = (2 * DEPTH) ** 0.25
LN_EPS = 1e-5
RMS_EPS = 1e-6
LANES = 128

GM_WIDTH = 2048
GM_GROUPS = 8
GM_GROUP_DIM = 256
GM_BLOCK = 128
SSM_WIDTH = 2048
SSM_P = 16
SSM_GROUPS = 128
SSM_N = 64
SSM_T = 16
SSM_OCT = LANES // SSM_P
SSM_NOCT = SSM_GROUPS // SSM_OCT
SSM_PAIRS = SSM_OCT // 2
SSM_SW = SSM_OCT * SSM_N

MLA_HEADS = 32
MLA_Q_RANK = 1024
MLA_KV_RANK = 512
MLA_NOPE = 128
MLA_ROPE = 64
MLA_V = 128
ROPE_THETA = 10000.0

N_EXPERTS = 32
TOP_K = 4
D_EXPERT = 1024
SWIGLU_LIMIT = 7.0
SWIGLU_ALPHA = 1.702

VMEM_LIMIT_BYTES = 56 * 1024 * 1024


def _params(*sem):
    return pltpu.CompilerParams(dimension_semantics=sem, vmem_limit_bytes=VMEM_LIMIT_BYTES)


def _dot(a, b):
    return jnp.dot(a, b, preferred_element_type=F32)


def _dot_nt(a, b):
    return lax.dot_general(a, b, (((1,), (1,)), ((), ())), preferred_element_type=F32)


def _mm_kernel(*refs, prologue, epilogue, n_x):
    it = iter(refs)
    x_refs = [next(it) for _ in range(n_x)]
    w_ref = next(it)
    g_ref = next(it) if prologue == "rms" else None
    b_ref = next(it) if epilogue == "glu" else None
    y_ref = next(it) if epilogue == "glu" else None
    o_ref = next(it)
    xs_ref = next(it) if prologue == "rms" else None

    if prologue == "rms":
        @pl.when(pl.program_id(1) == 0)
        def _():
            xf = x_refs[0][...].astype(F32)
            ms = jnp.mean(xf * xf, axis=-1, keepdims=True)
            xs_ref[...] = (xf * lax.rsqrt(ms + RMS_EPS) * g_ref[...]).astype(BF16)
        xs = [xs_ref[...]]
    else:
        xs = [r[...].astype(BF16) for r in x_refs]

    acc = None
    k0 = 0
    for x in xs:
        kk = x.shape[1]
        part = _dot(x, w_ref[k0:k0 + kk, :].astype(BF16))
        acc = part if acc is None else acc + part
        k0 += kk
    if epilogue == "gelu":
        acc = jax.nn.gelu(acc)
    elif epilogue == "glu":
        acc = y_ref[...].astype(F32) * jax.nn.sigmoid(acc + b_ref[...])
    o_ref[...] = acc.astype(o_ref.dtype)


def _mm(xs, w, *, col0=0, n_cols=None, tm, tn, out_dtype, prologue=None, gain=None,
        epilogue=None, bias=None, mul=None, name):
    xs = [(x, x.shape[1], 0) if not isinstance(x, tuple) else x for x in xs]
    m = xs[0][0].shape[0]
    k_total = w.shape[0]
    n_cols = w.shape[1] - col0 if n_cols is None else n_cols
    tn = min(tn, n_cols)
    assert m % tm == 0 and n_cols % tn == 0 and col0 % tn == 0
    cb0 = col0 // tn
    assert sum(kw for _, kw, _ in xs) == k_total
    in_specs = [pl.BlockSpec((tm, kw), functools.partial(lambda i, j, cb: (i, cb), cb=cb))
                for _, kw, cb in xs]
    in_specs.append(pl.BlockSpec((k_total, tn), lambda i, j: (0, cb0 + j)))
    args = [x for x, _, _ in xs] + [w]
    scratch = []
    if prologue == "rms":
        in_specs.append(pl.BlockSpec((1, k_total), lambda i, j: (0, 0)))
        args.append(gain.reshape(1, k_total))
        scratch.append(pltpu.VMEM((tm, k_total), BF16))
    if epilogue == "glu":
        in_specs.append(pl.BlockSpec((1, tn), lambda i, j: (0, j)))
        in_specs.append(pl.BlockSpec((tm, tn), lambda i, j: (i, j)))
        args += [bias.reshape(1, n_cols), mul]
    return pl.pallas_call(
        functools.partial(_mm_kernel, prologue=prologue, epilogue=epilogue, n_x=len(xs)),
        grid=(m // tm, n_cols // tn),
        in_specs=in_specs,
        out_specs=pl.BlockSpec((tm, tn), lambda i, j: (i, j)),
        out_shape=jax.ShapeDtypeStruct((m, n_cols), out_dtype),
        scratch_shapes=scratch,
        compiler_params=_params("arbitrary", "arbitrary"),
        name=name,
    )(*args)


def _gating_kernel(u_ref, v_ref, g_ref, b_ref, ws_ref, bst_ref, o_ref):
    row = lax.broadcasted_iota(jnp.int32, (GM_BLOCK, GM_BLOCK), 0)
    col = lax.broadcasted_iota(jnp.int32, (GM_BLOCK, GM_BLOCK), 1)
    visible = (col // CHUNK) <= (row // CHUNK)
    for g in range(GM_GROUPS):
        sl = slice(g * GM_GROUP_DIM, (g + 1) * GM_GROUP_DIM)
        v = v_ref[:, sl].astype(F32)
        mu = jnp.mean(v, axis=-1, keepdims=True)
        vc = v - mu
        var = jnp.mean(vc * vc, axis=-1, keepdims=True)
        vn = vc * lax.rsqrt(var + LN_EPS) * g_ref[:, sl] + b_ref[:, sl]
        w = jnp.where(visible, ws_ref[g], 0.0).astype(BF16)
        s = _dot(w, vn.astype(BF16)) + bst_ref[:, g:g + 1]
        o_ref[:, sl] = (u_ref[:, sl].astype(F32) * s).astype(o_ref.dtype)


def _spatial_gating(z_gm, ln_g, ln_b, w_s, b_s):
    n_tok = z_gm.shape[0]
    return pl.pallas_call(
        _gating_kernel,
        grid=(n_tok // GM_BLOCK,),
        in_specs=[
            pl.BlockSpec((GM_BLOCK, GM_WIDTH), lambda i: (i, 0)),
            pl.BlockSpec((GM_BLOCK, GM_WIDTH), lambda i: (i, 1)),
            pl.BlockSpec((1, GM_WIDTH), lambda i: (0, 0)),
            pl.BlockSpec((1, GM_WIDTH), lambda i: (0, 0)),
            pl.BlockSpec((GM_GROUPS, GM_BLOCK, GM_BLOCK), lambda i: (0, 0, 0)),
            pl.BlockSpec((GM_BLOCK, GM_GROUPS), lambda i: (0, 0)),
        ],
        out_specs=pl.BlockSpec((GM_BLOCK, GM_WIDTH), lambda i: (i, 0)),
        out_shape=jax.ShapeDtypeStruct((n_tok, GM_WIDTH), BF16),
        compiler_params=_params("arbitrary"),
        name="spatial_gating",
    )(z_gm, z_gm, ln_g.reshape(1, GM_WIDTH), ln_b.reshape(1, GM_WIDTH), w_s, b_s.T)


def _s5_tables(lam_re, lam_im, log_dt, b_re, b_im, c_re, c_im, d_skip, seq):
    hi = lax.Precision.HIGHEST
    n, p, t = SSM_N, SSM_P, SSM_T
    no, npr = SSM_NOCT, SSM_PAIRS
    dt = jnp.exp(log_dt)[:, None]
    mag = jnp.exp(lam_re * dt)
    ab_re = mag * jnp.cos(lam_im * dt)
    ab_im = mag * jnp.sin(lam_im * dt)
    den = lam_re * lam_re + lam_im * lam_im
    num_re = ab_re - 1.0
    coef_re = (num_re * lam_re + ab_im * lam_im) / den
    coef_im = (ab_im * lam_re - num_re * lam_im) / den
    bb_re = coef_re[..., None] * b_re - coef_im[..., None] * b_im
    bb_im = coef_re[..., None] * b_im + coef_im[..., None] * b_re

    def power(k):
        k = jnp.asarray(k, F32)[..., None, None]
        mk = jnp.exp(k * (lam_re * dt))
        return mk * jnp.cos(k * (lam_im * dt)), mk * jnp.sin(k * (lam_im * dt))

    pw_re, pw_im = power(np.arange(t + 1))
    cp_re = c_re[None] * pw_re[:, :, None, :] - c_im[None] * pw_im[:, :, None, :]
    cp_im = c_re[None] * pw_im[:, :, None, :] + c_im[None] * pw_re[:, :, None, :]
    kern = (jnp.einsum('tgpn,gnq->gtpq', cp_re[:t], bb_re, precision=hi)
            - jnp.einsum('tgpn,gnq->gtpq', cp_im[:t], bb_im, precision=hi))
    eye_o = jnp.eye(SSM_OCT, dtype=F32)
    eye_2 = jnp.eye(2, dtype=F32)
    kbd = jnp.einsum('oatpq,ab->otaqbp', kern.reshape(no, SSM_OCT, t, p, p), eye_o)
    kbd = kbd.reshape(no, t, LANES, LANES)
    rv_re, rv_im = pw_re[t - 1::-1], pw_im[t - 1::-1]
    st_re = rv_re[..., None] * bb_re[None] - rv_im[..., None] * bb_im[None]
    st_im = rv_re[..., None] * bb_im[None] + rv_im[..., None] * bb_re[None]
    st = jnp.stack([st_re, st_im], 0).reshape(2, t, no, npr, 2, n, p)
    w_st = jnp.einsum('rjoxanq,ab->ojxraqbn', st, eye_2).reshape(no, t, npr, 2, 2 * p, 2 * n)
    wc = jnp.stack([cp_re[1:], -cp_im[1:]], 0).reshape(2, t, no, npr, 2, p, n)
    w_ct = jnp.einsum('rioxapn,ab->oixrapbn', wc, eye_2).reshape(no, t, npr, 2, 2 * p, 2 * n)
    n_lvl = int(math.log2(seq // t))
    lv_re, lv_im = power(t * 2 ** np.arange(n_lvl))
    lr = lv_re.reshape(n_lvl, no, SSM_SW)
    li = lv_im.reshape(n_lvl, no, SSM_SW)
    lvl = jnp.stack([jnp.concatenate([lr, lr], -1), jnp.concatenate([-li, li], -1)], axis=1)
    lvl = lvl.transpose(2, 0, 1, 3).reshape(no, 2 * n_lvl, 2 * SSM_SW)
    d_vec = jnp.tile(d_skip.reshape(no, 1, LANES), (1, 1, t))
    return kbd.astype(BF16), w_st.astype(BF16), w_ct.astype(BF16), lvl, d_vec


def _s5_kernel(u_ref, kbd_ref, wst_ref, wct_ref, lvl_ref, d_ref, o_ref, toe_ref, st_ref, ct_ref,
               *, n_chunks, n_lvl):
    t = SSM_T

    @pl.when(pl.program_id(0) == 0)
    def _():
        toe_ref[...] = jnp.zeros_like(toe_ref)
        st_ref[...] = jnp.zeros_like(st_ref)
        ct_ref[...] = jnp.zeros_like(ct_ref)

    for i in range(t):
        for j in range(i + 1):
            toe_ref[j * LANES:(j + 1) * LANES, i * LANES:(i + 1) * LANES] = kbd_ref[i - j]
    for j in range(t):
        for x in range(SSM_PAIRS):
            for r in range(2):
                rows = slice(j * LANES + x * 2 * SSM_P, j * LANES + (x + 1) * 2 * SSM_P)
                lanes = slice(r * SSM_SW + x * LANES, r * SSM_SW + (x + 1) * LANES)
                st_ref[rows, lanes] = wst_ref[j, x, r]
                ct_ref[rows, lanes] = wct_ref[j, x, r]

    u = u_ref[...]
    ub = u.astype(BF16)
    h = _dot(ub, st_ref[...])
    cidx = lax.broadcasted_iota(jnp.int32, (u.shape[0], 1), 0) % n_chunks
    for lv in range(n_lvl):
        d = 1 << lv
        a_rr = lvl_ref[2 * lv:2 * lv + 1, :]
        a_is = lvl_ref[2 * lv + 1:2 * lv + 2, :]
        sh = jnp.where(cidx >= d, pltpu.roll(h, d, 0), 0.0)
        h = h + sh * a_rr + pltpu.roll(sh, SSM_SW, 1) * a_is
    h_prev = jnp.where(cidx >= 1, pltpu.roll(h, 1, 0), 0.0)
    y = _dot(ub, toe_ref[...]) + _dot_nt(h_prev.astype(BF16), ct_ref[...]) + d_ref[...] * u
    o_ref[...] = jax.nn.gelu(y).astype(o_ref.dtype)


def _s5_mixer(z_ssm, batch, seq, tables):
    kbd, w_st, w_ct, lvl, d_vec = tables
    t, no = SSM_T, SSM_NOCT
    n_chunks = seq // t
    rows = batch * n_chunks
    width = t * LANES
    n_lvl = lvl.shape[1] // 2
    u = z_ssm.reshape(rows, t, no, LANES).transpose(2, 0, 1, 3).reshape(no, rows, width)
    y = pl.pallas_call(
        functools.partial(_s5_kernel, n_chunks=n_chunks, n_lvl=n_lvl),
        grid=(no,),
        in_specs=[
            pl.BlockSpec((None, rows, width), lambda i: (i, 0, 0)),
            pl.BlockSpec((None, t, LANES, LANES), lambda i: (i, 0, 0, 0)),
            pl.BlockSpec((None, t, SSM_PAIRS, 2, 2 * SSM_P, LANES), lambda i: (i, 0, 0, 0, 0, 0)),
            pl.BlockSpec((None, t, SSM_PAIRS, 2, 2 * SSM_P, LANES), lambda i: (i, 0, 0, 0, 0, 0)),
            pl.BlockSpec((None, 2 * n_lvl, 2 * SSM_SW), lambda i: (i, 0, 0)),
            pl.BlockSpec((None, 1, width), lambda i: (i, 0, 0)),
        ],
        out_specs=pl.BlockSpec((None, rows, width), lambda i: (i, 0, 0)),
        out_shape=jax.ShapeDtypeStruct((no, rows, width), BF16),
        scratch_shapes=[
            pltpu.VMEM((width, width), BF16),
            pltpu.VMEM((width, 2 * SSM_SW), BF16),
            pltpu.VMEM((width, 2 * SSM_SW), BF16),
        ],
        compiler_params=_params("arbitrary"),
        name="s5_mixer",
    )(u, kbd, w_st, w_ct, lvl, d_vec)
    return y.reshape(no, rows, t, LANES).transpose(1, 2, 0, 3).reshape(batch * seq, SSM_WIDTH)


def _rope_kernel(pos_ref, freq_ref, kr_ref, tab_ref, kro_ref):
    ang = pos_ref[...].astype(F32) * freq_ref[...]
    lane = lax.broadcasted_iota(jnp.int32, ang.shape, 1)
    sin = jnp.sin(ang)
    tab = jnp.where(lane < MLA_ROPE, jnp.cos(ang), jnp.where(lane < MLA_ROPE + MLA_ROPE // 2, -sin, sin))
    tab_ref[...] = tab
    kr = kr_ref[...]
    half = MLA_ROPE // 2
    sw = jnp.concatenate([kr[:, half:], kr[:, :half]], axis=1)
    kro_ref[...] = (kr * tab[:, :MLA_ROPE] + sw * tab[:, MLA_ROPE:]).astype(kro_ref.dtype)


def _rope_tables(positions, k_rope):
    n_tok = k_rope.shape[0]
    tm = 512
    inv_freq = ROPE_THETA ** (-np.arange(0, MLA_ROPE, 2, dtype=np.float64) / MLA_ROPE)
    freq = jnp.asarray(np.tile(inv_freq, 4)[None, :], F32)
    return pl.pallas_call(
        _rope_kernel,
        grid=(n_tok // tm,),
        in_specs=[
            pl.BlockSpec((tm, 1), lambda i: (i, 0)),
            pl.BlockSpec((1, 2 * MLA_ROPE), lambda i: (0, 0)),
            pl.BlockSpec((tm, MLA_ROPE), lambda i: (i, 0)),
        ],
        out_specs=[
            pl.BlockSpec((tm, 2 * MLA_ROPE), lambda i: (i, 0)),
            pl.BlockSpec((tm, MLA_ROPE), lambda i: (i, 0)),
        ],
        out_shape=[
            jax.ShapeDtypeStruct((n_tok, 2 * MLA_ROPE), F32),
            jax.ShapeDtypeStruct((n_tok, MLA_ROPE), BF16),
        ],
        compiler_params=_params("arbitrary"),
        name="rope_tables",
    )(positions.reshape(n_tok, 1), freq, k_rope)


ATT_TK = 256
ATT_TQ = 2 * ATT_TK
ATT_DK = 2 * MLA_NOPE


def _attn_kernel(q_ref, tab_ref, k0_ref, v0_ref, k1_ref, v1_ref, kr_ref, o_ref, kc_ref):
    qi = pl.program_id(2)
    hd = MLA_NOPE + MLA_ROPE
    scale = hd ** -0.5 * math.log2(math.e)
    half = MLA_ROPE // 2
    tk = ATT_TK

    @pl.when(qi == 0)
    def _():
        for hh, k_ref in enumerate((k0_ref, k1_ref)):
            kc_ref[hh, :, 0:MLA_NOPE] = k_ref[...]
            kc_ref[hh, :, MLA_NOPE:hd] = kr_ref[...]
            kc_ref[hh, :, hd:] = jnp.zeros((kr_ref.shape[0], ATT_DK - hd), BF16)

    q = q_ref[...]
    cosf = tab_ref[:, :MLA_ROPE]
    sinf = tab_ref[:, MLA_ROPE:]
    qc = []
    for hh in range(2):
        qn = q[:, hh * hd:hh * hd + MLA_NOPE].astype(F32) * scale
        qr = q[:, hh * hd + MLA_NOPE:(hh + 1) * hd].astype(F32)
        sw = jnp.concatenate([qr[:, half:], qr[:, :half]], axis=1)
        qr = (qr * cosf + sw * sinf) * scale
        pad = jnp.zeros((ATT_TQ, ATT_DK - hd), F32)
        qc.append(jnp.concatenate([qn, qr, pad], axis=1).astype(BF16))
    v_refs = (v0_ref, v1_ref)
    row = lax.broadcasted_iota(jnp.int32, (tk, tk), 0)
    col = lax.broadcasted_iota(jnp.int32, (tk, tk), 1)
    visible = (col // CHUNK) <= (row // CHUNK)

    def update(state, hh, sub, j, masked):
        m, l, acc = state
        start = pl.multiple_of(j * tk, tk)
        ks = kc_ref[hh, pl.ds(start, tk), :]
        vs = v_refs[hh][pl.ds(start, tk), :]
        s = _dot_nt(qc[hh][sub * tk:(sub + 1) * tk], ks)
        if masked:
            s = jnp.where(visible, s, -1e30)
        m_new = jnp.maximum(m, jnp.max(s, axis=-1, keepdims=True))
        alpha = jnp.exp2(m - m_new)
        p = jnp.exp2(s - m_new)
        l = alpha * l + jnp.sum(p, axis=-1, keepdims=True)
        acc = alpha * acc + _dot(p.astype(BF16), vs)
        return m_new, l, acc

    chains = [(hh, sub) for hh in range(2) for sub in range(2)]

    def body(jj, states):
        states = tuple(update(st, hh, sub, 2 * jj, False) for st, (hh, sub) in zip(states, chains))
        return tuple(update(st, hh, sub, 2 * jj + 1, False) for st, (hh, sub) in zip(states, chains))

    init = tuple((jnp.full((tk, 1), -1e30, F32), jnp.zeros((tk, 1), F32), jnp.zeros((tk, MLA_V), F32))
                 for _ in chains)
    states = list(lax.fori_loop(0, qi, body, init))
    for c, (hh, sub) in enumerate(chains):
        st = update(states[c], hh, sub, 2 * qi, masked=(sub == 0))
        if sub == 1:
            st = update(st, hh, sub, 2 * qi + 1, masked=True)
        _, l, acc = st
        o_ref[sub * tk:(sub + 1) * tk, hh * MLA_V:(hh + 1) * MLA_V] = (acc / l).astype(o_ref.dtype)


def _attention(q, kv, k_rope, tab, batch, seq):
    n_tok = batch * seq
    nq = seq // ATT_TQ
    kvb = lambda off: pl.BlockSpec((seq, MLA_NOPE), lambda b, hp, qi: (b, 4 * hp + off))
    return pl.pallas_call(
        _attn_kernel,
        grid=(batch, MLA_HEADS // 2, nq),
        in_specs=[
            pl.BlockSpec((ATT_TQ, 2 * (MLA_NOPE + MLA_ROPE)), lambda b, hp, qi: (b * nq + qi, hp)),
            pl.BlockSpec((ATT_TQ, 2 * MLA_ROPE), lambda b, hp, qi: (b * nq + qi, 0)),
            kvb(0), kvb(1), kvb(2), kvb(3),
            pl.BlockSpec((seq, MLA_ROPE), lambda b, hp, qi: (b, 0)),
        ],
        out_specs=pl.BlockSpec((ATT_TQ, 2 * MLA_V), lambda b, hp, qi: (b * nq + qi, hp)),
        out_shape=jax.ShapeDtypeStruct((n_tok, MLA_HEADS * MLA_V), BF16),
        scratch_shapes=[pltpu.VMEM((2, seq, ATT_DK), BF16)],
        compiler_params=_params("arbitrary", "arbitrary", "arbitrary"),
        name="mla_attention",
    )(q, tab, kv, kv, kv, kv, k_rope)


LNR_TM = 256
SLOT_LANES = 128


def _pack_halves(x):
    w = x.shape[1] // 2
    bits = lax.bitcast_convert_type(x.astype(BF16).astype(F32), U32)
    return (bits[:, :w] >> 16) | (bits[:, w:] & HI_MASK)


def _unpack_halves(p):
    return (lax.bitcast_convert_type(p << 16, F32), lax.bitcast_convert_type(p & HI_MASK, F32))


def _ln_router_kernel(h_ref, mix_ref, g_ref, b_ref, wr_ref, br_ref,
                      ho_ref, hq_ref, idx_ref, gate_ref, rank_ref, cnt_ref, carry_ref):
    @pl.when(pl.program_id(0) == 0)
    def _():
        carry_ref[...] = jnp.zeros_like(carry_ref)

    x = DEEPNORM_ALPHA * h_ref[...] + mix_ref[...]
    mu = jnp.mean(x, axis=-1, keepdims=True)
    xc = x - mu
    var = jnp.mean(xc * xc, axis=-1, keepdims=True)
    hn = xc * lax.rsqrt(var + LN_EPS) * g_ref[...] + b_ref[...]
    ho_ref[...] = hn
    hq_ref[...] = _pack_halves(hn)

    logits =jnp.dot(hn, wr_ref[...], precision=lax.Precision.HIGHEST,
                     preferred_element_type=F32) + br_ref[...]
    tm = logits.shape[0]
    lane_e = lax.broadcasted_iota(jnp.int32, (tm, N_EXPERTS), 1)
    lane_s = lax.broadcasted_iota(jnp.int32, (tm, SLOT_LANES), 1)
    work = logits
    sel = jnp.zeros((tm, N_EXPERTS), F32)
    top_v, top_i, hot = [], [], []
    for _ in range(TOP_K):
        mx = jnp.max(work, axis=-1, keepdims=True)
        ix = jnp.min(jnp.where(work == mx, lane_e, N_EXPERTS), axis=-1, keepdims=True)
        oh = lane_e == ix
        work = jnp.where(oh, -jnp.inf, work)
        sel = sel + oh.astype(F32)
        top_v.append(mx)
        top_i.append(ix)
        hot.append(oh)
    ex = [jnp.exp(v - top_v[0]) for v in top_v]
    den = ex[0] + ex[1] + ex[2] + ex[3]

    r = lax.broadcasted_iota(jnp.int32, (tm, tm), 0)
    c = lax.broadcasted_iota(jnp.int32, (tm, tm), 1)
    strict = jnp.where(c < r, 1.0, 0.0).astype(BF16)
    prefix = _dot(strict, sel.astype(BF16)) + carry_ref[0:1, :]
    carry_ref[0:1, :] = carry_ref[0:1, :] + jnp.sum(sel, axis=0, keepdims=True)
    cnt_ref[...] = jnp.broadcast_to(carry_ref[0:1, :], cnt_ref.shape)

    idx_o = jnp.zeros((tm, SLOT_LANES), jnp.int32)
    gate_o = jnp.zeros((tm, SLOT_LANES), F32)
    rank_o = jnp.zeros((tm, SLOT_LANES), jnp.int32)
    for k in range(TOP_K):
        rk = jnp.sum(jnp.where(hot[k], prefix, 0.0), axis=-1, keepdims=True)
        idx_o = jnp.where(lane_s == k, top_i[k], idx_o)
        gate_o = jnp.where(lane_s == k, ex[k] / den, gate_o)
        rank_o = jnp.where(lane_s == k, rk.astype(jnp.int32), rank_o)
    idx_ref[...] = idx_o
    gate_ref[...] = gate_o
    rank_ref[...] = rank_o


def _ln_router(h, mix, ln_g, ln_b, w_router, b_router):
    n_tok, d = h.shape
    tm = LNR_TM
    row = lambda i: (i, 0)
    fixed = lambda i: (0, 0)
    return pl.pallas_call(
        _ln_router_kernel,
        grid=(n_tok // tm,),
        in_specs=[
            pl.BlockSpec((tm, d), row), pl.BlockSpec((tm, d), row),
            pl.BlockSpec((1, d), fixed), pl.BlockSpec((1, d), fixed),
            pl.BlockSpec((d, N_EXPERTS), fixed), pl.BlockSpec((1, N_EXPERTS), fixed),
        ],
        out_specs=[
            pl.BlockSpec((tm, d), row), pl.BlockSpec((tm, d // 2), row),
            pl.BlockSpec((tm, SLOT_LANES), row), pl.BlockSpec((tm, SLOT_LANES), row),
            pl.BlockSpec((tm, SLOT_LANES), row), pl.BlockSpec((8, N_EXPERTS), fixed),
        ],
        out_shape=[
            jax.ShapeDtypeStruct((n_tok, d), F32), jax.ShapeDtypeStruct((n_tok, d // 2), U32),
            jax.ShapeDtypeStruct((n_tok, SLOT_LANES), jnp.int32),
            jax.ShapeDtypeStruct((n_tok, SLOT_LANES), F32),
            jax.ShapeDtypeStruct((n_tok, SLOT_LANES), jnp.int32),
            jax.ShapeDtypeStruct((8, N_EXPERTS), F32),
        ],
        scratch_shapes=[pltpu.VMEM((8, N_EXPERTS), F32)],
        compiler_params=_params("arbitrary"),
        name="ln_router",
    )(h, mix, ln_g.reshape(1, d), ln_b.reshape(1, d), w_router, b_router.reshape(1, N_EXPERTS))


MOE_TM = 1280
MOE_SB = 256
MOE_NSUB = MOE_TM // MOE_SB
MOE_TH = 256
MOE_TN = 512
MOE_T1 = 2 * (D_EXPERT // MOE_TH)
MOE_T2 = D_MODEL // 2 // MOE_TN
MOE_UNIT = 32
MOE_ISUB = 4
MOE_R1 = 32
MOE_R2 = 16
assert MOE_ISUB * (MOE_T1 * MOE_R1 + MOE_T2 * MOE_R2) == MOE_TM
MOE_KD = MOE_TH
MOE_NB_SHORT = N_EXPERTS + 4


def _moe_kernel(e_ref, n_ref, prow_ref, nv_ref, ids_cur, ids_nxt, xq_hbm, wgu_ref, bgu_ref,
                wdl_ref, wdh_ref, bdl_ref, bdh_ref, y_hbm, xbuf, act_ref, hg_ref, wb_ref, obuf, xsem, osem):
    i = pl.program_id(0)
    t = pl.program_id(1)
    n_i = n_ref[i]
    nsub_i = (n_i + MOE_SB - 1) // MOE_SB
    slot = i % 2
    half = D_MODEL // 2

    def issue_rows(ids_ref, slot_, base, count):
        for r in range(count):
            pltpu.make_async_copy(xq_hbm.at[pl.ds(ids_ref[0, base + r], 1)],
                                  xbuf.at[slot_, pl.ds(base + r, 1)], xsem.at[slot_]).start()

    def issue_gate_up(t1, s):
        issue_rows(ids_nxt, 1 - slot, (t1 * MOE_ISUB + s) * MOE_R1, MOE_R1)

    def issue_down(t2, s):
        issue_rows(ids_nxt, 1 - slot, MOE_T1 * MOE_ISUB * MOE_R1 + (t2 * MOE_ISUB + s) * MOE_R2, MOE_R2)

    def slot_rows(slot_):
        return pltpu.make_async_copy(xq_hbm.at[pl.ds(0, MOE_TM)], xbuf.at[slot_], xsem.at[slot_])

    @pl.when(jnp.logical_and(i == 0, t == 0))
    def _first_block_rows():
        def body(u, c):
            issue_rows(ids_cur, 0, u * MOE_UNIT, MOE_UNIT)
            return c
        lax.fori_loop(0, MOE_TM // MOE_UNIT, body, 0)

    @pl.when(jnp.logical_and(t == 0, n_i > 0))
    def _wait_rows():
        slot_rows(slot).wait()

    def sub_blocks(body):
        for s0 in range(0, MOE_NSUB - 1, 2):
            pl.when(s0 + 1 < nsub_i)(functools.partial(body, s0, 2))
            pl.when(s0 + 1 == nsub_i)(functools.partial(body, s0, 1))
        if MOE_NSUB % 2:
            pl.when(MOE_NSUB - 1 < nsub_i)(functools.partial(body, MOE_NSUB - 1, 1))

    @pl.when(jnp.logical_and(t < MOE_T1, n_i > 0))
    def _gate_up():
        wb_ref[...] = wgu_ref[...].astype(BF16)

        @sub_blocks
        def _(s0, ns):
                for s in range(s0, min(s0 + ns, MOE_ISUB)):
                    issue_gate_up(t, s)
                rows = slice(s0 * MOE_SB, (s0 + ns) * MOE_SB)
                x_lo, x_hi = _unpack_halves(xbuf[slot, rows, :])
                h = (_dot(x_lo.astype(BF16), wb_ref[0:half, :])
                     + _dot(x_hi.astype(BF16), wb_ref[half:, :]) + bgu_ref[...])

                @pl.when(t % 2 == 0)
                def _():
                    hg_ref[rows, :] = jnp.minimum(h, SWIGLU_LIMIT)

                @pl.when(t % 2 == 1)
                def _():
                    hg = hg_ref[rows, :]
                    hl = jnp.clip(h, -SWIGLU_LIMIT, SWIGLU_LIMIT)
                    act_ref[t // 2, rows, :] = (hg * jax.nn.sigmoid(SWIGLU_ALPHA * hg) * (hl + 1.0)).astype(BF16)

    @pl.when(jnp.logical_and(t >= MOE_T1, n_i > 0))
    def _down():
        t2 = t - MOE_T1
        oslot = t2 % 2
        wdl = wdl_ref[...].astype(BF16)
        wdh = wdh_ref[...].astype(BF16)
        col0 = pl.multiple_of(t2 * MOE_TN, MOE_TN)

        def result_copy(slot_, s):
            return pltpu.make_async_copy(
                obuf.at[slot_, pl.ds(s * MOE_SB, MOE_SB)],
                y_hbm.at[pl.ds(pl.multiple_of(prow_ref[i] + s * MOE_SB, MOE_SB), MOE_SB),
                         pl.ds(col0, MOE_TN)], osem.at[slot_])

        @sub_blocks
        def _(s0, ns):
                for s in range(s0, min(s0 + ns, MOE_ISUB)):
                    issue_down(t2, s)
                rows = slice(s0 * MOE_SB, (s0 + ns) * MOE_SB)
                y_lo = bdl_ref[...]
                y_hi = bdh_ref[...]
                for k in range(D_EXPERT // MOE_KD):
                    a = act_ref[k, rows, :]
                    y_lo = y_lo + _dot(a, wdl[k * MOE_KD:(k + 1) * MOE_KD, :])
                    y_hi = y_hi + _dot(a, wdh[k * MOE_KD:(k + 1) * MOE_KD, :])
                obuf[oslot, rows, :] = _pack_halves(jnp.concatenate([y_lo, y_hi], axis=1))
                for s in range(s0, s0 + ns):
                    result_copy(oslot, s).start()

        for s in range(MOE_NSUB):
            @pl.when(jnp.logical_and(s < nsub_i, t2 >= 1))
            def _():
                result_copy(1 - oslot, s).wait()

            @pl.when(jnp.logical_and(s < nsub_i, t2 == MOE_T2 - 1))
            def _():
                result_copy(oslot, s).wait()

    @pl.when(jnp.logical_and(t == MOE_T1 + MOE_T2 - 1, n_i > 0))
    def _missing_sub_block_shares():
        for s in range(1, MOE_ISUB):
            @pl.when(s >= nsub_i)
            def _():
                def gate_up_share(t1, c):
                    issue_gate_up(t1, s)
                    return c

                def down_share(t2, c):
                    issue_down(t2, s)
                    return c
                lax.fori_loop(0, MOE_T1, gate_up_share, 0)
                lax.fori_loop(0, MOE_T2, down_share, 0)

    @pl.when(jnp.logical_and(i == nv_ref[0] - 1, t == MOE_T1 + MOE_T2 - 1))
    def _zero_tail():
        zslot = 1 - slot
        slot_rows(zslot).wait()
        xbuf[zslot, 0:MOE_SB, :] = jnp.zeros((MOE_SB, half), U32)

        def tail_copy(j):
            return pltpu.make_async_copy(
                xbuf.at[zslot, pl.ds(0, MOE_SB)],
                y_hbm.at[pl.ds(pl.multiple_of(j * MOE_SB, MOE_SB), MOE_SB)], xsem.at[zslot])

        def start(j, c):
            tail_copy(j).start()
            return c

        def wait(j, c):
            tail_copy(j).wait()
            return c

        first = nv_ref[1] // MOE_SB
        total = y_hbm.shape[0] // MOE_SB
        lax.fori_loop(first, total, start, 0)
        lax.fori_loop(first, total, wait, 0)


def _moe_experts(xq, blk_exp, blk_n, blk_prow, n_valid_blocks, ids, w_gu, b_gu, w_down, b_down, layer,
                 n_rows_out):
    nb = ids.shape[0]
    t_last = MOE_T1 + MOE_T2 - 1
    half = D_MODEL // 2
    hi_blk = half // MOE_TN

    def src(i, nv):
        return jnp.maximum(jnp.minimum(i, nv[0] - 1), 0)

    def tt(i, t, nv):
        return jnp.where(i < nv[0], t, t_last)

    def t1(i, t, nv):
        return jnp.minimum(tt(i, t, nv), MOE_T1 - 1)

    def t2(i, t, nv):
        return jnp.maximum(tt(i, t, nv) - MOE_T1, 0)

    ids_cur_map = lambda i, t, e, n, p, nv: (src(i, nv), 0, 0)
    ids_nxt_map = lambda i, t, e, n, p, nv: (jnp.minimum(src(i, nv) + 1, nb - 1), 0, 0)
    def gu_col(i, t, nv):
        tt1 = t1(i, t, nv)
        return (tt1 % 2) * (D_EXPERT // MOE_TH) + tt1 // 2

    wgu_map = lambda i, t, e, n, p, nv: (layer, e[i], 0, gu_col(i, t, nv))
    wdl_map = lambda i, t, e, n, p, nv: (layer, e[i], 0, t2(i, t, nv))
    wdh_map = lambda i, t, e, n, p, nv: (layer, e[i], 0, hi_blk + t2(i, t, nv))
    grid_spec = pltpu.PrefetchScalarGridSpec(
        num_scalar_prefetch=4,
        grid=(nb, MOE_T1 + MOE_T2),
        in_specs=[
            pl.BlockSpec((None, 1, MOE_TM), ids_cur_map, memory_space=pltpu.SMEM),
            pl.BlockSpec((None, 1, MOE_TM), ids_nxt_map, memory_space=pltpu.SMEM),
            pl.BlockSpec(memory_space=pl.ANY),
            pl.BlockSpec((None, None, D_MODEL, MOE_TH), wgu_map),
            pl.BlockSpec((None, None, 1, MOE_TH), wgu_map),
            pl.BlockSpec((None, None, D_EXPERT, MOE_TN), wdl_map),
            pl.BlockSpec((None, None, D_EXPERT, MOE_TN), wdh_map),
            pl.BlockSpec((None, None, 1, MOE_TN), wdl_map),
            pl.BlockSpec((None, None, 1, MOE_TN), wdh_map),
        ],
        out_specs=pl.BlockSpec(memory_space=pl.ANY),
        scratch_shapes=[
            pltpu.VMEM((2, MOE_TM, half), U32),
            pltpu.VMEM((D_EXPERT // MOE_KD, MOE_TM, MOE_KD), BF16),
            pltpu.VMEM((MOE_TM, MOE_TH), F32),
            pltpu.VMEM((D_MODEL, MOE_TH), BF16),
            pltpu.VMEM((2, MOE_TM, MOE_TN), U32),
            pltpu.SemaphoreType.DMA((2,)),
            pltpu.SemaphoreType.DMA((2,)),
        ],
    )
    bgu = b_gu.reshape(DEPTH, N_EXPERTS, 1, 2 * D_EXPERT)
    ids3 = ids.reshape(nb, 1, MOE_TM)
    bdn = b_down.reshape(DEPTH, N_EXPERTS, 1, D_MODEL)
    return pl.pallas_call(
        _moe_kernel,
        grid_spec=grid_spec,
        out_shape=jax.ShapeDtypeStruct((n_rows_out, half), U32),
        compiler_params=_params("arbitrary", "arbitrary"),
        name="moe_experts",
    )(blk_exp, blk_n, blk_prow, n_valid_blocks, ids3, ids3, xq, w_gu, bgu,
      w_down, w_down, bdn, bdn)


LNC_TM = 128


def _ln_combine_kernel(ids_cur, ids_nxt, h_ref, gate_ref, g_ref, b_ref, y_hbm, ho_ref, hb_ref, ybuf, sem):
    i = pl.program_id(0)
    tm = h_ref.shape[0]
    w = h_ref.shape[1] // 2
    rows = TOP_K * tm
    slot = i % 2

    def row_copy(ids_ref, slot_, idx):
        return pltpu.make_async_copy(y_hbm.at[pl.ds(ids_ref[0, idx], 1)],
                                     ybuf.at[slot_, pl.ds(idx, 1)], sem.at[slot_])

    def tile_rows(slot_):
        return pltpu.make_async_copy(y_hbm.at[pl.ds(0, rows)], ybuf.at[slot_], sem.at[slot_])

    @pl.when(i == 0)
    def _():
        def body(u, c):
            for r in range(MOE_UNIT):
                row_copy(ids_cur, 0, u * MOE_UNIT + r).start()
            return c
        lax.fori_loop(0, rows // MOE_UNIT, body, 0)

    tile_rows(slot).wait()
    for idx in range(rows):
        row_copy(ids_nxt, 1 - slot, idx).start()

    x_lo = DEEPNORM_ALPHA * h_ref[:, :w]
    x_hi = DEEPNORM_ALPHA * h_ref[:, w:]
    for k in range(TOP_K):
        lo, hi = _unpack_halves(ybuf[slot, k * tm:(k + 1) * tm, :])
        gk = gate_ref[:, k:k + 1]
        x_lo = x_lo + gk * lo
        x_hi = x_hi + gk * hi
    x = jnp.concatenate([x_lo, x_hi], axis=1)
    mu = jnp.mean(x, axis=-1, keepdims=True)
    xc = x - mu
    var = jnp.mean(xc * xc, axis=-1, keepdims=True)
    hn = xc * lax.rsqrt(var + LN_EPS) * g_ref[...] + b_ref[...]
    ho_ref[...] = hn
    hb_ref[...] = hn.astype(BF16)

    @pl.when(i == pl.num_programs(0) - 1)
    def _():
        tile_rows(1 - slot).wait()


def _ln_combine(h, yq, dest_tiles, gates, ln_g, ln_b):
    n_tok, d = h.shape
    tm = LNC_TM
    nblk = n_tok // tm
    row = lambda i: (i, 0)
    fixed = lambda i: (0, 0)
    dest3 = dest_tiles.reshape(nblk, 1, TOP_K * tm)
    return pl.pallas_call(
        _ln_combine_kernel,
        grid=(nblk,),
        in_specs=[
            pl.BlockSpec((None, 1, TOP_K * tm), lambda i: (i, 0, 0), memory_space=pltpu.SMEM),
            pl.BlockSpec((None, 1, TOP_K * tm), lambda i: (jnp.minimum(i + 1, nblk - 1), 0, 0),
                         memory_space=pltpu.SMEM),
            pl.BlockSpec((tm, d), row),
            pl.BlockSpec((tm, SLOT_LANES), row),
            pl.BlockSpec((1, d), fixed), pl.BlockSpec((1, d), fixed),
            pl.BlockSpec(memory_space=pl.ANY),
        ],
        out_specs=[pl.BlockSpec((tm, d), row), pl.BlockSpec((tm, d), row)],
        out_shape=[jax.ShapeDtypeStruct((n_tok, d), F32), jax.ShapeDtypeStruct((n_tok, d), BF16)],
        scratch_shapes=[pltpu.VMEM((2, TOP_K * tm, d // 2), U32), pltpu.SemaphoreType.DMA((2,))],
        compiler_params=_params("arbitrary"),
        name="ln_combine",
    )(dest3, dest3, h, gates, ln_g.reshape(1, d), ln_b.reshape(1, d), yq)


def _moe_layer(h, hq, top_idx, gates, rank, counts, w_gu, b_gu, w_down, b_down, layer, ln_g, ln_b):
    n_tok = h.shape[0]
    n_assign = n_tok * TOP_K
    nb = N_EXPERTS + n_assign // MOE_TM
    n_rows_out = n_assign + N_EXPERTS * MOE_SB
    i32 = jnp.int32
    cnt = counts[0].astype(i32)
    pcnt = (cnt + MOE_SB - 1) // MOE_SB * MOE_SB
    pstart = jnp.cumsum(pcnt) - pcnt
    nblk = (cnt + MOE_TM - 1) // MOE_TM
    bend = jnp.cumsum(nblk)
    bfirst = bend - nblk
    n_valid = bend[-1]
    e_flat = top_idx[:, :TOP_K].reshape(-1)
    rank_flat = rank[:, :TOP_K].reshape(-1)
    tok = jnp.arange(n_assign, dtype=i32) // TOP_K
    dest = pstart[e_flat] + rank_flat
    blk = jnp.arange(nb, dtype=i32)
    src = jnp.maximum(jnp.minimum(blk, n_valid - 1), 0)
    be = jnp.minimum(jnp.sum((bend[None, :] <= src[:, None]).astype(i32), axis=1), N_EXPERTS - 1)
    b_in = src - bfirst[be]
    blk_n = jnp.where(blk < n_valid, jnp.clip(cnt[be] - b_in * MOE_TM, 0, MOE_TM), 0).astype(i32)
    blk_prow = (pstart[be] + b_in * MOE_TM).astype(i32)
    id_pos = (bfirst[e_flat] + rank_flat // MOE_TM) * MOE_TM + rank_flat % MOE_TM
    ids = jnp.zeros((nb * MOE_TM,), i32).at[id_pos].set(tok).reshape(nb, MOE_TM)
    used_rows = pstart[-1] + pcnt[-1]
    scalars = jnp.stack([n_valid, used_rows]).astype(i32)

    def experts(nb_run):
        return lambda: _moe_experts(hq, be.astype(i32)[:nb_run], blk_n[:nb_run], blk_prow[:nb_run], scalars,
                                    ids[:nb_run], w_gu, b_gu, w_down, b_down, layer, n_rows_out)

    yq = lax.cond(n_valid <= MOE_NB_SHORT, experts(MOE_NB_SHORT), experts(nb))
    tm = LNC_TM
    dest_tiles = dest.reshape(n_tok // tm, tm, TOP_K).transpose(0, 2, 1).reshape(n_tok // tm, TOP_K * tm)
    return _ln_combine(h, yq, dest_tiles, gates, ln_g, ln_b)


def kernel(x, positions, ln_g, ln_b, hy_w_in, hy_w_out, gm_ln_g, gm_ln_b, gm_w_s, gm_b_s, ssm_lam_re, ssm_lam_im, ssm_log_dt, ssm_b_re, ssm_b_im, ssm_c_re, ssm_c_im, ssm_d, ssm_w_glu, ssm_b_glu, mla_w_in, mla_q_norm_g, mla_kv_norm_g, mla_w_uq, mla_w_ukv, mla_w_o, moe_w_router, moe_b_router, moe_w_gu, moe_b_gu, moe_w_down, moe_b_down):
    batch, seq, d = x.shape
    n_tok = batch * seq
    h = x.reshape(n_tok, d)
    hb = h.astype(BF16)
    for layer in range(DEPTH):
        i = layer // 2
        if layer % 2 == 0:
            z_gm = _mm([hb], hy_w_in[i], col0=0, n_cols=2 * GM_WIDTH, tm=1024, tn=256,
                       out_dtype=BF16, epilogue="gelu", name="hy_in_gm")
            z_ssm = _mm([hb], hy_w_in[i], col0=2 * GM_WIDTH, n_cols=SSM_WIDTH, tm=1024, tn=256,
                        out_dtype=F32, name="hy_in_ssm")
            y_gm = _spatial_gating(z_gm, gm_ln_g[i], gm_ln_b[i], gm_w_s[i], gm_b_s[i])
            tables = _s5_tables(ssm_lam_re[i], ssm_lam_im[i], ssm_log_dt[i], ssm_b_re[i], ssm_b_im[i],
                                ssm_c_re[i], ssm_c_im[i], ssm_d[i], seq)
            y_act = _s5_mixer(z_ssm, batch, seq, tables)
            y_ssm = _mm([y_act], ssm_w_glu[i], tm=1024, tn=512, out_dtype=BF16, epilogue="glu",
                        bias=ssm_b_glu[i], mul=y_act, name="s5_glu")
            mix = _mm([y_gm, y_ssm], hy_w_out[i], tm=1024, tn=256, out_dtype=F32, name="hy_out")
        else:
            w_in = mla_w_in[i]
            n_main = MLA_Q_RANK + MLA_KV_RANK
            c_main = _mm([hb], w_in, col0=0, n_cols=n_main, tm=1024, tn=256, out_dtype=F32,
                         name="mla_in")
            k_rope = _mm([hb], w_in[:, n_main:], tm=1024, tn=MLA_ROPE, out_dtype=F32, name="mla_in_rope")
            tab, k_rope = _rope_tables(positions, k_rope)
            q = _mm([(c_main, MLA_Q_RANK, 0)], mla_w_uq[i], tm=1024, tn=768, out_dtype=BF16,
                    prologue="rms", gain=mla_q_norm_g[i], name="mla_uq")
            kv = _mm([(c_main, MLA_KV_RANK, MLA_Q_RANK // MLA_KV_RANK)], mla_w_ukv[i], tm=1024, tn=1024,
                     out_dtype=BF16, prologue="rms", gain=mla_kv_norm_g[i], name="mla_ukv")
            o = _attention(q, kv, k_rope, tab, batch, seq)
            mix = _mm([o], mla_w_o[i], tm=1024, tn=256, out_dtype=F32, name="mla_out")
        h, hq, top_idx, gates, rank, counts = _ln_router(
            h, mix, ln_g[layer, 0], ln_b[layer, 0], moe_w_router[layer], moe_b_router[layer])
        h, hb = _moe_layer(h, hq, top_idx, gates, rank, counts, moe_w_gu, moe_b_gu,
                           moe_w_down, moe_b_down, layer, ln_g[layer, 1], ln_b[layer, 1])
    return h.reshape(batch, seq, d)
```

```python
import functools
import math

import numpy as np
import jax
import jax.numpy as jnp
from jax import lax
from jax.experimental import pallas as pl
from jax.experimental.pallas import tpu as pltpu

F32 = jnp.float32
BF16 = jnp.bfloat16
U32 = jnp.uint32
HI_MASK = np.uint32(0xFFFF0000)

D_MODEL = 4096
DEPTH = 2
CHUNK = 64
DEEPNORM_ALPHA = (2 * DEPTH) ** 0.25
LN_EPS = 1e-5
RMS_EPS = 1e-6
LANES = 128

GM_WIDTH = 2048
GM_GROUPS = 8
GM_GROUP_DIM = 256
GM_BLOCK = 128
SSM_WIDTH = 2048
SSM_P = 16
SSM_GROUPS = 128
SSM_N = 64
SSM_T = 16
SSM_OCT = LANES // SSM_P
SSM_NOCT = SSM_GROUPS // SSM_OCT
SSM_PAIRS = SSM_OCT // 2
SSM_SW = SSM_OCT * SSM_N

MLA_HEADS = 32
MLA_Q_RANK = 1024
MLA_KV_RANK = 512
MLA_NOPE = 128
MLA_ROPE = 64
MLA_V = 128
ROPE_THETA = 10000.0

N_EXPERTS = 32
TOP_K = 4
D_EXPERT = 1024
SWIGLU_LIMIT = 7.0
SWIGLU_ALPHA = 1.702

VMEM_LIMIT_BYTES = 56 * 1024 * 1024


def _params(*sem):
    return pltpu.CompilerParams(dimension_semantics=sem, vmem_limit_bytes=VMEM_LIMIT_BYTES)


def _dot(a, b):
    return jnp.dot(a, b, preferred_element_type=F32)


def _dot_nt(a, b):
    return lax.dot_general(a, b, (((1,), (1,)), ((), ())), preferred_element_type=F32)


def _mm_kernel(*refs, prologue, epilogue, n_x):
    it = iter(refs)
    x_refs = [next(it) for _ in range(n_x)]
    w_ref = next(it)
    g_ref = next(it) if prologue == "rms" else None
    b_ref = next(it) if epilogue == "glu" else None
    y_ref = next(it) if epilogue == "glu" else None
    o_ref = next(it)
    xs_ref = next(it) if prologue == "rms" else None

    if prologue == "rms":
        @pl.when(pl.program_id(1) == 0)
        def _():
            xf = x_refs[0][...].astype(F32)
            ms = jnp.mean(xf * xf, axis=-1, keepdims=True)
            xs_ref[...] = (xf * lax.rsqrt(ms + RMS_EPS) * g_ref[...]).astype(BF16)
        xs = [xs_ref[...]]
    else:
        xs = [r[...].astype(BF16) for r in x_refs]

    acc = None
    k0 = 0
    for x in xs:
        kk = x.shape[1]
        part = _dot(x, w_ref[k0:k0 + kk, :].astype(BF16))
        acc = part if acc is None else acc + part
        k0 += kk
    if epilogue == "gelu":
        acc = jax.nn.gelu(acc)
    elif epilogue == "glu":
        acc = y_ref[...].astype(F32) * jax.nn.sigmoid(acc + b_ref[...])
    o_ref[...] = acc.astype(o_ref.dtype)


def _mm(xs, w, *, col0=0, n_cols=None, tm, tn, out_dtype, prologue=None, gain=None,
        epilogue=None, bias=None, mul=None, name):
    xs = [(x, x.shape[1], 0) if not isinstance(x, tuple) else x for x in xs]
    m = xs[0][0].shape[0]
    k_total = w.shape[0]
    n_cols = w.shape[1] - col0 if n_cols is None else n_cols
    tn = min(tn, n_cols)
    assert m % tm == 0 and n_cols % tn == 0 and col0 % tn == 0
    cb0 = col0 // tn
    assert sum(kw for _, kw, _ in xs) == k_total
    in_specs = [pl.BlockSpec((tm, kw), functools.partial(lambda i, j, cb: (i, cb), cb=cb))
                for _, kw, cb in xs]
    in_specs.append(pl.BlockSpec((k_total, tn), lambda i, j: (0, cb0 + j)))
    args = [x for x, _, _ in xs] + [w]
    scratch = []
    if prologue == "rms":
        in_specs.append(pl.BlockSpec((1, k_total), lambda i, j: (0, 0)))
        args.append(gain.reshape(1, k_total))
        scratch.append(pltpu.VMEM((tm, k_total), BF16))
    if epilogue == "glu":
        in_specs.append(pl.BlockSpec((1, tn), lambda i, j: (0, j)))
        in_specs.append(pl.BlockSpec((tm, tn), lambda i, j: (i, j)))
        args += [bias.reshape(1, n_cols), mul]
    return pl.pallas_call(
        functools.partial(_mm_kernel, prologue=prologue, epilogue=epilogue, n_x=len(xs)),
        grid=(m // tm, n_cols // tn),
        in_specs=in_specs,
        out_specs=pl.BlockSpec((tm, tn), lambda i, j: (i, j)),
        out_shape=jax.ShapeDtypeStruct((m, n_cols), out_dtype),
        scratch_shapes=scratch,
        compiler_params=_params("arbitrary", "arbitrary"),
        name=name,
    )(*args)


def _gating_kernel(u_ref, v_ref, g_ref, b_ref, ws_ref, bst_ref, o_ref):
    row = lax.broadcasted_iota(jnp.int32, (GM_BLOCK, GM_BLOCK), 0)
    col = lax.broadcasted_iota(jnp.int32, (GM_BLOCK, GM_BLOCK), 1)
    visible = (col // CHUNK) <= (row // CHUNK)
    for g in range(GM_GROUPS):
        sl = slice(g * GM_GROUP_DIM, (g + 1) * GM_GROUP_DIM)
        v = v_ref[:, sl].astype(F32)
        mu = jnp.mean(v, axis=-1, keepdims=True)
        vc = v - mu
        var = jnp.mean(vc * vc, axis=-1, keepdims=True)
        vn = vc * lax.rsqrt(var + LN_EPS) * g_ref[:, sl] + b_ref[:, sl]
        w = jnp.where(visible, ws_ref[g], 0.0).astype(BF16)
        s = _dot(w, vn.astype(BF16)) + bst_ref[:, g:g + 1]
        o_ref[:, sl] = (u_ref[:, sl].astype(F32) * s).astype(o_ref.dtype)


def _spatial_gating(z_gm, ln_g, ln_b, w_s, b_s):
    n_tok = z_gm.shape[0]
    return pl.pallas_call(
        _gating_kernel,
        grid=(n_tok // GM_BLOCK,),
        in_specs=[
            pl.BlockSpec((GM_BLOCK, GM_WIDTH), lambda i: (i, 0)),
            pl.BlockSpec((GM_BLOCK, GM_WIDTH), lambda i: (i, 1)),
            pl.BlockSpec((1, GM_WIDTH), lambda i: (0, 0)),
            pl.BlockSpec((1, GM_WIDTH), lambda i: (0, 0)),
            pl.BlockSpec((GM_GROUPS, GM_BLOCK, GM_BLOCK), lambda i: (0, 0, 0)),
            pl.BlockSpec((GM_BLOCK, GM_GROUPS), lambda i: (0, 0)),
        ],
        out_specs=pl.BlockSpec((GM_BLOCK, GM_WIDTH), lambda i: (i, 0)),
        out_shape=jax.ShapeDtypeStruct((n_tok, GM_WIDTH), BF16),
        compiler_params=_params("arbitrary"),
        name="spatial_gating",
    )(z_gm, z_gm, ln_g.reshape(1, GM_WIDTH), ln_b.reshape(1, GM_WIDTH), w_s, b_s.T)


def _s5_tables(lam_re, lam_im, log_dt, b_re, b_im, c_re, c_im, d_skip, seq):
    hi = lax.Precision.HIGHEST
    n, p, t = SSM_N, SSM_P, SSM_T
    no, npr = SSM_NOCT, SSM_PAIRS
    dt = jnp.exp(log_dt)[:, None]
    mag = jnp.exp(lam_re * dt)
    ab_re = mag * jnp.cos(lam_im * dt)
    ab_im = mag * jnp.sin(lam_im * dt)
    den = lam_re * lam_re + lam_im * lam_im
    num_re = ab_re - 1.0
    coef_re = (num_re * lam_re + ab_im * lam_im) / den
    coef_im = (ab_im * lam_re - num_re * lam_im) / den
    bb_re = coef_re[..., None] * b_re - coef_im[..., None] * b_im
    bb_im = coef_re[..., None] * b_im + coef_im[..., None] * b_re

    def power(k):
        k = jnp.asarray(k, F32)[..., None, None]
        mk = jnp.exp(k * (lam_re * dt))
        return mk * jnp.cos(k * (lam_im * dt)), mk * jnp.sin(k * (lam_im * dt))

    pw_re, pw_im = power(np.arange(t + 1))
    cp_re = c_re[None] * pw_re[:, :, None, :] - c_im[None] * pw_im[:, :, None, :]
    cp_im = c_re[None] * pw_im[:, :, None, :] + c_im[None] * pw_re[:, :, None, :]
    kern = (jnp.einsum('tgpn,gnq->gtpq', cp_re[:t], bb_re, precision=hi)
            - jnp.einsum('tgpn,gnq->gtpq', cp_im[:t], bb_im, precision=hi))
    eye_o = jnp.eye(SSM_OCT, dtype=F32)
    eye_2 = jnp.eye(2, dtype=F32)
    kbd = jnp.einsum('oatpq,ab->otaqbp', kern.reshape(no, SSM_OCT, t, p, p), eye_o)
    kbd = kbd.reshape(no, t, LANES, LANES)
    rv_re, rv_im = pw_re[t - 1::-1], pw_im[t - 1::-1]
    st_re = rv_re[..., None] * bb_re[None] - rv_im[..., None] * bb_im[None]
    st_im = rv_re[..., None] * bb_im[None] + rv_im[..., None] * bb_re[None]
    st = jnp.stack([st_re, st_im], 0).reshape(2, t, no, npr, 2, n, p)
    w_st = jnp.einsum('rjoxanq,ab->ojxraqbn', st, eye_2).reshape(no, t, npr, 2, 2 * p, 2 * n)
    wc = jnp.stack([cp_re[1:], -cp_im[1:]], 0).reshape(2, t, no, npr, 2, p, n)
    w_ct = jnp.einsum('rioxapn,ab->oixrapbn', wc, eye_2).reshape(no, t, npr, 2, 2 * p, 2 * n)
    n_lvl = int(math.log2(seq // t))
    lv_re, lv_im = power(t * 2 ** np.arange(n_lvl))
    lr = lv_re.reshape(n_lvl, no, SSM_SW)
    li = lv_im.reshape(n_lvl, no, SSM_SW)
    lvl = jnp.stack([jnp.concatenate([lr, lr], -1), jnp.concatenate([-li, li], -1)], axis=1)
    lvl = lvl.transpose(2, 0, 1, 3).reshape(no, 2 * n_lvl, 2 * SSM_SW)
    d_vec = jnp.tile(d_skip.reshape(no, 1, LANES), (1, 1, t))
    return kbd.astype(BF16), w_st.astype(BF16), w_ct.astype(BF16), lvl, d_vec


def _s5_kernel(u_ref, kbd_ref, wst_ref, wct_ref, lvl_ref, d_ref, o_ref, toe_ref, st_ref, ct_ref,
               *, n_chunks, n_lvl):
    t = SSM_T

    @pl.when(pl.program_id(0) == 0)
    def _():
        toe_ref[...] = jnp.zeros_like(toe_ref)
        st_ref[...] = jnp.zeros_like(st_ref)
        ct_ref[...] = jnp.zeros_like(ct_ref)

    for i in range(t):
        for j in range(i + 1):
            toe_ref[j * LANES:(j + 1) * LANES, i * LANES:(i + 1) * LANES] = kbd_ref[i - j]
    for j in range(t):
        for x in range(SSM_PAIRS):
            for r in range(2):
                rows = slice(j * LANES + x * 2 * SSM_P, j * LANES + (x + 1) * 2 * SSM_P)
                lanes = slice(r * SSM_SW + x * LANES, r * SSM_SW + (x + 1) * LANES)
                st_ref[rows, lanes] = wst_ref[j, x, r]
                ct_ref[rows, lanes] = wct_ref[j, x, r]

    u = u_ref[...]
    ub = u.astype(BF16)
    h = _dot(ub, st_ref[...])
    cidx = lax.broadcasted_iota(jnp.int32, (u.shape[0], 1), 0) % n_chunks
    for lv in range(n_lvl):
        d = 1 << lv
        a_rr = lvl_ref[2 * lv:2 * lv + 1, :]
        a_is = lvl_ref[2 * lv + 1:2 * lv + 2, :]
        sh = jnp.where(cidx >= d, pltpu.roll(h, d, 0), 0.0)
        h = h + sh * a_rr + pltpu.roll(sh, SSM_SW, 1) * a_is
    h_prev = jnp.where(cidx >= 1, pltpu.roll(h, 1, 0), 0.0)
    y = _dot(ub, toe_ref[...]) + _dot_nt(h_prev.astype(BF16), ct_ref[...]) + d_ref[...] * u
    o_ref[...] = jax.nn.gelu(y).astype(o_ref.dtype)


def _s5_mixer(z_ssm, batch, seq, tables):
    kbd, w_st, w_ct, lvl, d_vec = tables
    t, no = SSM_T, SSM_NOCT
    n_chunks = seq // t
    rows = batch * n_chunks
    width = t * LANES
    n_lvl = lvl.shape[1] // 2
    u = z_ssm.reshape(rows, t, no, LANES).transpose(2, 0, 1, 3).reshape(no, rows, width)
    y = pl.pallas_call(
        functools.partial(_s5_kernel, n_chunks=n_chunks, n_lvl=n_lvl),
        grid=(no,),
        in_specs=[
            pl.BlockSpec((None, rows, width), lambda i: (i, 0, 0)),
            pl.BlockSpec((None, t, LANES, LANES), lambda i: (i, 0, 0, 0)),
            pl.BlockSpec((None, t, SSM_PAIRS, 2, 2 * SSM_P, LANES), lambda i: (i, 0, 0, 0, 0, 0)),
            pl.BlockSpec((None, t, SSM_PAIRS, 2, 2 * SSM_P, LANES), lambda i: (i, 0, 0, 0, 0, 0)),
            pl.BlockSpec((None, 2 * n_lvl, 2 * SSM_SW), lambda i: (i, 0, 0)),
            pl.BlockSpec((None, 1, width), lambda i: (i, 0, 0)),
        ],
        out_specs=pl.BlockSpec((None, rows, width), lambda i: (i, 0, 0)),
        out_shape=jax.ShapeDtypeStruct((no, rows, width), BF16),
        scratch_shapes=[
            pltpu.VMEM((width, width), BF16),
            pltpu.VMEM((width, 2 * SSM_SW), BF16),
            pltpu.VMEM((width, 2 * SSM_SW), BF16),
        ],
        compiler_params=_params("arbitrary"),
        name="s5_mixer",
    )(u, kbd, w_st, w_ct, lvl, d_vec)
    return y.reshape(no, rows, t, LANES).transpose(1, 2, 0, 3).reshape(batch * seq, SSM_WIDTH)


def _rope_kernel(pos_ref, freq_ref, kr_ref, tab_ref, kro_ref):
    ang = pos_ref[...].astype(F32) * freq_ref[...]
    lane = lax.broadcasted_iota(jnp.int32, ang.shape, 1)
    sin = jnp.sin(ang)
    tab = jnp.where(lane < MLA_ROPE, jnp.cos(ang), jnp.where(lane < MLA_ROPE + MLA_ROPE // 2, -sin, sin))
    tab_ref[...] = tab
    kr = kr_ref[...]
    half = MLA_ROPE // 2
    sw = jnp.concatenate([kr[:, half:], kr[:, :half]], axis=1)
    kro_ref[...] = (kr * tab[:, :MLA_ROPE] + sw * tab[:, MLA_ROPE:]).astype(kro_ref.dtype)


def _rope_tables(positions, k_rope):
    n_tok = k_rope.shape[0]
    tm = 512
    inv_freq = ROPE_THETA ** (-np.arange(0, MLA_ROPE, 2, dtype=np.float64) / MLA_ROPE)
    freq = jnp.asarray(np.tile(inv_freq, 4)[None, :], F32)
    return pl.pallas_call(
        _rope_kernel,
        grid=(n_tok // tm,),
        in_specs=[
            pl.BlockSpec((tm, 1), lambda i: (i, 0)),
            pl.BlockSpec((1, 2 * MLA_ROPE), lambda i: (0, 0)),
            pl.BlockSpec((tm, MLA_ROPE), lambda i: (i, 0)),
        ],
        out_specs=[
            pl.BlockSpec((tm, 2 * MLA_ROPE), lambda i: (i, 0)),
            pl.BlockSpec((tm, MLA_ROPE), lambda i: (i, 0)),
        ],
        out_shape=[
            jax.ShapeDtypeStruct((n_tok, 2 * MLA_ROPE), F32),
            jax.ShapeDtypeStruct((n_tok, MLA_ROPE), BF16),
        ],
        compiler_params=_params("arbitrary"),
        name="rope_tables",
    )(positions.reshape(n_tok, 1), freq, k_rope)


ATT_TK = 256
ATT_TQ = 2 * ATT_TK
ATT_DK = 2 * MLA_NOPE


def _attn_kernel(q_ref, tab_ref, k0_ref, v0_ref, k1_ref, v1_ref, kr_ref, o_ref, kc_ref):
    qi = pl.program_id(2)
    hd = MLA_NOPE + MLA_ROPE
    scale = hd ** -0.5 * math.log2(math.e)
    half = MLA_ROPE // 2
    tk = ATT_TK

    @pl.when(qi == 0)
    def _():
        for hh, k_ref in enumerate((k0_ref, k1_ref)):
            kc_ref[hh, :, 0:MLA_NOPE] = k_ref[...]
            kc_ref[hh, :, MLA_NOPE:hd] = kr_ref[...]
            kc_ref[hh, :, hd:] = jnp.zeros((kr_ref.shape[0], ATT_DK - hd), BF16)

    q = q_ref[...]
    cosf = tab_ref[:, :MLA_ROPE]
    sinf = tab_ref[:, MLA_ROPE:]
    qc = []
    for hh in range(2):
        qn = q[:, hh * hd:hh * hd + MLA_NOPE].astype(F32) * scale
        qr = q[:, hh * hd + MLA_NOPE:(hh + 1) * hd].astype(F32)
        sw = jnp.concatenate([qr[:, half:], qr[:, :half]], axis=1)
        qr = (qr * cosf + sw * sinf) * scale
        pad = jnp.zeros((ATT_TQ, ATT_DK - hd), F32)
        qc.append(jnp.concatenate([qn, qr, pad], axis=1).astype(BF16))
    v_refs = (v0_ref, v1_ref)
    row = lax.broadcasted_iota(jnp.int32, (tk, tk), 0)
    col = lax.broadcasted_iota(jnp.int32, (tk, tk), 1)
    visible = (col // CHUNK) <= (row // CHUNK)

    def update(state, hh, sub, j, masked):
        m, l, acc = state
        start = pl.multiple_of(j * tk, tk)
        ks = kc_ref[hh, pl.ds(start, tk), :]
        vs = v_refs[hh][pl.ds(start, tk), :]
        s = _dot_nt(qc[hh][sub * tk:(sub + 1) * tk], ks)
        if masked:
            s = jnp.where(visible, s, -1e30)
        m_new = jnp.maximum(m, jnp.max(s, axis=-1, keepdims=True))
        alpha = jnp.exp2(m - m_new)
        p = jnp.exp2(s - m_new)
        l = alpha * l + jnp.sum(p, axis=-1, keepdims=True)
        acc = alpha * acc + _dot(p.astype(BF16), vs)
        return m_new, l, acc

    chains = [(hh, sub) for hh in range(2) for sub in range(2)]

    def body(jj, states):
        states = tuple(update(st, hh, sub, 2 * jj, False) for st, (hh, sub) in zip(states, chains))
        return tuple(update(st, hh, sub, 2 * jj + 1, False) for st, (hh, sub) in zip(states, chains))

    init = tuple((jnp.full((tk, 1), -1e30, F32), jnp.zeros((tk, 1), F32), jnp.zeros((tk, MLA_V), F32))
                 for _ in chains)
    states = list(lax.fori_loop(0, qi, body, init))
    for c, (hh, sub) in enumerate(chains):
        st = update(states[c], hh, sub, 2 * qi, masked=(sub == 0))
        if sub == 1:
            st = update(st, hh, sub, 2 * qi + 1, masked=True)
        _, l, acc = st
        o_ref[sub * tk:(sub + 1) * tk, hh * MLA_V:(hh + 1) * MLA_V] = (acc / l).astype(o_ref.dtype)


def _attention(q, kv, k_rope, tab, batch, seq):
    n_tok = batch * seq
    nq = seq // ATT_TQ
    kvb = lambda off: pl.BlockSpec((seq, MLA_NOPE), lambda b, hp, qi: (b, 4 * hp + off))
    return pl.pallas_call(
        _attn_kernel,
        grid=(batch, MLA_HEADS // 2, nq),
        in_specs=[
            pl.BlockSpec((ATT_TQ, 2 * (MLA_NOPE + MLA_ROPE)), lambda b, hp, qi: (b * nq + qi, hp)),
            pl.BlockSpec((ATT_TQ, 2 * MLA_ROPE), lambda b, hp, qi: (b * nq + qi, 0)),
            kvb(0), kvb(1), kvb(2), kvb(3),
            pl.BlockSpec((seq, MLA_ROPE), lambda b, hp, qi: (b, 0)),
        ],
        out_specs=pl.BlockSpec((ATT_TQ, 2 * MLA_V), lambda b, hp, qi: (b * nq + qi, hp)),
        out_shape=jax.ShapeDtypeStruct((n_tok, MLA_HEADS * MLA_V), BF16),
        scratch_shapes=[pltpu.VMEM((2, seq, ATT_DK), BF16)],
        compiler_params=_params("arbitrary", "arbitrary", "arbitrary"),
        name="mla_attention",
    )(q, tab, kv, kv, kv, kv, k_rope)


LNR_TM = 256
SLOT_LANES = 128


def _pack_halves(x):
    w = x.shape[1] // 2
    bits = lax.bitcast_convert_type(x.astype(BF16).astype(F32), U32)
    return (bits[:, :w] >> 16) | (bits[:, w:] & HI_MASK)


def _unpack_halves(p):
    return (lax.bitcast_convert_type(p << 16, F32), lax.bitcast_convert_type(p & HI_MASK, F32))


def _ln_router_kernel(h_ref, mix_ref, g_ref, b_ref, wr_ref, br_ref,
                      ho_ref, hq_ref, idx_ref, gate_ref, rank_ref, cnt_ref, carry_ref):
    @pl.when(pl.program_id(0) == 0)
    def _():
        carry_ref[...] = jnp.zeros_like(carry_ref)

    x = DEEPNORM_ALPHA * h_ref[...] + mix_ref[...]
    mu = jnp.mean(x, axis=-1, keepdims=True)
    xc = x - mu
    var = jnp.mean(xc * xc, axis=-1, keepdims=True)
    hn = xc * lax.rsqrt(var + LN_EPS) * g_ref[...] + b_ref[...]
    ho_ref[...] = hn
    hq_ref[...] = _pack_halves(hn)

    logits =jnp.dot(hn, wr_ref[...], precision=lax.Precision.HIGHEST,
                     preferred_element_type=F32) + br_ref[...]
    tm = logits.shape[0]
    lane_e = lax.broadcasted_iota(jnp.int32, (tm, N_EXPERTS), 1)
    lane_s = lax.broadcasted_iota(jnp.int32, (tm, SLOT_LANES), 1)
    work = logits
    sel = jnp.zeros((tm, N_EXPERTS), F32)
    top_v, top_i, hot = [], [], []
    for _ in range(TOP_K):
        mx = jnp.max(work, axis=-1, keepdims=True)
        ix = jnp.min(jnp.where(work == mx, lane_e, N_EXPERTS), axis=-1, keepdims=True)
        oh = lane_e == ix
        work = jnp.where(oh, -jnp.inf, work)
        sel = sel + oh.astype(F32)
        top_v.append(mx)
        top_i.append(ix)
        hot.append(oh)
    ex = [jnp.exp(v - top_v[0]) for v in top_v]
    den = ex[0] + ex[1] + ex[2] + ex[3]

    r = lax.broadcasted_iota(jnp.int32, (tm, tm), 0)
    c = lax.broadcasted_iota(jnp.int32, (tm, tm), 1)
    strict = jnp.where(c < r, 1.0, 0.0).astype(BF16)
    prefix = _dot(strict, sel.astype(BF16)) + carry_ref[0:1, :]
    carry_ref[0:1, :] = carry_ref[0:1, :] + jnp.sum(sel, axis=0, keepdims=True)
    cnt_ref[...] = jnp.broadcast_to(carry_ref[0:1, :], cnt_ref.shape)

    idx_o = jnp.zeros((tm, SLOT_LANES), jnp.int32)
    gate_o = jnp.zeros((tm, SLOT_LANES), F32)
    rank_o = jnp.zeros((tm, SLOT_LANES), jnp.int32)
    for k in range(TOP_K):
        rk = jnp.sum(jnp.where(hot[k], prefix, 0.0), axis=-1, keepdims=True)
        idx_o = jnp.where(lane_s == k, top_i[k], idx_o)
        gate_o = jnp.where(lane_s == k, ex[k] / den, gate_o)
        rank_o = jnp.where(lane_s == k, rk.astype(jnp.int32), rank_o)
    idx_ref[...] = idx_o
    gate_ref[...] = gate_o
    rank_ref[...] = rank_o


def _ln_router(h, mix, ln_g, ln_b, w_router, b_router):
    n_tok, d = h.shape
    tm = LNR_TM
    row = lambda i: (i, 0)
    fixed = lambda i: (0, 0)
    return pl.pallas_call(
        _ln_router_kernel,
        grid=(n_tok // tm,),
        in_specs=[
            pl.BlockSpec((tm, d), row), pl.BlockSpec((tm, d), row),
            pl.BlockSpec((1, d), fixed), pl.BlockSpec((1, d), fixed),
            pl.BlockSpec((d, N_EXPERTS), fixed), pl.BlockSpec((1, N_EXPERTS), fixed),
        ],
        out_specs=[
            pl.BlockSpec((tm, d), row), pl.BlockSpec((tm, d // 2), row),
            pl.BlockSpec((tm, SLOT_LANES), row), pl.BlockSpec((tm, SLOT_LANES), row),
            pl.BlockSpec((tm, SLOT_LANES), row), pl.BlockSpec((8, N_EXPERTS), fixed),
        ],
        out_shape=[
            jax.ShapeDtypeStruct((n_tok, d), F32), jax.ShapeDtypeStruct((n_tok, d // 2), U32),
            jax.ShapeDtypeStruct((n_tok, SLOT_LANES), jnp.int32),
            jax.ShapeDtypeStruct((n_tok, SLOT_LANES), F32),
            jax.ShapeDtypeStruct((n_tok, SLOT_LANES), jnp.int32),
            jax.ShapeDtypeStruct((8, N_EXPERTS), F32),
        ],
        scratch_shapes=[pltpu.VMEM((8, N_EXPERTS), F32)],
        compiler_params=_params("arbitrary"),
        name="ln_router",
    )(h, mix, ln_g.reshape(1, d), ln_b.reshape(1, d), w_router, b_router.reshape(1, N_EXPERTS))


MOE_TM = 1280
MOE_SB = 256
MOE_NSUB = MOE_TM // MOE_SB
MOE_TH = 256
MOE_TN = 512
MOE_T1 = 2 * (D_EXPERT // MOE_TH)
MOE_T2 = D_MODEL // 2 // MOE_TN
MOE_UNIT = 32
MOE_ISUB = 4
MOE_R1 = 32
MOE_R2 = 16
assert MOE_ISUB * (MOE_T1 * MOE_R1 + MOE_T2 * MOE_R2) == MOE_TM
MOE_KD = MOE_TH
MOE_NB_SHORT = N_EXPERTS + 4


def _moe_kernel(e_ref, n_ref, prow_ref, nv_ref, ids_cur, ids_nxt, xq_hbm, wgu_ref, bgu_ref,
                wdl_ref, wdh_ref, bdl_ref, bdh_ref, y_hbm, xbuf, act_ref, hg_ref, wb_ref, obuf, xsem, osem):
    i = pl.program_id(0)
    t = pl.program_id(1)
    n_i = n_ref[i]
    nsub_i = (n_i + MOE_SB - 1) // MOE_SB
    slot = i % 2
    half = D_MODEL // 2

    def issue_rows(ids_ref, slot_, base, count):
        for r in range(count):
            pltpu.make_async_copy(xq_hbm.at[pl.ds(ids_ref[0, base + r], 1)],
                                  xbuf.at[slot_, pl.ds(base + r, 1)], xsem.at[slot_]).start(priority=1)

    def issue_gate_up(t1, s):
        issue_rows(ids_nxt, 1 - slot, (t1 * MOE_ISUB + s) * MOE_R1, MOE_R1)

    def issue_down(t2, s):
        issue_rows(ids_nxt, 1 - slot, MOE_T1 * MOE_ISUB * MOE_R1 + (t2 * MOE_ISUB + s) * MOE_R2, MOE_R2)

    def slot_rows(slot_):
        return pltpu.make_async_copy(xq_hbm.at[pl.ds(0, MOE_TM)], xbuf.at[slot_], xsem.at[slot_])

    @pl.when(jnp.logical_and(i == 0, t == 0))
    def _first_block_rows():
        def body(u, c):
            issue_rows(ids_cur, 0, u * MOE_UNIT, MOE_UNIT)
            return c
        lax.fori_loop(0, MOE_TM // MOE_UNIT, body, 0)

    @pl.when(jnp.logical_and(t == 0, n_i > 0))
    def _wait_rows():
        slot_rows(slot).wait()

    def sub_blocks(body):
        for s0 in range(0, MOE_NSUB - 1, 2):
            pl.when(s0 + 1 < nsub_i)(functools.partial(body, s0, 2))
            pl.when(s0 + 1 == nsub_i)(functools.partial(body, s0, 1))
        if MOE_NSUB % 2:
            pl.when(MOE_NSUB - 1 < nsub_i)(functools.partial(body, MOE_NSUB - 1, 1))

    @pl.when(jnp.logical_and(t < MOE_T1, n_i > 0))
    def _gate_up():
        wb_ref[...] = wgu_ref[...].astype(BF16)

        @sub_blocks
        def _(s0, ns):
                for s in range(s0, min(s0 + ns, MOE_ISUB)):
                    issue_gate_up(t, s)
                rows = slice(s0 * MOE_SB, (s0 + ns) * MOE_SB)
                x_lo, x_hi = _unpack_halves(xbuf[slot, rows, :])
                h = (_dot(x_lo.astype(BF16), wb_ref[0:half, :])
                     + _dot(x_hi.astype(BF16), wb_ref[half:, :]) + bgu_ref[...])

                @pl.when(t % 2 == 0)
                def _():
                    hg_ref[rows, :] = jnp.minimum(h, SWIGLU_LIMIT)

                @pl.when(t % 2 == 1)
                def _():
                    hg = hg_ref[rows, :]
                    hl = jnp.clip(h, -SWIGLU_LIMIT, SWIGLU_LIMIT)
                    act_ref[t // 2, rows, :] = (hg * jax.nn.sigmoid(SWIGLU_ALPHA * hg) * (hl + 1.0)).astype(BF16)

    @pl.when(jnp.logical_and(t >= MOE_T1, n_i > 0))
    def _down():
        t2 = t - MOE_T1
        oslot = t2 % 2
        wdl = wdl_ref[...].astype(BF16)
        wdh = wdh_ref[...].astype(BF16)
        col0 = pl.multiple_of(t2 * MOE_TN, MOE_TN)

        def result_copy(slot_, s):
            return pltpu.make_async_copy(
                obuf.at[slot_, pl.ds(s * MOE_SB, MOE_SB)],
                y_hbm.at[pl.ds(pl.multiple_of(prow_ref[i] + s * MOE_SB, MOE_SB), MOE_SB),
                         pl.ds(col0, MOE_TN)], osem.at[slot_])

        @sub_blocks
        def _(s0, ns):
                for s in range(s0, min(s0 + ns, MOE_ISUB)):
                    issue_down(t2, s)
                rows = slice(s0 * MOE_SB, (s0 + ns) * MOE_SB)
                y_lo = bdl_ref[...]
                y_hi = bdh_ref[...]
                for k in range(D_EXPERT // MOE_KD):
                    a = act_ref[k, rows, :]
                    y_lo = y_lo + _dot(a, wdl[k * MOE_KD:(k + 1) * MOE_KD, :])
                    y_hi = y_hi + _dot(a, wdh[k * MOE_KD:(k + 1) * MOE_KD, :])
                obuf[oslot, rows, :] = _pack_halves(jnp.concatenate([y_lo, y_hi], axis=1))
                for s in range(s0, s0 + ns):
                    result_copy(oslot, s).start()

        for s in range(MOE_NSUB):
            @pl.when(jnp.logical_and(s < nsub_i, t2 >= 1))
            def _():
                result_copy(1 - oslot, s).wait()

            @pl.when(jnp.logical_and(s < nsub_i, t2 == MOE_T2 - 1))
            def _():
                result_copy(oslot, s).wait()

    @pl.when(jnp.logical_and(t == MOE_T1 + MOE_T2 - 1, n_i > 0))
    def _missing_sub_block_shares():
        for s in range(1, MOE_ISUB):
            @pl.when(s >= nsub_i)
            def _():
                def gate_up_share(t1, c):
                    issue_gate_up(t1, s)
                    return c

                def down_share(t2, c):
                    issue_down(t2, s)
                    return c
                lax.fori_loop(0, MOE_T1, gate_up_share, 0)
                lax.fori_loop(0, MOE_T2, down_share, 0)

    @pl.when(jnp.logical_and(i == nv_ref[0] - 1, t == MOE_T1 + MOE_T2 - 1))
    def _zero_tail():
        zslot = 1 - slot
        slot_rows(zslot).wait()
        xbuf[zslot, 0:MOE_SB, :] = jnp.zeros((MOE_SB, half), U32)

        def tail_copy(j):
            return pltpu.make_async_copy(
                xbuf.at[zslot, pl.ds(0, MOE_SB)],
                y_hbm.at[pl.ds(pl.multiple_of(j * MOE_SB, MOE_SB), MOE_SB)], xsem.at[zslot])

        def start(j, c):
            tail_copy(j).start()
            return c

        def wait(j, c):
            tail_copy(j).wait()
            return c

        first = nv_ref[1] // MOE_SB
        total = y_hbm.shape[0] // MOE_SB
        lax.fori_loop(first, total, start, 0)
        lax.fori_loop(first, total, wait, 0)


def _moe_experts(xq, blk_exp, blk_n, blk_prow, n_valid_blocks, ids, w_gu, b_gu, w_down, b_down, layer,
                 n_rows_out):
    nb = ids.shape[0]
    t_last = MOE_T1 + MOE_T2 - 1
    half = D_MODEL // 2
    hi_blk = half // MOE_TN

    def src(i, nv):
        return jnp.maximum(jnp.minimum(i, nv[0] - 1), 0)

    def tt(i, t, nv):
        return jnp.where(i < nv[0], t, t_last)

    def t1(i, t, nv):
        return jnp.minimum(tt(i, t, nv), MOE_T1 - 1)

    def t2(i, t, nv):
        return jnp.maximum(tt(i, t, nv) - MOE_T1, 0)

    ids_cur_map = lambda i, t, e, n, p, nv: (src(i, nv), 0, 0)
    ids_nxt_map = lambda i, t, e, n, p, nv: (jnp.minimum(src(i, nv) + 1, nb - 1), 0, 0)
    def gu_col(i, t, nv):
        tt1 = t1(i, t, nv)
        return (tt1 % 2) * (D_EXPERT // MOE_TH) + tt1 // 2

    wgu_map = lambda i, t, e, n, p, nv: (layer, e[i], 0, gu_col(i, t, nv))
    wdl_map = lambda i, t, e, n, p, nv: (layer, e[i], 0, t2(i, t, nv))
    wdh_map = lambda i, t, e, n, p, nv: (layer, e[i], 0, hi_blk + t2(i, t, nv))
    grid_spec = pltpu.PrefetchScalarGridSpec(
        num_scalar_prefetch=4,
        grid=(nb, MOE_T1 + MOE_T2),
        in_specs=[
            pl.BlockSpec((None, 1, MOE_TM), ids_cur_map, memory_space=pltpu.SMEM),
            pl.BlockSpec((None, 1, MOE_TM), ids_nxt_map, memory_space=pltpu.SMEM),
            pl.BlockSpec(memory_space=pl.ANY),
            pl.BlockSpec((None, None, D_MODEL, MOE_TH), wgu_map),
            pl.BlockSpec((None, None, 1, MOE_TH), wgu_map),
            pl.BlockSpec((None, None, D_EXPERT, MOE_TN), wdl_map),
            pl.BlockSpec((None, None, D_EXPERT, MOE_TN), wdh_map),
            pl.BlockSpec((None, None, 1, MOE_TN), wdl_map),
            pl.BlockSpec((None, None, 1, MOE_TN), wdh_map),
        ],
        out_specs=pl.BlockSpec(memory_space=pl.ANY),
        scratch_shapes=[
            pltpu.VMEM((2, MOE_TM, half), U32),
            pltpu.VMEM((D_EXPERT // MOE_KD, MOE_TM, MOE_KD), BF16),
            pltpu.VMEM((MOE_TM, MOE_TH), F32),
            pltpu.VMEM((D_MODEL, MOE_TH), BF16),
            pltpu.VMEM((2, MOE_TM, MOE_TN), U32),
            pltpu.SemaphoreType.DMA((2,)),
            pltpu.SemaphoreType.DMA((2,)),
        ],
    )
    bgu = b_gu.reshape(DEPTH, N_EXPERTS, 1, 2 * D_EXPERT)
    ids3 = ids.reshape(nb, 1, MOE_TM)
    bdn = b_down.reshape(DEPTH, N_EXPERTS, 1, D_MODEL)
    return pl.pallas_call(
        _moe_kernel,
        grid_spec=grid_spec,
        out_shape=jax.ShapeDtypeStruct((n_rows_out, half), U32),
        compiler_params=_params("arbitrary", "arbitrary"),
        name="moe_experts",
    )(blk_exp, blk_n, blk_prow, n_valid_blocks, ids3, ids3, xq, w_gu, bgu,
      w_down, w_down, bdn, bdn)


LNC_TM = 128


def _ln_combine_kernel(ids_cur, ids_nxt, h_ref, gate_ref, g_ref, b_ref, y_hbm, ho_ref, hb_ref, ybuf, sem):
    i = pl.program_id(0)
    tm = h_ref.shape[0]
    w = h_ref.shape[1] // 2
    rows = TOP_K * tm
    slot = i % 2

    def row_copy(ids_ref, slot_, idx):
        return pltpu.make_async_copy(y_hbm.at[pl.ds(ids_ref[0, idx], 1)],
                                     ybuf.at[slot_, pl.ds(idx, 1)], sem.at[slot_])

    def tile_rows(slot_):
        return pltpu.make_async_copy(y_hbm.at[pl.ds(0, rows)], ybuf.at[slot_], sem.at[slot_])

    @pl.when(i == 0)
    def _():
        def body(u, c):
            for r in range(MOE_UNIT):
                row_copy(ids_cur, 0, u * MOE_UNIT + r).start()
            return c
        lax.fori_loop(0, rows // MOE_UNIT, body, 0)

    tile_rows(slot).wait()
    for idx in range(rows):
        row_copy(ids_nxt, 1 - slot, idx).start(priority=idx % 2)

    x_lo = DEEPNORM_ALPHA * h_ref[:, :w]
    x_hi = DEEPNORM_ALPHA * h_ref[:, w:]
    for k in range(TOP_K):
        lo, hi = _unpack_halves(ybuf[slot, k * tm:(k + 1) * tm, :])
        gk = gate_ref[:, k:k + 1]
        x_lo = x_lo + gk * lo
        x_hi = x_hi + gk * hi
    x = jnp.concatenate([x_lo, x_hi], axis=1)
    mu = jnp.mean(x, axis=-1, keepdims=True)
    xc = x - mu
    var = jnp.mean(xc * xc, axis=-1, keepdims=True)
    hn = xc * lax.rsqrt(var + LN_EPS) * g_ref[...] + b_ref[...]
    ho_ref[...] = hn
    hb_ref[...] = hn.astype(BF16)

    @pl.when(i == pl.num_programs(0) - 1)
    def _():
        tile_rows(1 - slot).wait()


def _ln_combine(h, yq, dest_tiles, gates, ln_g, ln_b):
    n_tok, d = h.shape
    tm = LNC_TM
    nblk = n_tok // tm
    row = lambda i: (i, 0)
    fixed = lambda i: (0, 0)
    dest3 = dest_tiles.reshape(nblk, 1, TOP_K * tm)
    return pl.pallas_call(
        _ln_combine_kernel,
        grid=(nblk,),
        in_specs=[
            pl.BlockSpec((None, 1, TOP_K * tm), lambda i: (i, 0, 0), memory_space=pltpu.SMEM),
            pl.BlockSpec((None, 1, TOP_K * tm), lambda i: (jnp.minimum(i + 1, nblk - 1), 0, 0),
                         memory_space=pltpu.SMEM),
            pl.BlockSpec((tm, d), row),
            pl.BlockSpec((tm, SLOT_LANES), row),
            pl.BlockSpec((1, d), fixed), pl.BlockSpec((1, d), fixed),
            pl.BlockSpec(memory_space=pl.ANY),
        ],
        out_specs=[pl.BlockSpec((tm, d), row), pl.BlockSpec((tm, d), row)],
        out_shape=[jax.ShapeDtypeStruct((n_tok, d), F32), jax.ShapeDtypeStruct((n_tok, d), BF16)],
        scratch_shapes=[pltpu.VMEM((2, TOP_K * tm, d // 2), U32), pltpu.SemaphoreType.DMA((2,))],
        compiler_params=_params("arbitrary"),
        name="ln_combine",
    )(dest3, dest3, h, gates, ln_g.reshape(1, d), ln_b.reshape(1, d), yq)


def _moe_layer(h, hq, top_idx, gates, rank, counts, w_gu, b_gu, w_down, b_down, layer, ln_g, ln_b):
    n_tok = h.shape[0]
    n_assign = n_tok * TOP_K
    nb = N_EXPERTS + n_assign // MOE_TM
    n_rows_out = n_assign + N_EXPERTS * MOE_SB
    i32 = jnp.int32
    cnt = counts[0].astype(i32)
    pcnt = (cnt + MOE_SB - 1) // MOE_SB * MOE_SB
    pstart = jnp.cumsum(pcnt) - pcnt
    nblk = (cnt + MOE_TM - 1) // MOE_TM
    bend = jnp.cumsum(nblk)
    bfirst = bend - nblk
    n_valid = bend[-1]
    e_flat = top_idx[:, :TOP_K].reshape(-1)
    rank_flat = rank[:, :TOP_K].reshape(-1)
    tok = jnp.arange(n_assign, dtype=i32) // TOP_K
    dest = pstart[e_flat] + rank_flat
    blk = jnp.arange(nb, dtype=i32)
    src = jnp.maximum(jnp.minimum(blk, n_valid - 1), 0)
    be = jnp.minimum(jnp.sum((bend[None, :] <= src[:, None]).astype(i32), axis=1), N_EXPERTS - 1)
    b_in = src - bfirst[be]
    blk_n = jnp.where(blk < n_valid, jnp.clip(cnt[be] - b_in * MOE_TM, 0, MOE_TM), 0).astype(i32)
    blk_prow = (pstart[be] + b_in * MOE_TM).astype(i32)
    id_pos = (bfirst[e_flat] + rank_flat // MOE_TM) * MOE_TM + rank_flat % MOE_TM
    ids = jnp.zeros((nb * MOE_TM,), i32).at[id_pos].set(tok).reshape(nb, MOE_TM)
    used_rows = pstart[-1] + pcnt[-1]
    scalars = jnp.stack([n_valid, used_rows]).astype(i32)

    def experts(nb_run):
        return lambda: _moe_experts(hq, be.astype(i32)[:nb_run], blk_n[:nb_run], blk_prow[:nb_run], scalars,
                                    ids[:nb_run], w_gu, b_gu, w_down, b_down, layer, n_rows_out)

    yq = lax.cond(n_valid <= MOE_NB_SHORT, experts(MOE_NB_SHORT), experts(nb))
    tm = LNC_TM
    dest_tiles = dest.reshape(n_tok // tm, tm, TOP_K).transpose(0, 2, 1).reshape(n_tok // tm, TOP_K * tm)
    return _ln_combine(h, yq, dest_tiles, gates, ln_g, ln_b)


def kernel(x, positions, ln_g, ln_b, hy_w_in, hy_w_out, gm_ln_g, gm_ln_b, gm_w_s, gm_b_s, ssm_lam_re, ssm_lam_im, ssm_log_dt, ssm_b_re, ssm_b_im, ssm_c_re, ssm_c_im, ssm_d, ssm_w_glu, ssm_b_glu, mla_w_in, mla_q_norm_g, mla_kv_norm_g, mla_w_uq, mla_w_ukv, mla_w_o, moe_w_router, moe_b_router, moe_w_gu, moe_b_gu, moe_w_down, moe_b_down):
    batch, seq, d = x.shape
    n_tok = batch * seq
    h = x.reshape(n_tok, d)
    hb = h.astype(BF16)
    for layer in range(DEPTH):
        i = layer // 2
        if layer % 2 == 0:
            z_gm = _mm([hb], hy_w_in[i], col0=0, n_cols=2 * GM_WIDTH, tm=1024, tn=256,
                       out_dtype=BF16, epilogue="gelu", name="hy_in_gm")
            z_ssm = _mm([hb], hy_w_in[i], col0=2 * GM_WIDTH, n_cols=SSM_WIDTH, tm=1024, tn=256,
                        out_dtype=F32, name="hy_in_ssm")
            y_gm = _spatial_gating(z_gm, gm_ln_g[i], gm_ln_b[i], gm_w_s[i], gm_b_s[i])
            tables = _s5_tables(ssm_lam_re[i], ssm_lam_im[i], ssm_log_dt[i], ssm_b_re[i], ssm_b_im[i],
                                ssm_c_re[i], ssm_c_im[i], ssm_d[i], seq)
            y_act = _s5_mixer(z_ssm, batch, seq, tables)
            y_ssm = _mm([y_act], ssm_w_glu[i], tm=1024, tn=512, out_dtype=BF16, epilogue="glu",
                        bias=ssm_b_glu[i], mul=y_act, name="s5_glu")
            mix = _mm([y_gm, y_ssm], hy_w_out[i], tm=1024, tn=256, out_dtype=F32, name="hy_out")
        else:
            w_in = mla_w_in[i]
            n_main = MLA_Q_RANK + MLA_KV_RANK
            c_main = _mm([hb], w_in, col0=0, n_cols=n_main, tm=1024, tn=256, out_dtype=F32,
                         name="mla_in")
            k_rope = _mm([hb], w_in[:, n_main:], tm=1024, tn=MLA_ROPE, out_dtype=F32, name="mla_in_rope")
            tab, k_rope = _rope_tables(positions, k_rope)
            q = _mm([(c_main, MLA_Q_RANK, 0)], mla_w_uq[i], tm=1024, tn=768, out_dtype=BF16,
                    prologue="rms", gain=mla_q_norm_g[i], name="mla_uq")
            kv = _mm([(c_main, MLA_KV_RANK, MLA_Q_RANK // MLA_KV_RANK)], mla_w_ukv[i], tm=1024, tn=1024,
                     out_dtype=BF16, prologue="rms", gain=mla_kv_norm_g[i], name="mla_ukv")
            o = _attention(q, kv, k_rope, tab, batch, seq)
            mix = _mm([o], mla_w_o[i], tm=1024, tn=256, out_dtype=F32, name="mla_out")
        h, hq, top_idx, gates, rank, counts = _ln_router(
            h, mix, ln_g[layer, 0], ln_b[layer, 0], moe_w_router[layer], moe_b_router[layer])
        h, hb = _moe_layer(h, hq, top_idx, gates, rank, counts, moe_w_gu, moe_b_gu,
                           moe_w_down, moe_b_down, layer, ln_g[layer, 1], ln_b[layer, 1])
    return h.reshape(batch, seq, d)
```
